```python
import jax
import jax.numpy as jnp
from jax import lax
import numpy as np

D_MODEL = 1024
BATCH = 8
SEQ = 2048
DEPTH = 2
DEC_BATCH = 128
DEC_SEQ = 4
PAST_LEN = 16384
PAGE_SIZE = 128

N_META = 16
CHUNK = 64
EPS = 1e-6
H_A = 8
DK_A = 128
DV_A = 128
CONV_W = 4
CONV_CH = 2 * H_A * DK_A + H_A * DV_A
H_B = 4
DK_B = 128
DV_B = 256
GATE_RANK = 16
GATE_TAU = 16.0
N_GROUPS = 4
EXPERTS_PER_GROUP = 8
N_EXPERTS = N_GROUPS * EXPERTS_PER_GROUP
TOP_K_FINE = 2
D_FF_E = 256
COL_SIZES = (CONV_CH, H_A * DV_A, H_A, H_A, H_B * DK_B, H_B * DK_B, H_B * DV_B, H_B * DV_B, GATE_RANK, D_MODEL, D_MODEL)
IN_COLS = sum(COL_SIZES)

kernel_name = 'hybrid_gdn_gla_hmoe_step'

F32 = jnp.float32


def rmsnorm(x, w):
    x32 = x.astype(F32)
    y = x32 * lax.rsqrt(jnp.mean(x32 * x32, axis=-1, keepdims=True) + EPS)
    return (y * w.astype(F32)).astype(x.dtype)


def l2norm(x):
    return x * lax.rsqrt(jnp.sum(x * x, axis=-1, keepdims=True) + EPS)


def split_cols(p):
    offs = []
    s = 0
    for c in COL_SIZES[:-1]:
        s += c
        offs.append(s)
    return jnp.split(p, offs, axis=-1)


def to_chunks(a, L):
    B, T = a.shape[:2]
    a = a.reshape(B, T // L, L, *a.shape[2:])
    return jnp.swapaxes(jnp.moveaxis(a, 1, 0), 2, 3)


def from_chunks(o):
    o = jnp.moveaxis(jnp.swapaxes(o, 2, 3), 0, 1)
    return o.reshape(o.shape[0], o.shape[1] * o.shape[2], *o.shape[3:])


def gdn_chunk(S, inp):
    q, k, v, g, beta = inp
    L = q.shape[-2]
    incl = jnp.tril(jnp.ones((L, L), bool))
    strict = jnp.tril(jnp.ones((L, L), bool), -1)
    G = jnp.cumsum(g, axis=-1)
    diff = G[..., :, None] - G[..., None, :]
    dec_incl = jnp.exp(jnp.where(incl, diff, -jnp.inf))
    dec_strict = jnp.where(strict, dec_incl, 0.0)
    A = beta[..., :, None] * jnp.einsum('bhtd,bhjd->bhtj', k, k) * dec_strict
    M = A + jnp.eye(L, dtype=F32)
    rhs = jnp.concatenate([beta[..., None] * v, (beta * jnp.exp(G))[..., None] * k], axis=-1)
    sol = lax.linalg.triangular_solve(M, rhs, left_side=True, lower=True, unit_diagonal=True)
    dv = v.shape[-1]
    value, kcd = sol[..., :dv], sol[..., dv:]
    v_new = value - jnp.einsum('bhld,bhdv->bhlv', kcd, S)
    qk = jnp.einsum('bhtd,bhjd->bhtj', q, k) * dec_incl
    o = jnp.einsum('bhld,bhdv->bhlv', q * jnp.exp(G)[..., None], S) + jnp.einsum('bhtj,bhjv->bhtv', qk, v_new)
    GL = G[..., -1:]
    S_new = S * jnp.exp(GL)[..., None] + jnp.einsum('bhjd,bhjv->bhdv', k * jnp.exp(GL - G)[..., None], v_new)
    return S_new, o


def gla_chunk(S, inp):
    q, k, v, lg = inp
    L = q.shape[-2]
    incl = jnp.tril(jnp.ones((L, L), bool))
    Bc = jnp.cumsum(lg, axis=-2)
    diff = Bc[..., :, None, :] - Bc[..., None, :, :]
    D = jnp.exp(jnp.where(incl[..., None], diff, -jnp.inf))
    A = jnp.einsum('bhtc,bhjc,bhtjc->bhtj', q, k, D)
    o = jnp.einsum('bhtc,bhcv->bhtv', q * jnp.exp(Bc), S) + jnp.einsum('bhtj,bhjv->bhtv', A, v)
    BL = Bc[..., -1:, :]
    S_new = S * jnp.swapaxes(jnp.exp(BL), -1, -2) + jnp.einsum('bhjc,bhjv->bhcv', k * jnp.exp(BL - Bc), v)
    return S_new, o


def run_chunked(body, S0, seqs, segments):
    S = S0
    outs = []
    for start, length, L in segments:
        parts = tuple(to_chunks(a[:, start:start + length], L) for a in seqs)
        S, o = lax.scan(body, S, parts)
        outs.append(from_chunks(o))
    return jnp.concatenate(outs, axis=1), S


def mixer(h, conv_buf, S_a, S_b, segments, w_in, conv_w, a_log, dt_bias, norm_a,
          gla_gate_up, gla_gate_bias, norm_b, w_proj_a, w_proj_b, w_out):
    B, T, _ = h.shape
    p = jnp.einsum('btd,dc->btc', h, w_in).astype(F32)
    qkv, z, b_a, a_a, q_b, k_b, v_b, r_b, lr, g_a, g_b = split_cols(p)
    xc = jnp.concatenate([conv_buf.astype(F32), qkv], axis=1)
    cw = conv_w.astype(F32)
    conv = cw[0] * xc[:, 0:T]
    for i in range(1, CONV_W):
        conv = conv + cw[i] * xc[:, i:i + T]
    new_buf = xc[:, T:]
    qkv = jax.nn.silu(conv)
    q_a, k_a, v_a = jnp.split(qkv, [H_A * DK_A, 2 * H_A * DK_A], axis=-1)
    q_a = l2norm(q_a.reshape(B, T, H_A, DK_A)) * (DK_A ** -0.5)
    k_a = l2norm(k_a.reshape(B, T, H_A, DK_A))
    v_a = v_a.reshape(B, T, H_A, DV_A)
    beta = jax.nn.sigmoid(b_a)
    g = -jnp.exp(a_log.astype(F32)) * jax.nn.softplus(a_a + dt_bias.astype(F32))
    o_a, S_a_new = run_chunked(gdn_chunk, S_a.astype(F32), (q_a, k_a, v_a, g, beta), segments)
    o_a = rmsnorm(o_a, norm_a) * jax.nn.silu(z.reshape(B, T, H_A, DV_A))
    y_a = jnp.einsum('btc,cd->btd', o_a.reshape(B, T, H_A * DV_A), w_proj_a.astype(F32))
    lg = jax.nn.log_sigmoid(jnp.einsum('btr,rc->btc', lr, gla_gate_up.astype(F32)) + gla_gate_bias.astype(F32)) / GATE_TAU
    lg = lg.reshape(B, T, H_B, DK_B)
    q_b = q_b.reshape(B, T, H_B, DK_B) * (DK_B ** -0.5)
    k_b = k_b.reshape(B, T, H_B, DK_B)
    v_b = v_b.reshape(B, T, H_B, DV_B)
    o_b, S_b_new = run_chunked(gla_chunk, S_b.astype(F32), (q_b, k_b, v_b, lg), segments)
    o_b = rmsnorm(o_b, norm_b) * jax.nn.silu(r_b.reshape(B, T, H_B, DV_B))
    y_b = jnp.einsum('btc,cd->btd', o_b.reshape(B, T, H_B * DV_B), w_proj_b.astype(F32))
    merged = jax.nn.sigmoid(g_a) * y_a + jax.nn.sigmoid(g_b) * y_b
    out = jnp.einsum('btd,de->bte', merged, w_out.astype(F32)).astype(h.dtype)
    return out, new_buf, S_a_new, S_b_new


def hier_moe(h, w_rg, b_rg, w_re, b_re, w_gate, w_up, w_down):
    B, T, D = h.shape
    x = h.reshape(B * T, D)
    lg = jnp.einsum('nd,dg->ng', x, w_rg).astype(F32) + b_rg.astype(F32)
    pg = jax.nn.softmax(lg, axis=-1)
    gsel = jnp.argmax(lg, axis=-1)
    pgsel = jnp.max(pg, axis=-1, keepdims=True)
    le = (jnp.einsum('nd,de->ne', x, w_re).astype(F32) + b_re.astype(F32)).reshape(-1, N_GROUPS, EXPERTS_PER_GROUP)
    le = jnp.einsum('ngk,ng->nk', le, jax.nn.one_hot(gsel, N_GROUPS, dtype=F32))
    pe = jax.nn.softmax(le, axis=-1)
    top_v, top_i = lax.top_k(pe, TOP_K_FINE)
    w_sel = pgsel * top_v / jnp.sum(top_v, axis=-1, keepdims=True)
    eid = gsel[:, None] * EXPERTS_PER_GROUP + top_i
    combine = jnp.sum(jax.nn.one_hot(eid, N_EXPERTS, dtype=F32) * w_sel[..., None], axis=1)
    out = jnp.zeros((B * T, D), F32)
    for gi in range(N_GROUPS):
        sl = slice(gi * EXPERTS_PER_GROUP, (gi + 1) * EXPERTS_PER_GROUP)
        a = jax.nn.silu(jnp.einsum('nd,edf->nef', x, w_gate[sl])) * jnp.einsum('nd,edf->nef', x, w_up[sl])
        a = a * combine[:, sl, None].astype(a.dtype)
        out = out + jnp.einsum('nef,efd->nd', a, w_down[sl]).astype(F32)
    return out.astype(h.dtype).reshape(B, T, D)


def trunk(x, segments, conv0, sa0, sb0, norm_mix, w_in, conv_w, a_log, dt_bias, norm_a,
          gla_gate_up, gla_gate_bias, norm_b, w_proj_a, w_proj_b, w_out, norm_ffn,
          router_group_w, router_group_b, router_expert_w, router_expert_b,
          expert_w_gate, expert_w_up, expert_w_down, norm_final):
    convs, sas, sbs = [], [], []
    for l in range(DEPTH):
        mix, c, sa, sb = mixer(rmsnorm(x, norm_mix[l]), conv0[l], sa0[l], sb0[l], segments,
                               w_in[l], conv_w[l], a_log[l], dt_bias[l], norm_a[l],
                               gla_gate_up[l], gla_gate_bias[l], norm_b[l],
                               w_proj_a[l], w_proj_b[l], w_out[l])
        x = x + mix
        x = x + hier_moe(rmsnorm(x, norm_ffn[l]), router_group_w[l], router_group_b[l],
                         router_expert_w[l], router_expert_b[l],
                         expert_w_gate[l], expert_w_up[l], expert_w_down[l])
        convs.append(c)
        sas.append(sa)
        sbs.append(sb)
    return rmsnorm(x, norm_final), jnp.stack(sas), jnp.stack(convs), jnp.stack(sbs)


def setup_inputs(seed: int = 0) -> dict:
    key = jax.random.key(seed)
    ks = jax.random.split(key, 32)

    def nrm(i, shape, scale):
        return jax.random.normal(ks[i], shape, F32) * scale

    dt = jnp.exp(jax.random.uniform(ks[10], (DEPTH, H_A), F32, minval=float(np.log(1e-3)), maxval=float(np.log(1e-1))))
    return {
        'x_prompt': nrm(0, (BATCH, SEQ, D_MODEL), 1.0),
        'x_sample': nrm(1, (DEC_BATCH, DEC_SEQ, D_MODEL), 1.0),
        'state_delta': nrm(2, (DEPTH, DEC_BATCH, H_A, DK_A, DV_A), 0.1),
        'state_conv': nrm(3, (DEPTH, DEC_BATCH, CONV_W - 1, CONV_CH), 1.0),
        'state_gla': nrm(4, (DEPTH, DEC_BATCH, H_B, DK_B, DV_B), 0.5),
        'meta_tokens': nrm(5, (N_META, D_MODEL), 1.0),
        'norm_mix': 1.0 + nrm(6, (DEPTH, D_MODEL), 0.02),
        'w_in': nrm(7, (DEPTH, D_MODEL, IN_COLS), D_MODEL ** -0.5),
        'conv_w': nrm(8, (DEPTH, CONV_W, CONV_CH), CONV_W ** -0.5),
        'a_log': jnp.log(jax.random.uniform(ks[9], (DEPTH, H_A), F32, minval=1.0, maxval=16.0)),
        'dt_bias': dt + jnp.log(-jnp.expm1(-dt)),
        'norm_a': 1.0 + nrm(11, (DEPTH, DV_A), 0.02),
        'gla_gate_up': nrm(12, (DEPTH, GATE_RANK, H_B * DK_B), GATE_RANK ** -0.5),
        'gla_gate_bias': nrm(13, (DEPTH, H_B * DK_B), 0.1),
        'norm_b': 1.0 + nrm(14, (DEPTH, DV_B), 0.02),
        'w_proj_a': nrm(15, (DEPTH, H_A * DV_A, D_MODEL), (H_A * DV_A) ** -0.5),
        'w_proj_b': nrm(16, (DEPTH, H_B * DV_B, D_MODEL), (H_B * DV_B) ** -0.5),
        'w_out': nrm(17, (DEPTH, D_MODEL, D_MODEL), D_MODEL ** -0.5),
        'norm_ffn': 1.0 + nrm(18, (DEPTH, D_MODEL), 0.02),
        'router_group_w': nrm(19, (DEPTH, D_MODEL, N_GROUPS), D_MODEL ** -0.5),
        'router_group_b': nrm(20, (DEPTH, N_GROUPS), 0.01),
        'router_expert_w': nrm(21, (DEPTH, D_MODEL, N_EXPERTS), D_MODEL ** -0.5),
        'router_expert_b': nrm(22, (DEPTH, N_EXPERTS), 0.01),
        'expert_w_gate': nrm(23, (DEPTH, N_EXPERTS, D_MODEL, D_FF_E), D_MODEL ** -0.5),
        'expert_w_up': nrm(24, (DEPTH, N_EXPERTS, D_MODEL, D_FF_E), D_MODEL ** -0.5),
        'expert_w_down': nrm(25, (DEPTH, N_EXPERTS, D_FF_E, D_MODEL), D_FF_E ** -0.5),
        'norm_final': 1.0 + nrm(26, (D_MODEL,), 0.02),
    }


def reference(x_prompt, x_sample, state_delta, state_conv, state_gla, meta_tokens, norm_mix, w_in,
              conv_w, a_log, dt_bias, norm_a, gla_gate_up, gla_gate_bias, norm_b, w_proj_a, w_proj_b,
              w_out, norm_ffn, router_group_w, router_group_b, router_expert_w, router_expert_b,
              expert_w_gate, expert_w_up, expert_w_down, norm_final):
    Bp, Tp, D = x_prompt.shape
    meta = jnp.broadcast_to(meta_tokens.astype(x_prompt.dtype)[None], (Bp, N_META, D))
    xp = jnp.concatenate([meta, x_prompt], axis=1)
    segs_p = [(0, N_META, N_META), (N_META, Tp, CHUNK)]
    conv0 = jnp.zeros((DEPTH, Bp, CONV_W - 1, CONV_CH), F32)
    sa0 = jnp.zeros((DEPTH, Bp, H_A, DK_A, DV_A), F32)
    sb0 = jnp.zeros((DEPTH, Bp, H_B, DK_B, DV_B), F32)
    yp, sa_p, conv_p, sb_p = trunk(xp, segs_p, conv0, sa0, sb0, norm_mix, w_in, conv_w, a_log, dt_bias,
                                   norm_a, gla_gate_up, gla_gate_bias, norm_b, w_proj_a, w_proj_b, w_out,
                                   norm_ffn, router_group_w, router_group_b, router_expert_w,
                                   router_expert_b, expert_w_gate, expert_w_up, expert_w_down, norm_final)
    y_prompt = yp[:, N_META:]
    Ts = x_sample.shape[1]
    segs_s = [(0, Ts, Ts)]
    y_sample, sa_s, conv_s, sb_s = trunk(x_sample, segs_s, state_conv, state_delta, state_gla, norm_mix, w_in,
                                         conv_w, a_log, dt_bias, norm_a, gla_gate_up, gla_gate_bias, norm_b,
                                         w_proj_a, w_proj_b, w_out, norm_ffn, router_group_w, router_group_b,
                                         router_expert_w, router_expert_b, expert_w_gate, expert_w_up,
                                         expert_w_down, norm_final)
    pdt = x_prompt.dtype
    return (y_prompt, y_sample,
            sa_p.astype(pdt), conv_p.astype(pdt), sb_p.astype(pdt),
            sa_s.astype(state_delta.dtype), conv_s.astype(state_conv.dtype), sb_s.astype(state_gla.dtype))
```

```python
import functools

import jax
import jax.numpy as jnp
from jax import lax
from jax.experimental import pallas as pl
from jax.experimental.pallas import tpu as pltpu

F32 = jnp.float32
BF16 = jnp.bfloat16
HI = lax.Precision.HIGHEST

D_MODEL = 1024
N_META = 16
CHUNK = 64
EPS = 1e-6
H_A, DK_A, DV_A = 8, 128, 128
CONV_W = 4
CONV_CH = 2 * H_A * DK_A + H_A * DV_A
H_B, DK_B, DV_B = 4, 128, 256
GATE_RANK = 16
GATE_TAU = 16.0
N_GROUPS = 4
EXPERTS_PER_GROUP = 8
N_EXPERTS = N_GROUPS * EXPERTS_PER_GROUP
D_FF_E = 256
COL_SIZES = (CONV_CH, H_A * DV_A, H_A, H_A, H_B * DK_B, H_B * DK_B, H_B * DV_B, H_B * DV_B,
             GATE_RANK, D_MODEL, D_MODEL)

LANES = 128
SUBLANES = 8
VMEM_LIMIT = 48 * 1024 * 1024

C_QKV = 0
C_Z = C_QKV + CONV_CH
C_QB = C_Z + H_A * DV_A
C_KB = C_QB + H_B * DK_B
C_VB = C_KB + H_B * DK_B
C_RB = C_VB + H_B * DV_B
C_GA = C_RB + H_B * DV_B
C_GB = C_GA + D_MODEL
C_BIG = C_GB + D_MODEL
S_BETA = 0
S_DECAY = H_A
S_LR = 2 * H_A

SAMPLE_L = SUBLANES


def _dot(a, b, prec=None):
    return jnp.dot(a, b, preferred_element_type=F32, precision=prec)


def _dot_nt(a, b, prec=None):
    return lax.dot_general(a, b, (((1,), (1,)), ((), ())), preferred_element_type=F32, precision=prec)


def _dot_tn(a, b, prec=None):
    return lax.dot_general(a, b, (((0,), (0,)), ((), ())), preferred_element_type=F32, precision=prec)


def _bdot(a, b):
    return _dot(a.astype(BF16), b.astype(BF16))


def _bdot_nt(a, b):
    return _dot_nt(a.astype(BF16), b.astype(BF16))


def _bdot_tn(a, b):
    return _dot_tn(a.astype(BF16), b.astype(BF16))


def _sigmoid(x):
    return 1.0 / (1.0 + jnp.exp(-x))


def _silu(x):
    return x * _sigmoid(x)


def _softplus(x):
    return jnp.maximum(x, 0.0) + jnp.log(1.0 + jnp.exp(-jnp.abs(x)))


def _rms(x, w):
    return x * lax.rsqrt(jnp.mean(x * x, axis=-1, keepdims=True) + EPS) * w


def _inproj_kernel(x_ref, nw_ref, wb_ref, ws_ref, big_ref, small_ref, xn_ref):
    @pl.when(pl.program_id(1) == 0)
    def _():
        xn = _rms(x_ref[...], nw_ref[...])
        xn_ref[...] = xn.astype(BF16)
        small_ref[...] = _dot(xn, ws_ref[...], HI)

    big_ref[...] = _dot(xn_ref[...], wb_ref[...])


def _inproj(x, nw, wbig, wsmall, tm, tn):
    n = x.shape[0]
    return pl.pallas_call(
        _inproj_kernel,
        grid=(n // tm, C_BIG // tn),
        in_specs=[
            pl.BlockSpec((tm, D_MODEL), lambda i, j: (i, 0)),
            pl.BlockSpec((1, D_MODEL), lambda i, j: (0, 0)),
            pl.BlockSpec((D_MODEL, tn), lambda i, j: (0, j)),
            pl.BlockSpec((D_MODEL, LANES), lambda i, j: (0, 0)),
        ],
        out_specs=[
            pl.BlockSpec((tm, tn), lambda i, j: (i, j)),
            pl.BlockSpec((tm, LANES), lambda i, j: (i, 0)),
        ],
        out_shape=[
            jax.ShapeDtypeStruct((n, C_BIG), F32),
            jax.ShapeDtypeStruct((n, LANES), F32),
        ],
        scratch_shapes=[pltpu.VMEM((tm, D_MODEL), BF16)],
        compiler_params=pltpu.CompilerParams(
            dimension_semantics=("parallel", "arbitrary"), vmem_limit_bytes=VMEM_LIMIT),
        name="inproj",
    )(x, nw, wbig, wsmall)


def _outproj_kernel(oa_ref, ob_ref, ga_ref, gb_ref, x_ref, wa_ref, wb_ref, wo_ref, out_ref):
    ya = _dot(oa_ref[...].astype(BF16), wa_ref[...])
    yb = _dot(ob_ref[...].astype(BF16), wb_ref[...])
    merged = _sigmoid(ga_ref[...]) * ya + _sigmoid(gb_ref[...]) * yb
    out_ref[...] = x_ref[...] + _dot(merged.astype(BF16), wo_ref[...])


def _outproj(oa, ob, big, x, wa, wb, wo, tm):
    n = x.shape[0]
    row = lambda i: (i, 0)
    full = lambda i: (0, 0)
    return pl.pallas_call(
        _outproj_kernel,
        grid=(n // tm,),
        in_specs=[
            pl.BlockSpec((tm, D_MODEL), row),
            pl.BlockSpec((tm, D_MODEL), row),
            pl.BlockSpec((tm, D_MODEL), lambda i: (i, C_GA // D_MODEL)),
            pl.BlockSpec((tm, D_MODEL), lambda i: (i, C_GB // D_MODEL)),
            pl.BlockSpec((tm, D_MODEL), row),
            pl.BlockSpec((D_MODEL, D_MODEL), full),
            pl.BlockSpec((D_MODEL, D_MODEL), full),
            pl.BlockSpec((D_MODEL, D_MODEL), full),
        ],
        out_specs=pl.BlockSpec((tm, D_MODEL), row),
        out_shape=jax.ShapeDtypeStruct((n, D_MODEL), F32),
        compiler_params=pltpu.CompilerParams(
            dimension_semantics=("parallel",), vmem_limit_bytes=VMEM_LIMIT),
        name="outproj",
    )(oa, ob, big, big, x, wa, wb, wo)


def _route(logits):
    lane = lax.broadcasted_iota(jnp.int32, logits.shape, 1)
    neg = jnp.float32(-jnp.inf)
    big = jnp.int32(1 << 20)
    gmask = (lane >= N_EXPERTS) & (lane < N_EXPERTS + N_GROUPS)
    gl = jnp.where(gmask, logits, neg)
    gmax = jnp.max(gl, axis=-1, keepdims=True)
    gsel = jnp.min(jnp.where(gl == gmax, lane - N_EXPERTS, big), axis=-1, keepdims=True)
    pgsel = 1.0 / jnp.sum(jnp.where(gmask, jnp.exp(gl - gmax), 0.0), axis=-1, keepdims=True)
    lo = gsel * EXPERTS_PER_GROUP
    emask = (lane >= lo) & (lane < lo + EXPERTS_PER_GROUP)
    el = jnp.where(emask, logits, neg)
    m1 = jnp.max(el, axis=-1, keepdims=True)
    i1 = jnp.min(jnp.where(el == m1, lane, big), axis=-1, keepdims=True)
    el2 = jnp.where(lane == i1, neg, el)
    m2 = jnp.max(el2, axis=-1, keepdims=True)
    i2 = jnp.min(jnp.where(el2 == m2, lane, big), axis=-1, keepdims=True)
    r = jnp.exp(m2 - m1)
    w1 = pgsel / (1.0 + r)
    w2 = pgsel * r / (1.0 + r)
    return jnp.where(lane == i1, w1, 0.0) + jnp.where(lane == i2, w2, 0.0)


def _moe_kernel(x_ref, nw_ref, wr_ref, br_ref, wgu_ref, wd_ref, out_ref, xn_ref, comb_ref, acc_ref):
    e = pl.program_id(1)

    @pl.when(e == 0)
    def _():
        xn = _rms(x_ref[...], nw_ref[...])
        xn_ref[...] = xn.astype(BF16)
        comb_ref[...] = _route(_dot(xn, wr_ref[...], HI) + br_ref[...])
        acc_ref[...] = jnp.zeros_like(acc_ref)

    comb = comb_ref[...]
    lane = lax.broadcasted_iota(jnp.int32, comb.shape, 1)
    c = jnp.sum(jnp.where(lane == e, comb, 0.0), axis=-1, keepdims=True)
    h = _dot(xn_ref[...], wgu_ref[...])
    a = _silu(h[:, :D_FF_E]) * h[:, D_FF_E:] * c
    acc_ref[...] += _dot(a.astype(BF16), wd_ref[...])

    @pl.when(e == N_EXPERTS - 1)
    def _():
        out_ref[...] = x_ref[...] + acc_ref[...]


def _moe(x, nw, wr, br, wgu, wd, tm):
    n = x.shape[0]
    row = lambda i, e: (i, 0)
    full = lambda i, e: (0, 0)
    return pl.pallas_call(
        _moe_kernel,
        grid=(n // tm, N_EXPERTS),
        in_specs=[
            pl.BlockSpec((tm, D_MODEL), row),
            pl.BlockSpec((1, D_MODEL), full),
            pl.BlockSpec((D_MODEL, LANES), full),
            pl.BlockSpec((1, LANES), full),
            pl.BlockSpec((None, D_MODEL, 2 * D_FF_E), lambda i, e: (e, 0, 0)),
            pl.BlockSpec((None, D_FF_E, D_MODEL), lambda i, e: (e, 0, 0)),
        ],
        out_specs=pl.BlockSpec((tm, D_MODEL), row),
        out_shape=jax.ShapeDtypeStruct((n, D_MODEL), F32),
        scratch_shapes=[
            pltpu.VMEM((tm, D_MODEL), BF16),
            pltpu.VMEM((tm, LANES), F32),
            pltpu.VMEM((tm, D_MODEL), F32),
        ],
        compiler_params=pltpu.CompilerParams(
            dimension_semantics=("parallel", "arbitrary"), vmem_limit_bytes=VMEM_LIMIT),
        name="moe",
    )(x, nw, wr, br, wgu, wd)


def _final_norm_kernel(x_ref, nw_ref, out_ref):
    out_ref[...] = _rms(x_ref[...], nw_ref[...])


def _final_norm(x, nw, tm):
    n = x.shape[0]
    return pl.pallas_call(
        _final_norm_kernel,
        grid=(n // tm,),
        in_specs=[pl.BlockSpec((tm, D_MODEL), lambda i: (i, 0)),
                  pl.BlockSpec((1, D_MODEL), lambda i: (0, 0))],
        out_specs=pl.BlockSpec((tm, D_MODEL), lambda i: (i, 0)),
        out_shape=jax.ShapeDtypeStruct((n, D_MODEL), F32),
        compiler_params=pltpu.CompilerParams(dimension_semantics=("parallel",)),
        name="final_norm",
    )(x, nw)


def _chunk_masks(L):
    ri = lax.broadcasted_iota(jnp.int32, (L, L), 0)
    ci = lax.broadcasted_iota(jnp.int32, (L, L), 1)
    return ri >= ci, ri > ci, ri == ci


def _valid_rows(L, t_lo, t_hi):
    row = lax.broadcasted_iota(jnp.int32, (L, 1), 0) + pl.program_id(1) * L
    return (row >= t_lo) & (row < t_hi)


def _gdn_kernel(L, t_lo, t_hi, conv_row0,
                qkv_ref, z_ref, sm_ref, cin_ref, sin_ref, cw_ref, hp_ref, na_ref,
                o_ref, cout_ref, so_ref, ext_ref):
    @pl.when(pl.program_id(1) == 0)
    def _():
        ext_ref[0:SUBLANES, :] = cin_ref[...]
        so_ref[...] = sin_ref[...]

    valid = _valid_rows(L, t_lo, t_hi)
    ext_ref[SUBLANES:SUBLANES + L, :] = jnp.where(valid, qkv_ref[...], 0.0)

    def conv_act(col0):
        acc = None
        for i in range(CONV_W):
            r0 = SUBLANES - (CONV_W - 1) + i
            term = cw_ref[i:i + 1, col0:col0 + LANES] * ext_ref[r0:r0 + L, col0:col0 + LANES]
            acc = term if acc is None else acc + term
        return _silu(acc)

    incl, strict, diag = _chunk_masks(L)
    tri = incl.astype(F32)
    eye = diag.astype(F32)
    lane = lax.broadcasted_iota(jnp.int32, (L, LANES), 1)

    sm = sm_ref[...]
    beta_all = jnp.where(valid, _sigmoid(sm), 0.0)
    g_all = jnp.where(valid, -jnp.exp(hp_ref[0:1, :]) * _softplus(sm + hp_ref[1:2, :]), 0.0)
    G_all = _dot(tri, g_all, HI)

    for h in range(H_A):
        q = conv_act(h * DK_A)
        k = conv_act(H_A * DK_A + h * DK_A)
        v = conv_act(2 * H_A * DK_A + h * DV_A)
        q = q * lax.rsqrt(jnp.sum(q * q, axis=-1, keepdims=True) + EPS) * (DK_A ** -0.5)
        k = k * lax.rsqrt(jnp.sum(k * k, axis=-1, keepdims=True) + EPS)
        k = jnp.where(valid, k, 0.0)
        beta = beta_all[:, S_BETA + h:S_BETA + h + 1]
        Gc = G_all[:, S_DECAY + h:S_DECAY + h + 1]
        sel = (lane == S_DECAY + h).astype(F32)
        Gr = _dot_nt(sel, G_all, HI)
        dec = jnp.where(incl, jnp.exp(jnp.where(incl, Gc - Gr, 0.0)), 0.0)
        A = jnp.where(strict, beta * _bdot_nt(k, k) * dec, 0.0)
        Bp = -A
        P = eye + Bp
        p2 = 2
        while p2 < L:
            Bp = _dot(Bp, Bp, HI)
            P = P + _dot(P, Bp, HI)
            p2 *= 2
        eG = jnp.exp(Gc)
        rhs = jnp.concatenate([beta * v, (beta * eG) * k], axis=1)
        sol = _dot(P, rhs, HI)
        value, kcd = sol[:, :DV_A], sol[:, DV_A:]
        S = so_ref[h]
        v_new = value - _bdot(kcd, S)
        qk = _bdot_nt(q, k) * dec
        o = _bdot(q * eG, S) + _bdot(qk, v_new)
        GL = G_all[L - 1:L, S_DECAY + h:S_DECAY + h + 1]
        kd = k * jnp.exp(GL - Gc)
        so_ref[h] = S * jnp.exp(GL) + _bdot_tn(kd, v_new)
        zh = z_ref[:, h * DV_A:(h + 1) * DV_A]
        o_ref[:, h * DV_A:(h + 1) * DV_A] = _rms(o, na_ref[...]) * _silu(zh)

    cout_ref[...] = ext_ref[conv_row0:conv_row0 + SUBLANES, :]
    ext_ref[0:SUBLANES, :] = ext_ref[L:L + SUBLANES, :]


def _gdn(big, small, cin, sin, cw, hp, na, *, nseq, L, nchunks, row_block0, t_lo, t_hi):
    n = big.shape[0]
    conv_row0 = SUBLANES + (t_hi - (nchunks - 1) * L) - SUBLANES
    rb = lambda b, c: (row_block0 + b * nchunks + c, 0)
    kern = functools.partial(_gdn_kernel, L, t_lo, t_hi, conv_row0)
    return pl.pallas_call(
        kern,
        grid=(nseq, nchunks),
        in_specs=[
            pl.BlockSpec((L, CONV_CH), rb),
            pl.BlockSpec((L, H_A * DV_A), lambda b, c: (row_block0 + b * nchunks + c, C_Z // (H_A * DV_A))),
            pl.BlockSpec((L, LANES), rb),
            pl.BlockSpec((None, SUBLANES, CONV_CH), lambda b, c: (b, 0, 0)),
            pl.BlockSpec((None, H_A, DK_A, DV_A), lambda b, c: (b, 0, 0, 0)),
            pl.BlockSpec((SUBLANES, CONV_CH), lambda b, c: (0, 0)),
            pl.BlockSpec((SUBLANES, LANES), lambda b, c: (0, 0)),
            pl.BlockSpec((1, DV_A), lambda b, c: (0, 0)),
        ],
        out_specs=[
            pl.BlockSpec((L, H_A * DV_A), lambda b, c: (b * nchunks + c, 0)),
            pl.BlockSpec((None, SUBLANES, CONV_CH), lambda b, c: (b, 0, 0)),
            pl.BlockSpec((None, H_A, DK_A, DV_A), lambda b, c: (b, 0, 0, 0)),
        ],
        out_shape=[
            jax.ShapeDtypeStruct((nseq * nchunks * L, H_A * DV_A), F32),
            jax.ShapeDtypeStruct((nseq, SUBLANES, CONV_CH), F32),
            jax.ShapeDtypeStruct((nseq, H_A, DK_A, DV_A), F32),
        ],
        scratch_shapes=[pltpu.VMEM((SUBLANES + L, CONV_CH), F32)],
        compiler_params=pltpu.CompilerParams(
            dimension_semantics=("parallel", "arbitrary"), vmem_limit_bytes=VMEM_LIMIT),
        name="gdn_L%d" % L,
    )(big, big, small, cin, sin, cw, hp, na)


def _gla_kernel(L, SB, t_lo, t_hi,
                qb_ref, kb_ref, vb_ref, rb_ref, sm_ref, sin_ref, wup_ref, bup_ref, nb_ref,
                o_ref, so_ref):
    @pl.when(pl.program_id(1) == 0)
    def _():
        so_ref[...] = sin_ref[...]

    valid = _valid_rows(L, t_lo, t_hi)
    incl, _, _ = _chunk_masks(L)
    tri = incl.astype(F32)
    ri = lax.broadcasted_iota(jnp.int32, (SB, L), 0)
    ci = lax.broadcasted_iota(jnp.int32, (SB, L), 1)
    krow = lax.broadcasted_iota(jnp.int32, (L, 1), 0)

    pre = _dot(sm_ref[...], wup_ref[...], HI) + bup_ref[...]
    lg = jnp.where(valid, -_softplus(-pre) / GATE_TAU, 0.0)
    Bc_all = _dot(tri, lg, HI)

    for h in range(H_B):
        q = qb_ref[:, h * DK_B:(h + 1) * DK_B] * (DK_B ** -0.5)
        k = jnp.where(valid, kb_ref[:, h * DK_B:(h + 1) * DK_B], 0.0)
        v = vb_ref[:, h * DV_B:(h + 1) * DV_B]
        Bc = Bc_all[:, h * DK_B:(h + 1) * DK_B]
        S = so_ref[h]
        o = _bdot(q * jnp.exp(Bc), S)
        parts = []
        for blk in range(L // SB):
            r0 = blk * SB
            nk = r0 + SB
            b0 = Bc[r0:r0 + 1, :]
            qs = q[r0:nk] * jnp.exp(Bc[r0:nk] - b0)
            ks = k * jnp.exp(jnp.where(krow < nk, b0 - Bc, 0.0))
            a = jnp.where(ci <= ri + r0, _dot_nt(qs, ks, HI), 0.0)
            parts.append(_bdot(a, v))
        o = o + (parts[0] if len(parts) == 1 else jnp.concatenate(parts, axis=0))
        BL = Bc[L - 1:L, :]
        kd = k * jnp.exp(BL - Bc)
        e_col = jnp.transpose(jnp.broadcast_to(jnp.exp(BL), (DK_B, DK_B)))[:, 0:1]
        so_ref[h] = S * e_col + _bdot_tn(kd, v)
        rh = rb_ref[:, h * DV_B:(h + 1) * DV_B]
        o_ref[:, h * DV_B:(h + 1) * DV_B] = _rms(o, nb_ref[...]) * _silu(rh)


def _gla(big, small, sin, wup, bup, nb, *, nseq, L, SB, nchunks, row_block0, t_lo, t_hi):
    kern = functools.partial(_gla_kernel, L, SB, t_lo, t_hi)
    r = lambda b, c: row_block0 + b * nchunks + c
    return pl.pallas_call(
        kern,
        grid=(nseq, nchunks),
        in_specs=[
            pl.BlockSpec((L, H_B * DK_B), lambda b, c: (r(b, c), C_QB // (H_B * DK_B))),
            pl.BlockSpec((L, H_B * DK_B), lambda b, c: (r(b, c), C_KB // (H_B * DK_B))),
            pl.BlockSpec((L, H_B * DV_B), lambda b, c: (r(b, c), C_VB // (H_B * DV_B))),
            pl.BlockSpec((L, H_B * DV_B), lambda b, c: (r(b, c), C_RB // (H_B * DV_B))),
            pl.BlockSpec((L, LANES), lambda b, c: (r(b, c), 0)),
            pl.BlockSpec((None, H_B, DK_B, DV_B), lambda b, c: (b, 0, 0, 0)),
            pl.BlockSpec((LANES, H_B * DK_B), lambda b, c: (0, 0)),
            pl.BlockSpec((1, H_B * DK_B), lambda b, c: (0, 0)),
            pl.BlockSpec((1, DV_B), lambda b, c: (0, 0)),
        ],
        out_specs=[
            pl.BlockSpec((L, H_B * DV_B), lambda b, c: (b * nchunks + c, 0)),
            pl.BlockSpec((None, H_B, DK_B, DV_B), lambda b, c: (b, 0, 0, 0)),
        ],
        out_shape=[
            jax.ShapeDtypeStruct((nseq * nchunks * L, H_B * DV_B), F32),
            jax.ShapeDtypeStruct((nseq, H_B, DK_B, DV_B), F32),
        ],
        compiler_params=pltpu.CompilerParams(
            dimension_semantics=("parallel", "arbitrary"), vmem_limit_bytes=VMEM_LIMIT),
        name="gla_L%d" % L,
    )(big, big, big, big, small, sin, wup, bup, nb)


def _split_cols(w):
    offs, s = [], 0
    for c in COL_SIZES[:-1]:
        s += c
        offs.append(s)
    return jnp.split(w, offs, axis=-1)


def kernel(x_prompt, x_sample, state_delta, state_conv, state_gla, meta_tokens, norm_mix, w_in, conv_w, a_log, dt_bias, norm_a, gla_gate_up, gla_gate_bias, norm_b, w_proj_a, w_proj_b, w_out, norm_ffn, router_group_w, router_group_b, router_expert_w, router_expert_b, expert_w_gate, expert_w_up, expert_w_down, norm_final):
    Bp, Tp, D = x_prompt.shape
    Bs, Ts, _ = x_sample.shape
    depth = w_in.shape[0]
    pad_p = (-N_META) % CHUNK
    Tpp = pad_p + N_META + Tp
    ncp = Tpp // CHUNK
    np_rows = Bp * Tpp
    ns_rows = Bs * SAMPLE_L
    n = np_rows + ns_rows
    dt = x_prompt.dtype
    xp = jnp.concatenate([
        jnp.zeros((Bp, pad_p, D), dt),
        jnp.broadcast_to(meta_tokens.astype(dt)[None], (Bp, N_META, D)),
        x_prompt], axis=1).reshape(np_rows, D)
    xs = jnp.pad(x_sample, ((0, 0), (0, SAMPLE_L - Ts), (0, 0))).reshape(ns_rows, D)
    x = jnp.concatenate([xp, xs], axis=0).astype(F32)

    tm_big = 1280 if n % 1280 == 0 else 512
    tm = 512

    zeros_conv = jnp.zeros((Bp, SUBLANES, CONV_CH), F32)
    zeros_sa = jnp.zeros((Bp, H_A, DK_A, DV_A), F32)
    zeros_sb = jnp.zeros((Bp, H_B, DK_B, DV_B), F32)

    sa_p, conv_p, sb_p, sa_s, conv_s, sb_s = [], [], [], [], [], []
    for l in range(depth):
        qkv, z, b_a, a_a, q_b, k_b, v_b, r_b, lr, g_a, g_b = _split_cols(w_in[l])
        wbig = jnp.concatenate([qkv, z, q_b, k_b, v_b, r_b, g_a, g_b], axis=1).astype(BF16)
        wsmall = jnp.concatenate(
            [b_a, a_a, lr, jnp.zeros((D, LANES - 2 * H_A - GATE_RANK), F32)], axis=1).astype(F32)
        big, small = _inproj(x, norm_mix[l][None].astype(F32), wbig, wsmall, tm_big, 512)

        cw = jnp.pad(conv_w[l].astype(F32), ((0, SUBLANES - CONV_W), (0, 0)))
        hp = jnp.zeros((SUBLANES, LANES), F32)
        hp = hp.at[0, S_DECAY:S_DECAY + H_A].set(a_log[l].astype(F32))
        hp = hp.at[1, S_DECAY:S_DECAY + H_A].set(dt_bias[l].astype(F32))
        na = norm_a[l][None].astype(F32)
        wup = jnp.zeros((LANES, H_B * DK_B), F32).at[S_LR:S_LR + GATE_RANK].set(gla_gate_up[l].astype(F32))
        bup = gla_gate_bias[l][None].astype(F32)
        nb = norm_b[l][None].astype(F32)

        cin_s = jnp.pad(state_conv[l].astype(F32), ((0, 0), (SUBLANES - (CONV_W - 1), 0), (0, 0)))
        oa_p, c_p, s_ap = _gdn(big, small, zeros_conv, zeros_sa, cw, hp, na,
                               nseq=Bp, L=CHUNK, nchunks=ncp, row_block0=0, t_lo=pad_p, t_hi=Tpp)
        oa_s, c_s, s_as = _gdn(big, small, cin_s, state_delta[l].astype(F32), cw, hp, na,
                               nseq=Bs, L=SAMPLE_L, nchunks=1, row_block0=np_rows // SAMPLE_L,
                               t_lo=0, t_hi=Ts)
        ob_p, s_bp = _gla(big, small, zeros_sb, wup, bup, nb,
                          nseq=Bp, L=CHUNK, SB=16, nchunks=ncp, row_block0=0, t_lo=pad_p, t_hi=Tpp)
        ob_s, s_bs = _gla(big, small, state_gla[l].astype(F32), wup, bup, nb,
                          nseq=Bs, L=SAMPLE_L, SB=SAMPLE_L, nchunks=1,
                          row_block0=np_rows // SAMPLE_L, t_lo=0, t_hi=Ts)
        oa = jnp.concatenate([oa_p, oa_s], axis=0)
        ob = jnp.concatenate([ob_p, ob_s], axis=0)
        x = _outproj(oa, ob, big, x, w_proj_a[l].astype(BF16), w_proj_b[l].astype(BF16),
                     w_out[l].astype(BF16), tm)

        wr = jnp.concatenate([router_expert_w[l], router_group_w[l],
                              jnp.zeros((D, LANES - N_EXPERTS - N_GROUPS), F32)], axis=1).astype(F32)
        br = jnp.concatenate([router_expert_b[l], router_group_b[l],
                              jnp.zeros((LANES - N_EXPERTS - N_GROUPS,), F32)])[None].astype(F32)
        wgu = jnp.concatenate([expert_w_gate[l], expert_w_up[l]], axis=-1).astype(BF16)
        wd = expert_w_down[l].astype(BF16)
        x = _moe(x, norm_ffn[l][None].astype(F32), wr, br, wgu, wd, tm_big)

        sa_p.append(s_ap)
        conv_p.append(c_p[:, SUBLANES - (CONV_W - 1):])
        sb_p.append(s_bp)
        sa_s.append(s_as)
        conv_s.append(c_s[:, SUBLANES - (CONV_W - 1):])
        sb_s.append(s_bs)

    y = _final_norm(x, norm_final[None].astype(F32), tm)
    y_prompt = y[:np_rows].reshape(Bp, Tpp, D)[:, pad_p + N_META:].astype(dt)
    y_sample = y[np_rows:].reshape(Bs, SAMPLE_L, D)[:, :Ts].astype(x_sample.dtype)
    return (y_prompt, y_sample,
            jnp.stack(sa_p).astype(dt), jnp.stack(conv_p).astype(dt), jnp.stack(sb_p).astype(dt),
            jnp.stack(sa_s).astype(state_delta.dtype), jnp.stack(conv_s).astype(state_conv.dtype),
            jnp.stack(sb_s).astype(state_gla.dtype))
```

```python
import functools

import jax
import jax.numpy as jnp
from jax import lax
from jax.experimental import pallas as pl
from jax.experimental.pallas import tpu as pltpu

F32 = jnp.float32
BF16 = jnp.bfloat16
HI = lax.Precision.HIGHEST

D_MODEL = 1024
N_META = 16
CHUNK = 64
EPS = 1e-6
H_A, DK_A, DV_A = 8, 128, 128
CONV_W = 4
CONV_CH = 2 * H_A * DK_A + H_A * DV_A
H_B, DK_B, DV_B = 4, 128, 256
GATE_RANK = 16
GATE_TAU = 16.0
N_GROUPS = 4
EXPERTS_PER_GROUP = 8
N_EXPERTS = N_GROUPS * EXPERTS_PER_GROUP
D_FF_E = 256
COL_SIZES = (CONV_CH, H_A * DV_A, H_A, H_A, H_B * DK_B, H_B * DK_B, H_B * DV_B, H_B * DV_B,
             GATE_RANK, D_MODEL, D_MODEL)

LANES = 128
SUBLANES = 8
VMEM_LIMIT = 48 * 1024 * 1024

C_QKV = 0
C_Z = C_QKV + CONV_CH
C_QB = C_Z + H_A * DV_A
C_KB = C_QB + H_B * DK_B
C_VB = C_KB + H_B * DK_B
C_RB = C_VB + H_B * DV_B
C_GA = C_RB + H_B * DV_B
C_GB = C_GA + D_MODEL
C_BIG = C_GB + D_MODEL
S_BETA = 0
S_DECAY = H_A
S_LR = 2 * H_A

SAMPLE_L = SUBLANES
ROWS = CHUNK
GLA_SUB = 16


def _dot(a, b, prec=None):
    return jnp.dot(a, b, preferred_element_type=F32, precision=prec)


def _dot_nt(a, b, prec=None):
    return lax.dot_general(a, b, (((1,), (1,)), ((), ())), preferred_element_type=F32, precision=prec)


def _dot_tn(a, b, prec=None):
    return lax.dot_general(a, b, (((0,), (0,)), ((), ())), preferred_element_type=F32, precision=prec)


def _bdot(a, b):
    return _dot(a.astype(BF16), b.astype(BF16))


def _bdot_nt(a, b):
    return _dot_nt(a.astype(BF16), b.astype(BF16))


def _bdot_tn(a, b):
    return _dot_tn(a.astype(BF16), b.astype(BF16))


def _split2(a):
    hi = a.astype(BF16)
    return hi, (a - hi.astype(F32)).astype(BF16)


def _dot3(a, b, dot=_dot):
    ah, al = _split2(a)
    bh, bl = _split2(b)
    return dot(ah, bh) + dot(al, bh) + dot(ah, bl)


def _mask_dot(m, b):
    hi = b.astype(BF16)
    r = b - hi.astype(F32)
    mid = r.astype(BF16)
    lo = (r - mid.astype(F32)).astype(BF16)
    return _dot(m, hi) + _dot(m, mid) + _dot(m, lo)


def _sigmoid(x):
    return 1.0 / (1.0 + jnp.exp(-x))


def _silu(x):
    return x * _sigmoid(x)


def _softplus(x):
    return jnp.maximum(x, 0.0) + jnp.log(1.0 + jnp.exp(-jnp.abs(x)))


def _rms(x, w):
    return x * lax.rsqrt(jnp.mean(x * x, axis=-1, keepdims=True) + EPS) * w


def _inproj_kernel(x_ref, nw_ref, wb_ref, ws_ref, big_ref, small_ref, xn_ref):
    @pl.when(pl.program_id(1) == 0)
    def _():
        xn = _rms(x_ref[...], nw_ref[...])
        xn_ref[...] = xn.astype(BF16)
        small_ref[...] = _dot(xn, ws_ref[...], HI)

    big_ref[...] = _dot(xn_ref[...], wb_ref[...])


def _inproj(x, nw, wbig, wsmall, tm, tn):
    n = x.shape[0]
    return pl.pallas_call(
        _inproj_kernel,
        grid=(n // tm, C_BIG // tn),
        in_specs=[
            pl.BlockSpec((tm, D_MODEL), lambda i, j: (i, 0)),
            pl.BlockSpec((1, D_MODEL), lambda i, j: (0, 0)),
            pl.BlockSpec((D_MODEL, tn), lambda i, j: (0, j)),
            pl.BlockSpec((D_MODEL, LANES), lambda i, j: (0, 0)),
        ],
        out_specs=[
            pl.BlockSpec((tm, tn), lambda i, j: (i, j)),
            pl.BlockSpec((tm, LANES), lambda i, j: (i, 0)),
        ],
        out_shape=[
            jax.ShapeDtypeStruct((n, C_BIG), F32),
            jax.ShapeDtypeStruct((n, LANES), F32),
        ],
        scratch_shapes=[pltpu.VMEM((tm, D_MODEL), BF16)],
        compiler_params=pltpu.CompilerParams(
            dimension_semantics=("parallel", "arbitrary"), vmem_limit_bytes=VMEM_LIMIT),
        name="inproj",
    )(x, nw, wbig, wsmall)


def _outproj_kernel(oa_ref, ob_ref, ga_ref, gb_ref, x_ref, wa_ref, wb_ref, wo_ref, out_ref):
    ya = _dot(oa_ref[...].astype(BF16), wa_ref[...])
    yb = _dot(ob_ref[...].astype(BF16), wb_ref[...])
    merged = _sigmoid(ga_ref[...]) * ya + _sigmoid(gb_ref[...]) * yb
    out_ref[...] = x_ref[...] + _dot(merged.astype(BF16), wo_ref[...])


def _outproj(oa, ob, big, x, wa, wb, wo, tm):
    n = x.shape[0]
    row = lambda i: (i, 0)
    full = lambda i: (0, 0)
    return pl.pallas_call(
        _outproj_kernel,
        grid=(n // tm,),
        in_specs=[
            pl.BlockSpec((tm, D_MODEL), row),
            pl.BlockSpec((tm, D_MODEL), row),
            pl.BlockSpec((tm, D_MODEL), lambda i: (i, C_GA // D_MODEL)),
            pl.BlockSpec((tm, D_MODEL), lambda i: (i, C_GB // D_MODEL)),
            pl.BlockSpec((tm, D_MODEL), row),
            pl.BlockSpec((D_MODEL, D_MODEL), full),
            pl.BlockSpec((D_MODEL, D_MODEL), full),
            pl.BlockSpec((D_MODEL, D_MODEL), full),
        ],
        out_specs=pl.BlockSpec((tm, D_MODEL), row),
        out_shape=jax.ShapeDtypeStruct((n, D_MODEL), F32),
        compiler_params=pltpu.CompilerParams(
            dimension_semantics=("parallel",), vmem_limit_bytes=VMEM_LIMIT),
        name="outproj",
    )(oa, ob, big, big, x, wa, wb, wo)


def _route(logits):
    lane = lax.broadcasted_iota(jnp.int32, logits.shape, 1)
    neg = jnp.float32(-jnp.inf)
    big = jnp.int32(1 << 20)
    gmask = (lane >= N_EXPERTS) & (lane < N_EXPERTS + N_GROUPS)
    gl = jnp.where(gmask, logits, neg)
    gmax = jnp.max(gl, axis=-1, keepdims=True)
    gsel = jnp.min(jnp.where(gl == gmax, lane - N_EXPERTS, big), axis=-1, keepdims=True)
    pgsel = 1.0 / jnp.sum(jnp.where(gmask, jnp.exp(gl - gmax), 0.0), axis=-1, keepdims=True)
    lo = gsel * EXPERTS_PER_GROUP
    emask = (lane >= lo) & (lane < lo + EXPERTS_PER_GROUP)
    el = jnp.where(emask, logits, neg)
    m1 = jnp.max(el, axis=-1, keepdims=True)
    i1 = jnp.min(jnp.where(el == m1, lane, big), axis=-1, keepdims=True)
    el2 = jnp.where(lane == i1, neg, el)
    m2 = jnp.max(el2, axis=-1, keepdims=True)
    i2 = jnp.min(jnp.where(el2 == m2, lane, big), axis=-1, keepdims=True)
    r = jnp.exp(m2 - m1)
    w1 = pgsel / (1.0 + r)
    w2 = pgsel * r / (1.0 + r)
    return jnp.where(lane == i1, w1, 0.0) + jnp.where(lane == i2, w2, 0.0)


def _moe_kernel(x_ref, nw_ref, wr_ref, br_ref, wgu_ref, wd_ref, out_ref, xn_ref, comb_ref, acc_ref):
    e = pl.program_id(1)

    @pl.when(e == 0)
    def _():
        xn = _rms(x_ref[...], nw_ref[...])
        xn_ref[...] = xn.astype(BF16)
        comb_ref[...] = _route(_dot(xn, wr_ref[...], HI) + br_ref[...])
        acc_ref[...] = jnp.zeros_like(acc_ref)

    comb = comb_ref[...]
    lane = lax.broadcasted_iota(jnp.int32, comb.shape, 1)
    c = jnp.sum(jnp.where(lane == e, comb, 0.0), axis=-1, keepdims=True)
    h = _dot(xn_ref[...], wgu_ref[...])
    a = _silu(h[:, :D_FF_E]) * h[:, D_FF_E:] * c
    acc_ref[...] += _dot(a.astype(BF16), wd_ref[...])

    @pl.when(e == N_EXPERTS - 1)
    def _():
        out_ref[...] = x_ref[...] + acc_ref[...]


def _moe(x, nw, wr, br, wgu, wd, tm):
    n = x.shape[0]
    row = lambda i, e: (i, 0)
    full = lambda i, e: (0, 0)
    return pl.pallas_call(
        _moe_kernel,
        grid=(n // tm, N_EXPERTS),
        in_specs=[
            pl.BlockSpec((tm, D_MODEL), row),
            pl.BlockSpec((1, D_MODEL), full),
            pl.BlockSpec((D_MODEL, LANES), full),
            pl.BlockSpec((1, LANES), full),
            pl.BlockSpec((None, D_MODEL, 2 * D_FF_E), lambda i, e: (e, 0, 0)),
            pl.BlockSpec((None, D_FF_E, D_MODEL), lambda i, e: (e, 0, 0)),
        ],
        out_specs=pl.BlockSpec((tm, D_MODEL), row),
        out_shape=jax.ShapeDtypeStruct((n, D_MODEL), F32),
        scratch_shapes=[
            pltpu.VMEM((tm, D_MODEL), BF16),
            pltpu.VMEM((tm, LANES), F32),
            pltpu.VMEM((tm, D_MODEL), F32),
        ],
        compiler_params=pltpu.CompilerParams(
            dimension_semantics=("parallel", "arbitrary"), vmem_limit_bytes=VMEM_LIMIT),
        name="moe",
    )(x, nw, wr, br, wgu, wd)


def _final_norm_kernel(x_ref, nw_ref, out_ref):
    out_ref[...] = _rms(x_ref[...], nw_ref[...])


def _final_norm(x, nw, tm):
    n = x.shape[0]
    return pl.pallas_call(
        _final_norm_kernel,
        grid=(n // tm,),
        in_specs=[pl.BlockSpec((tm, D_MODEL), lambda i: (i, 0)),
                  pl.BlockSpec((1, D_MODEL), lambda i: (0, 0))],
        out_specs=pl.BlockSpec((tm, D_MODEL), lambda i: (i, 0)),
        out_shape=jax.ShapeDtypeStruct((n, D_MODEL), F32),
        compiler_params=pltpu.CompilerParams(dimension_semantics=("parallel",)),
        name="final_norm",
    )(x, nw)


def _seq_masks(n, lseq):
    r = lax.broadcasted_iota(jnp.int32, (n, n), 0)
    c = lax.broadcasted_iota(jnp.int32, (n, n), 1)
    sh = lseq.bit_length() - 1
    same = (r >> sh) == (c >> sh)
    return same & (r >= c), same & (r > c), r == c


def _valid_rows(n, lseq, t_lo, t_hi):
    row = lax.broadcasted_iota(jnp.int32, (n, 1), 0)
    t = (row & (lseq - 1)) + pl.program_id(1) * lseq
    return (t >= t_lo) & (t < t_hi)


def _pad_rows(a, n):
    return jnp.concatenate([a, jnp.zeros((n - a.shape[0], a.shape[1]), a.dtype)], axis=0)


def _gdn_kernel(Lseq, NS, t_lo, t_hi, conv_row0,
                qkv_ref, z_ref, sm_ref, cin_ref, sin_ref, cw_ref, hp_ref, na_ref,
                o_ref, cout_ref, so_ref, ext_ref):
    R = NS * Lseq
    R2 = 2 * R
    npairs = H_A // 2

    @pl.when(pl.program_id(1) == 0)
    def _():
        ext_ref[:, 0:SUBLANES, :] = cin_ref[...]
        so_ref[...] = sin_ref[...]

    valid = _valid_rows(R, Lseq, t_lo, t_hi)
    valid_l = _valid_rows(Lseq, Lseq, t_lo, t_hi)
    for s in range(NS):
        ext_ref[s, SUBLANES:SUBLANES + Lseq, :] = jnp.where(
            valid_l, qkv_ref[s * Lseq:(s + 1) * Lseq, :], 0.0)

    def conv_act(col0):
        pieces = []
        for s in range(NS):
            acc = None
            for i in range(CONV_W):
                r0 = SUBLANES - (CONV_W - 1) + i
                term = cw_ref[i:i + 1, col0:col0 + LANES] * ext_ref[s, r0:r0 + Lseq, col0:col0 + LANES]
                acc = term if acc is None else acc + term
            pieces.append(acc)
        return _silu(pieces[0] if NS == 1 else jnp.concatenate(pieces, axis=0))

    incl, _, _ = _seq_masks(R, Lseq)
    incl2, strict2, diag2 = _seq_masks(R2, Lseq)
    tri = jnp.where(incl, 1.0, 0.0).astype(BF16)
    eye2 = jnp.where(diag2, 1.0, 0.0)

    sm = sm_ref[...]
    beta_all = jnp.where(valid, _sigmoid(sm), 0.0)
    g_all = jnp.where(valid, -jnp.exp(hp_ref[0:1, :]) * _softplus(sm + hp_ref[1:2, :]), 0.0)
    G_all = _mask_dot(tri, g_all)
    GT = jnp.transpose(_pad_rows(G_all, LANES))

    q2, k2, v2, beta2, Gc2, Gr2 = [], [], [], [], [], []
    for p in range(npairs):
        qs, ks, vs = [], [], []
        for h in (2 * p, 2 * p + 1):
            q = conv_act(h * DK_A)
            k = conv_act(H_A * DK_A + h * DK_A)
            qs.append(q * lax.rsqrt(jnp.sum(q * q, axis=-1, keepdims=True) + EPS) * (DK_A ** -0.5))
            ks.append(jnp.where(valid, k * lax.rsqrt(jnp.sum(k * k, axis=-1, keepdims=True) + EPS), 0.0))
            vs.append(conv_act(2 * H_A * DK_A + h * DV_A))
        q2.append(jnp.concatenate(qs, axis=0))
        k2.append(jnp.concatenate(ks, axis=0))
        v2.append(jnp.concatenate(vs, axis=0))
        h0 = 2 * p
        beta2.append(jnp.concatenate([beta_all[:, S_BETA + h0:S_BETA + h0 + 1],
                                      beta_all[:, S_BETA + h0 + 1:S_BETA + h0 + 2]], axis=0))
        Gc2.append(jnp.concatenate([G_all[:, S_DECAY + h0:S_DECAY + h0 + 1],
                                    G_all[:, S_DECAY + h0 + 1:S_DECAY + h0 + 2]], axis=0))
        Gr2.append(jnp.concatenate([GT[S_DECAY + h0:S_DECAY + h0 + 1, :R],
                                    GT[S_DECAY + h0 + 1:S_DECAY + h0 + 2, :R]], axis=1))

    rng = range(npairs)
    kk = [_bdot_nt(k2[p], k2[p]) for p in rng]
    qk = [_bdot_nt(q2[p], k2[p]) for p in rng]
    dec = [jnp.where(incl2, jnp.exp(jnp.where(incl2, Gc2[p] - Gr2[p], 0.0)), 0.0) for p in rng]
    Bp = [-jnp.where(strict2, beta2[p] * kk[p] * dec[p], 0.0) for p in rng]
    P = [eye2 + Bp[p] for p in rng]
    n = 2
    while n < Lseq:
        Bp = [_dot3(b, b) for b in Bp]
        P = [pp + _dot3(pp, b) for pp, b in zip(P, Bp)]
        n *= 2
    eG = [jnp.exp(Gc2[p]) for p in rng]
    rhs = [jnp.concatenate([beta2[p] * v2[p], (beta2[p] * eG[p]) * k2[p]], axis=1) for p in rng]
    sol = [_dot3(P[p], rhs[p]) for p in rng]
    qk = [qk[p] * dec[p] for p in rng]
    qe = [q2[p] * eG[p] for p in rng]

    items = [(p, j, s) for p in rng for j in range(2) for s in range(NS)]

    def rows(j, s):
        return slice(j * R + s * Lseq, j * R + (s + 1) * Lseq)

    S = {it: so_ref[it[2], 2 * it[0] + it[1]] for it in items}
    both = {}
    for p, j, s in items:
        r = rows(j, s)
        both[(p, j, s)] = _bdot(jnp.concatenate([sol[p][r, DV_A:], qe[p][r]], axis=0), S[(p, j, s)])
    vnew = {}
    for p, j, s in items:
        vnew[(p, j, s)] = sol[p][rows(j, s), :DV_A] - both[(p, j, s)][:Lseq]
    for p, j, s in items:
        r = rows(j, s)
        last = j * R + (s + 1) * Lseq - 1
        GL = Gc2[p][last:last + 1]
        kd = k2[p][r] * jnp.exp(GL - Gc2[p][r])
        so_ref[s, 2 * p + j] = S[(p, j, s)] * jnp.exp(GL) + _bdot_tn(kd, vnew[(p, j, s)])
    for p in rng:
        order = [(p, j, s) for j in range(2) for s in range(NS)]
        vnew2 = jnp.concatenate([vnew[it] for it in order], axis=0)
        o2 = jnp.concatenate([both[it][Lseq:] for it in order], axis=0) + _bdot(qk[p], vnew2)
        for j in range(2):
            h = 2 * p + j
            zh = z_ref[:, h * DV_A:(h + 1) * DV_A]
            o_ref[:, h * DV_A:(h + 1) * DV_A] = _rms(o2[j * R:(j + 1) * R], na_ref[...]) * _silu(zh)

    for s in range(NS):
        cout_ref[s] = ext_ref[s, conv_row0:conv_row0 + SUBLANES, :]
        ext_ref[s, 0:SUBLANES, :] = ext_ref[s, Lseq:Lseq + SUBLANES, :]


def _gdn(big, small, cin, sin, cw, hp, na, *, nsteps, Lseq, NS, nchunks, row_block0, t_lo, t_hi):
    assert NS * Lseq == ROWS
    conv_row0 = t_hi - (nchunks - 1) * Lseq
    kern = functools.partial(_gdn_kernel, Lseq, NS, t_lo, t_hi, conv_row0)
    rb = lambda b, c: row_block0 + b * nchunks + c
    return pl.pallas_call(
        kern,
        grid=(nsteps, nchunks),
        in_specs=[
            pl.BlockSpec((ROWS, CONV_CH), lambda b, c: (rb(b, c), 0)),
            pl.BlockSpec((ROWS, H_A * DV_A), lambda b, c: (rb(b, c), C_Z // (H_A * DV_A))),
            pl.BlockSpec((ROWS, LANES), lambda b, c: (rb(b, c), 0)),
            pl.BlockSpec((NS, SUBLANES, CONV_CH), lambda b, c: (b, 0, 0)),
            pl.BlockSpec((NS, H_A, DK_A, DV_A), lambda b, c: (b, 0, 0, 0)),
            pl.BlockSpec((SUBLANES, CONV_CH), lambda b, c: (0, 0)),
            pl.BlockSpec((SUBLANES, LANES), lambda b, c: (0, 0)),
            pl.BlockSpec((1, DV_A), lambda b, c: (0, 0)),
        ],
        out_specs=[
            pl.BlockSpec((ROWS, H_A * DV_A), lambda b, c: (b * nchunks + c, 0)),
            pl.BlockSpec((NS, SUBLANES, CONV_CH), lambda b, c: (b, 0, 0)),
            pl.BlockSpec((NS, H_A, DK_A, DV_A), lambda b, c: (b, 0, 0, 0)),
        ],
        out_shape=[
            jax.ShapeDtypeStruct((nsteps * nchunks * ROWS, H_A * DV_A), F32),
            jax.ShapeDtypeStruct((nsteps * NS, SUBLANES, CONV_CH), F32),
            jax.ShapeDtypeStruct((nsteps * NS, H_A, DK_A, DV_A), F32),
        ],
        scratch_shapes=[pltpu.VMEM((NS, SUBLANES + Lseq, CONV_CH), F32)],
        compiler_params=pltpu.CompilerParams(
            dimension_semantics=("parallel", "arbitrary"), vmem_limit_bytes=VMEM_LIMIT),
        name="gdn_L%d" % Lseq,
    )(big, big, small, cin, sin, cw, hp, na)


def _gla_kernel(Lseq, NS, SB, t_lo, t_hi,
                qb_ref, kb_ref, vb_ref, rb_ref, sm_ref, sin_ref, wup_ref, bup_ref, nb_ref,
                o_ref, so_ref):
    R = NS * Lseq
    heads = range(H_B)

    @pl.when(pl.program_id(1) == 0)
    def _():
        so_ref[...] = sin_ref[...]

    nb = Lseq // SB
    assert nb == 1 or NS == 1
    n_off = SB * nb * (nb - 1) // 2
    sb_shift = SB.bit_length() - 1

    valid = _valid_rows(R, Lseq, t_lo, t_hi)
    r = lax.broadcasted_iota(jnp.int32, (R, R), 0)
    c = lax.broadcasted_iota(jnp.int32, (R, R), 1)
    same = (r | (Lseq - 1)) == (c | (Lseq - 1))
    m_row = jnp.where(same & (c <= r), 1.0, 0.0)
    m_first = jnp.where(same & (c <= (r & ~(SB - 1))), 1.0, 0.0)
    m_last = jnp.where(same, 1.0, 0.0)
    masks = jnp.concatenate([m_row, m_first, m_last], axis=0).astype(BF16)

    pre = _dot3(sm_ref[...], wup_ref[...]) + bup_ref[...]
    lg = jnp.where(valid, -_softplus(-pre) / GATE_TAU, 0.0)
    cums = _mask_dot(masks, lg)
    Bc_all, b0_all, BL_all = cums[:R], cums[R:2 * R], cums[2 * R:]

    re = lax.broadcasted_iota(jnp.int32, (R, n_off + R), 0)
    ce = lax.broadcasted_iota(jnp.int32, (R, n_off + R), 1)
    rblk = (re & (Lseq - 1)) >> sb_shift
    cd = ce - n_off
    see = ((ce >= (SB // 2) * rblk * (rblk - 1)) & (ce < (SB // 2) * rblk * (rblk + 1))) | (
        (cd >= 0) & ((cd >> sb_shift) == (re >> sb_shift)) & (cd <= re))

    def hs(a, h, w):
        return a[:, h * w:(h + 1) * w]

    q = [hs(qb_ref, h, DK_B) * (DK_B ** -0.5) for h in heads]
    k = [jnp.where(valid, hs(kb_ref, h, DK_B), 0.0) for h in heads]
    v = [hs(vb_ref, h, DV_B) for h in heads]
    Bc = [hs(Bc_all, h, DK_B) for h in heads]
    b0 = [hs(b0_all, h, DK_B) for h in heads]
    BL = [hs(BL_all, h, DK_B) for h in heads]
    qs = [q[h] * jnp.exp(Bc[h] - b0[h]) for h in heads]
    kx, vx = [], []
    for h in heads:
        kparts = [k[h][:i * SB] * jnp.exp(Bc[h][i * SB:i * SB + 1] - Bc[h][:i * SB]) for i in range(1, nb)]
        kparts.append(k[h] * jnp.exp(b0[h] - Bc[h]))
        vparts = [v[h][:i * SB] for i in range(1, nb)] + [v[h]]
        kx.append(kparts[0] if nb == 1 else jnp.concatenate(kparts, axis=0))
        vx.append(vparts[0] if nb == 1 else jnp.concatenate(vparts, axis=0))
    Ax = [jnp.where(see, _dot3(qs[h], kx[h], _dot_nt), 0.0) for h in heads]
    o_intra = [_bdot(Ax[h], vx[h]) for h in heads]
    qe = [q[h] * jnp.exp(Bc[h]) for h in heads]
    kd = [k[h] * jnp.exp(BL[h] - Bc[h]) for h in heads]
    ET = [jnp.transpose(_pad_rows(jnp.exp(BL[h]), LANES)) for h in heads]

    items = [(h, s) for h in heads for s in range(NS)]
    S = {it: so_ref[it[1], it[0]] for it in items}
    o_inter = {}
    for h, s in items:
        o_inter[(h, s)] = _bdot(qe[h][s * Lseq:(s + 1) * Lseq], S[(h, s)])
    for h, s in items:
        rows = slice(s * Lseq, (s + 1) * Lseq)
        so_ref[s, h] = S[(h, s)] * ET[h][:, s * Lseq:s * Lseq + 1] + _bdot_tn(kd[h][rows], v[h][rows])
    for h in heads:
        parts = [o_inter[(h, s)] for s in range(NS)]
        o = (parts[0] if NS == 1 else jnp.concatenate(parts, axis=0)) + o_intra[h]
        rh = hs(rb_ref, h, DV_B)
        o_ref[:, h * DV_B:(h + 1) * DV_B] = _rms(o, nb_ref[...]) * _silu(rh)


def _gla(big, small, sin, wup, bup, nb, *, nsteps, Lseq, NS, SB, nchunks, row_block0, t_lo, t_hi):
    assert NS * Lseq == ROWS
    kern = functools.partial(_gla_kernel, Lseq, NS, SB, t_lo, t_hi)
    r = lambda b, c: row_block0 + b * nchunks + c
    return pl.pallas_call(
        kern,
        grid=(nsteps, nchunks),
        in_specs=[
            pl.BlockSpec((ROWS, H_B * DK_B), lambda b, c: (r(b, c), C_QB // (H_B * DK_B))),
            pl.BlockSpec((ROWS, H_B * DK_B), lambda b, c: (r(b, c), C_KB // (H_B * DK_B))),
            pl.BlockSpec((ROWS, H_B * DV_B), lambda b, c: (r(b, c), C_VB // (H_B * DV_B))),
            pl.BlockSpec((ROWS, H_B * DV_B), lambda b, c: (r(b, c), C_RB // (H_B * DV_B))),
            pl.BlockSpec((ROWS, LANES), lambda b, c: (r(b, c), 0)),
            pl.BlockSpec((NS, H_B, DK_B, DV_B), lambda b, c: (b, 0, 0, 0)),
            pl.BlockSpec((LANES, H_B * DK_B), lambda b, c: (0, 0)),
            pl.BlockSpec((1, H_B * DK_B), lambda b, c: (0, 0)),
            pl.BlockSpec((1, DV_B), lambda b, c: (0, 0)),
        ],
        out_specs=[
            pl.BlockSpec((ROWS, H_B * DV_B), lambda b, c: (b * nchunks + c, 0)),
            pl.BlockSpec((NS, H_B, DK_B, DV_B), lambda b, c: (b, 0, 0, 0)),
        ],
        out_shape=[
            jax.ShapeDtypeStruct((nsteps * nchunks * ROWS, H_B * DV_B), F32),
            jax.ShapeDtypeStruct((nsteps * NS, H_B, DK_B, DV_B), F32),
        ],
        compiler_params=pltpu.CompilerParams(
            dimension_semantics=("parallel", "arbitrary"), vmem_limit_bytes=VMEM_LIMIT),
        name="gla_L%d" % Lseq,
    )(big, big, big, big, small, sin, wup, bup, nb)


def _split_cols(w):
    offs, s = [], 0
    for c in COL_SIZES[:-1]:
        s += c
        offs.append(s)
    return jnp.split(w, offs, axis=-1)


def kernel(x_prompt, x_sample, state_delta, state_conv, state_gla, meta_tokens, norm_mix, w_in, conv_w, a_log, dt_bias, norm_a, gla_gate_up, gla_gate_bias, norm_b, w_proj_a, w_proj_b, w_out, norm_ffn, router_group_w, router_group_b, router_expert_w, router_expert_b, expert_w_gate, expert_w_up, expert_w_down, norm_final):
    Bp, Tp, D = x_prompt.shape
    Bs, Ts, _ = x_sample.shape
    depth = w_in.shape[0]
    pad_p = (-N_META) % CHUNK
    Tpp = pad_p + N_META + Tp
    ncp = Tpp // CHUNK
    np_rows = Bp * Tpp
    ns_rows = Bs * SAMPLE_L
    n = np_rows + ns_rows
    seq_per_step = ROWS // SAMPLE_L
    dt = x_prompt.dtype
    xp = jnp.concatenate([
        jnp.zeros((Bp, pad_p, D), dt),
        jnp.broadcast_to(meta_tokens.astype(dt)[None], (Bp, N_META, D)),
        x_prompt], axis=1).reshape(np_rows, D)
    xs = jnp.pad(x_sample, ((0, 0), (0, SAMPLE_L - Ts), (0, 0))).reshape(ns_rows, D)
    x = jnp.concatenate([xp, xs], axis=0).astype(F32)

    tm_big = 1280 if n % 1280 == 0 else 512
    tm = 512

    zeros_conv = jnp.zeros((Bp, SUBLANES, CONV_CH), F32)
    zeros_sa = jnp.zeros((Bp, H_A, DK_A, DV_A), F32)
    zeros_sb = jnp.zeros((Bp, H_B, DK_B, DV_B), F32)
    prompt_args = dict(nsteps=Bp, Lseq=CHUNK, NS=1, nchunks=ncp, row_block0=0, t_lo=pad_p, t_hi=Tpp)
    sample_args = dict(nsteps=Bs // seq_per_step, Lseq=SAMPLE_L, NS=seq_per_step, nchunks=1,
                       row_block0=np_rows // ROWS, t_lo=0, t_hi=Ts)

    sa_p, conv_p, sb_p, sa_s, conv_s, sb_s = [], [], [], [], [], []
    for l in range(depth):
        qkv, z, b_a, a_a, q_b, k_b, v_b, r_b, lr, g_a, g_b = _split_cols(w_in[l])
        wbig = jnp.concatenate([qkv, z, q_b, k_b, v_b, r_b, g_a, g_b], axis=1).astype(BF16)
        wsmall = jnp.concatenate(
            [b_a, a_a, lr, jnp.zeros((D, LANES - 2 * H_A - GATE_RANK), F32)], axis=1).astype(F32)
        big, small = _inproj(x, norm_mix[l][None].astype(F32), wbig, wsmall, tm_big, 512)

        cw = jnp.pad(conv_w[l].astype(F32), ((0, SUBLANES - CONV_W), (0, 0)))
        hp = jnp.zeros((SUBLANES, LANES), F32)
        hp = hp.at[0, S_DECAY:S_DECAY + H_A].set(a_log[l].astype(F32))
        hp = hp.at[1, S_DECAY:S_DECAY + H_A].set(dt_bias[l].astype(F32))
        na = norm_a[l][None].astype(F32)
        wup = jnp.zeros((LANES, H_B * DK_B), F32).at[S_LR:S_LR + GATE_RANK].set(gla_gate_up[l].astype(F32))
        bup = gla_gate_bias[l][None].astype(F32)
        nb = norm_b[l][None].astype(F32)

        cin_s = jnp.pad(state_conv[l].astype(F32), ((0, 0), (SUBLANES - (CONV_W - 1), 0), (0, 0)))
        oa_p, c_p, s_ap = _gdn(big, small, zeros_conv, zeros_sa, cw, hp, na, **prompt_args)
        oa_s, c_s, s_as = _gdn(big, small, cin_s, state_delta[l].astype(F32), cw, hp, na, **sample_args)
        ob_p, s_bp = _gla(big, small, zeros_sb, wup, bup, nb, SB=GLA_SUB, **prompt_args)
        ob_s, s_bs = _gla(big, small, state_gla[l].astype(F32), wup, bup, nb, SB=SAMPLE_L, **sample_args)
        oa = jnp.concatenate([oa_p, oa_s], axis=0)
        ob = jnp.concatenate([ob_p, ob_s], axis=0)
        x = _outproj(oa, ob, big, x, w_proj_a[l].astype(BF16), w_proj_b[l].astype(BF16),
                     w_out[l].astype(BF16), tm)

        wr = jnp.concatenate([router_expert_w[l], router_group_w[l],
                              jnp.zeros((D, LANES - N_EXPERTS - N_GROUPS), F32)], axis=1).astype(F32)
        br = jnp.concatenate([router_expert_b[l], router_group_b[l],
                              jnp.zeros((LANES - N_EXPERTS - N_GROUPS,), F32)])[None].astype(F32)
        wgu = jnp.concatenate([expert_w_gate[l], expert_w_up[l]], axis=-1).astype(BF16)
        wd = expert_w_down[l].astype(BF16)
        x = _moe(x, norm_ffn[l][None].astype(F32), wr, br, wgu, wd, tm_big)

        sa_p.append(s_ap)
        conv_p.append(c_p[:, SUBLANES - (CONV_W - 1):])
        sb_p.append(s_bp)
        sa_s.append(s_as)
        conv_s.append(c_s[:, SUBLANES - (CONV_W - 1):])
        sb_s.append(s_bs)

    y = _final_norm(x, norm_final[None].astype(F32), tm)
    y_prompt = y[:np_rows].reshape(Bp, Tpp, D)[:, pad_p + N_META:].astype(dt)
    y_sample = y[np_rows:].reshape(Bs, SAMPLE_L, D)[:, :Ts].astype(x_sample.dtype)
    return (y_prompt, y_sample,
            jnp.stack(sa_p).astype(dt), jnp.stack(conv_p).astype(dt), jnp.stack(sb_p).astype(dt),
            jnp.stack(sa_s).astype(state_delta.dtype), jnp.stack(conv_s).astype(state_conv.dtype),
            jnp.stack(sb_s).astype(state_gla.dtype))
```

```python
import functools

import jax
import jax.numpy as jnp
from jax import lax
from jax.experimental import pallas as pl
from jax.experimental.pallas import tpu as pltpu

F32 = jnp.float32
BF16 = jnp.bfloat16
HI = lax.Precision.HIGHEST

D_MODEL = 1024
N_META = 16
CHUNK = 64
EPS = 1e-6
H_A, DK_A, DV_A = 8, 128, 128
CONV_W = 4
CONV_CH = 2 * H_A * DK_A + H_A * DV_A
H_B, DK_B, DV_B = 4, 128, 256
GATE_RANK = 16
GATE_TAU = 16.0
N_GROUPS = 4
EXPERTS_PER_GROUP = 8
N_EXPERTS = N_GROUPS * EXPERTS_PER_GROUP
D_FF_E = 256
COL_SIZES = (CONV_CH, H_A * DV_A, H_A, H_A, H_B * DK_B, H_B * DK_B, H_B * DV_B, H_B * DV_B,
             GATE_RANK, D_MODEL, D_MODEL)

LANES = 128
SUBLANES = 8
VMEM_LIMIT = 48 * 1024 * 1024

C_QKV = 0
C_Z = C_QKV + CONV_CH
C_QB = C_Z + H_A * DV_A
C_KB = C_QB + H_B * DK_B
C_VB = C_KB + H_B * DK_B
C_RB = C_VB + H_B * DV_B
C_GA = C_RB + H_B * DV_B
C_GB = C_GA + D_MODEL
C_BIG = C_GB + D_MODEL
S_BETA = 0
S_DECAY = H_A
S_LR = 2 * H_A

SAMPLE_L = SUBLANES
ROWS = CHUNK
GLA_SUB = 16
MOE_TILE = 256


def _dot(a, b, prec=None):
    return jnp.dot(a, b, preferred_element_type=F32, precision=prec)


def _dot_nt(a, b, prec=None):
    return lax.dot_general(a, b, (((1,), (1,)), ((), ())), preferred_element_type=F32, precision=prec)


def _dot_tn(a, b, prec=None):
    return lax.dot_general(a, b, (((0,), (0,)), ((), ())), preferred_element_type=F32, precision=prec)


def _bdot(a, b):
    return _dot(a.astype(BF16), b.astype(BF16))


def _bdot_nt(a, b):
    return _dot_nt(a.astype(BF16), b.astype(BF16))


def _bdot_tn(a, b):
    return _dot_tn(a.astype(BF16), b.astype(BF16))


def _split2(a):
    hi = a.astype(BF16)
    return hi, (a - hi.astype(F32)).astype(BF16)


def _dot3(a, b, dot=_dot):
    ah, al = _split2(a)
    bh, bl = _split2(b)
    return dot(ah, bh) + dot(al, bh) + dot(ah, bl)


def _mask_dot(m, b):
    hi = b.astype(BF16)
    r = b - hi.astype(F32)
    mid = r.astype(BF16)
    lo = (r - mid.astype(F32)).astype(BF16)
    return _dot(m, hi) + _dot(m, mid) + _dot(m, lo)


def _sigmoid(x):
    return 1.0 / (1.0 + jnp.exp(-x))


def _silu(x):
    return x * _sigmoid(x)


def _softplus(x):
    return jnp.maximum(x, 0.0) + jnp.log(1.0 + jnp.exp(-jnp.abs(x)))


def _rms(x, w):
    return x * lax.rsqrt(jnp.mean(x * x, axis=-1, keepdims=True) + EPS) * w


def _inproj_kernel(x_ref, nw_ref, wb_ref, ws_ref, big_ref, small_ref, xn_ref):
    @pl.when(pl.program_id(1) == 0)
    def _():
        xn = _rms(x_ref[...], nw_ref[...])
        xn_ref[...] = xn.astype(BF16)
        small_ref[...] = _dot(xn, ws_ref[...], HI)

    big_ref[...] = _dot(xn_ref[...], wb_ref[...])


def _inproj(x, nw, wbig, wsmall, tm, tn):
    n = x.shape[0]
    return pl.pallas_call(
        _inproj_kernel,
        grid=(n // tm, C_BIG // tn),
        in_specs=[
            pl.BlockSpec((tm, D_MODEL), lambda i, j: (i, 0)),
            pl.BlockSpec((1, D_MODEL), lambda i, j: (0, 0)),
            pl.BlockSpec((D_MODEL, tn), lambda i, j: (0, j)),
            pl.BlockSpec((D_MODEL, LANES), lambda i, j: (0, 0)),
        ],
        out_specs=[
            pl.BlockSpec((tm, tn), lambda i, j: (i, j)),
            pl.BlockSpec((tm, LANES), lambda i, j: (i, 0)),
        ],
        out_shape=[
            jax.ShapeDtypeStruct((n, C_BIG), F32),
            jax.ShapeDtypeStruct((n, LANES), F32),
        ],
        scratch_shapes=[pltpu.VMEM((tm, D_MODEL), BF16)],
        compiler_params=pltpu.CompilerParams(
            dimension_semantics=("parallel", "arbitrary"), vmem_limit_bytes=VMEM_LIMIT),
        name="inproj",
    )(x, nw, wbig, wsmall)


def _outproj_kernel(oa_ref, ob_ref, ga_ref, gb_ref, x_ref, wa_ref, wb_ref, wo_ref, out_ref):
    ya = _dot(oa_ref[...].astype(BF16), wa_ref[...])
    yb = _dot(ob_ref[...].astype(BF16), wb_ref[...])
    merged = _sigmoid(ga_ref[...]) * ya + _sigmoid(gb_ref[...]) * yb
    out_ref[...] = x_ref[...] + _dot(merged.astype(BF16), wo_ref[...])


def _outproj(oa, ob, big, x, wa, wb, wo, tm):
    n = x.shape[0]
    row = lambda i: (i, 0)
    full = lambda i: (0, 0)
    return pl.pallas_call(
        _outproj_kernel,
        grid=(n // tm,),
        in_specs=[
            pl.BlockSpec((tm, D_MODEL), row),
            pl.BlockSpec((tm, D_MODEL), row),
            pl.BlockSpec((tm, D_MODEL), lambda i: (i, C_GA // D_MODEL)),
            pl.BlockSpec((tm, D_MODEL), lambda i: (i, C_GB // D_MODEL)),
            pl.BlockSpec((tm, D_MODEL), row),
            pl.BlockSpec((D_MODEL, D_MODEL), full),
            pl.BlockSpec((D_MODEL, D_MODEL), full),
            pl.BlockSpec((D_MODEL, D_MODEL), full),
        ],
        out_specs=pl.BlockSpec((tm, D_MODEL), row),
        out_shape=jax.ShapeDtypeStruct((n, D_MODEL), F32),
        compiler_params=pltpu.CompilerParams(
            dimension_semantics=("parallel",), vmem_limit_bytes=VMEM_LIMIT),
        name="outproj",
    )(oa, ob, big, big, x, wa, wb, wo)


def _route(logits):
    lane = lax.broadcasted_iota(jnp.int32, logits.shape, 1)
    neg = jnp.float32(-jnp.inf)
    big = jnp.int32(1 << 20)
    gmask = (lane >= N_EXPERTS) & (lane < N_EXPERTS + N_GROUPS)
    gl = jnp.where(gmask, logits, neg)
    gmax = jnp.max(gl, axis=-1, keepdims=True)
    gsel = jnp.min(jnp.where(gl == gmax, lane - N_EXPERTS, big), axis=-1, keepdims=True)
    pgsel = 1.0 / jnp.sum(jnp.where(gmask, jnp.exp(gl - gmax), 0.0), axis=-1, keepdims=True)
    lo = gsel * EXPERTS_PER_GROUP
    emask = (lane >= lo) & (lane < lo + EXPERTS_PER_GROUP)
    el = jnp.where(emask, logits, neg)
    m1 = jnp.max(el, axis=-1, keepdims=True)
    i1 = jnp.min(jnp.where(el == m1, lane, big), axis=-1, keepdims=True)
    el2 = jnp.where(lane == i1, neg, el)
    m2 = jnp.max(el2, axis=-1, keepdims=True)
    i2 = jnp.min(jnp.where(el2 == m2, lane, big), axis=-1, keepdims=True)
    r = jnp.exp(m2 - m1)
    w1 = pgsel / (1.0 + r)
    w2 = pgsel * r / (1.0 + r)
    return i1, i2, w1, w2


def _router_kernel(x_ref, nw_ref, wr_ref, br_ref, xn_ref, idx_ref, wt_ref):
    xn = _rms(x_ref[...], nw_ref[...])
    xn_ref[...] = xn
    i1, i2, w1, w2 = _route(_dot(xn, wr_ref[...], HI) + br_ref[...])
    lane = lax.broadcasted_iota(jnp.int32, idx_ref.shape, 1)
    idx_ref[...] = jnp.where(lane == 0, i1, jnp.where(lane == 1, i2, 0))
    wt_ref[...] = jnp.where(lane == 0, w1, jnp.where(lane == 1, w2, 0.0))


def _router(x, nw, wr, br, tm):
    n = x.shape[0]
    row = lambda i: (i, 0)
    full = lambda i: (0, 0)
    return pl.pallas_call(
        _router_kernel,
        grid=(n // tm,),
        in_specs=[
            pl.BlockSpec((tm, D_MODEL), row),
            pl.BlockSpec((1, D_MODEL), full),
            pl.BlockSpec((D_MODEL, LANES), full),
            pl.BlockSpec((1, LANES), full),
        ],
        out_specs=[
            pl.BlockSpec((tm, D_MODEL), row),
            pl.BlockSpec((tm, LANES), row),
            pl.BlockSpec((tm, LANES), row),
        ],
        out_shape=[
            jax.ShapeDtypeStruct((n, D_MODEL), F32),
            jax.ShapeDtypeStruct((n, LANES), jnp.int32),
            jax.ShapeDtypeStruct((n, LANES), F32),
        ],
        compiler_params=pltpu.CompilerParams(
            dimension_semantics=("parallel",), vmem_limit_bytes=VMEM_LIMIT),
        name="router",
    )(x, nw, wr, br)


def _moe_plan(idx, tm):
    n = idx.shape[0]
    npairs = 2 * n
    nt = npairs // tm + N_EXPERTS
    e = idx[:, :2].reshape(-1)
    onehot = (e[:, None] == jnp.arange(N_EXPERTS, dtype=jnp.int32)[None]).astype(jnp.int32)
    csum = jnp.cumsum(onehot, axis=0)
    rank = jnp.take_along_axis(csum, e[:, None], axis=1)[:, 0] - 1
    cpad = (csum[-1] + tm - 1) // tm * tm
    ends = jnp.cumsum(cpad)
    dest = ((ends - cpad)[e] + rank).astype(jnp.int32)
    src = jnp.zeros((nt * tm,), jnp.int32).at[dest].set(jnp.arange(npairs, dtype=jnp.int32) // 2)
    tile_start = jnp.arange(nt, dtype=jnp.int32) * tm
    tile_e = jnp.minimum(jnp.searchsorted(ends, tile_start, side="right"), N_EXPERTS - 1).astype(jnp.int32)
    tile_valid = (tile_start < ends[-1]).astype(jnp.int32)
    return src.reshape(nt, 1, tm), tile_e, tile_valid, dest.reshape(n, 2)


def _row_gather(idx_ref, src_hbm, buf, sem, slot, nrows):
    def body(r, carry):
        pltpu.make_async_copy(src_hbm.at[pl.ds(idx_ref[0, r], 1), :],
                              buf.at[slot, pl.ds(r, 1), :], sem.at[slot]).start()
        return carry
    lax.fori_loop(0, nrows, body, 0, unroll=8)


def _row_gather_wait(src_hbm, buf, sem, slot, nrows):
    def body(r, carry):
        pltpu.make_async_copy(src_hbm.at[pl.ds(0, 1), :],
                              buf.at[slot, pl.ds(r, 1), :], sem.at[slot]).wait()
        return carry
    lax.fori_loop(0, nrows, body, 0, unroll=8)


def _expert_kernel(te_ref, tv_ref, src_ref, src_next_ref, xn_hbm, wgu_ref, wd_ref, y_ref, xbuf, sem):
    t = pl.program_id(0)
    nt = pl.num_programs(0)
    tm = xbuf.shape[1]
    slot = t & 1

    @pl.when((t == 0) & (tv_ref[0] == 1))
    def _():
        _row_gather(src_ref, xn_hbm, xbuf, sem, 0, tm)

    @pl.when((t + 1 < nt) & (tv_ref[jnp.minimum(t + 1, nt - 1)] == 1))
    def _():
        _row_gather(src_next_ref, xn_hbm, xbuf, sem, 1 - slot, tm)

    @pl.when(tv_ref[t] == 1)
    def _():
        _row_gather_wait(xn_hbm, xbuf, sem, slot, tm)
        h = _dot(xbuf[slot].astype(BF16), wgu_ref[...])
        a = _silu(h[:, :D_FF_E]) * h[:, D_FF_E:]
        y_ref[...] = _dot(a.astype(BF16), wd_ref[...])

    @pl.when(tv_ref[t] == 0)
    def _():
        y_ref[...] = jnp.zeros_like(y_ref)


def _experts(xn, src, tile_e, tile_valid, wgu, wd):
    nt, _, tm = src.shape
    return pl.pallas_call(
        _expert_kernel,
        grid_spec=pltpu.PrefetchScalarGridSpec(
            num_scalar_prefetch=2,
            grid=(nt,),
            in_specs=[
                pl.BlockSpec((None, 1, tm), lambda t, te, tv: (t, 0, 0), memory_space=pltpu.SMEM),
                pl.BlockSpec((None, 1, tm), lambda t, te, tv: (jnp.minimum(t + 1, nt - 1), 0, 0),
                             memory_space=pltpu.SMEM),
                pl.BlockSpec(memory_space=pl.ANY),
                pl.BlockSpec((None, D_MODEL, 2 * D_FF_E), lambda t, te, tv: (te[t], 0, 0)),
                pl.BlockSpec((None, D_FF_E, D_MODEL), lambda t, te, tv: (te[t], 0, 0)),
            ],
            out_specs=pl.BlockSpec((tm, D_MODEL), lambda t, te, tv: (t, 0)),
            scratch_shapes=[pltpu.VMEM((2, tm, D_MODEL), F32), pltpu.SemaphoreType.DMA((2,))],
        ),
        out_shape=jax.ShapeDtypeStruct((nt * tm, D_MODEL), F32),
        compiler_params=pltpu.CompilerParams(
            dimension_semantics=("arbitrary",), vmem_limit_bytes=VMEM_LIMIT),
        name="experts",
    )(tile_e, tile_valid, src, src, xn, wgu, wd)


def _combine_kernel(final, d_ref, d_next_ref, y_hbm, x_ref, wt_ref, fnw_ref, out_ref, ybuf, sem):
    t = pl.program_id(0)
    nt = pl.num_programs(0)
    tm = x_ref.shape[0]
    slot = t & 1

    @pl.when(t == 0)
    def _():
        _row_gather(d_ref, y_hbm, ybuf, sem, 0, 2 * tm)

    @pl.when(t + 1 < nt)
    def _():
        _row_gather(d_next_ref, y_hbm, ybuf, sem, 1 - slot, 2 * tm)

    _row_gather_wait(y_hbm, ybuf, sem, slot, 2 * tm)
    wt = wt_ref[...]
    out = x_ref[...] + wt[:, 0:1] * ybuf[slot, 0:tm, :] + wt[:, 1:2] * ybuf[slot, tm:2 * tm, :]
    out_ref[...] = _rms(out, fnw_ref[...]) if final else out


def _combine(x, y, dest, wt, fnw, final, tm):
    n = x.shape[0]
    nt = n // tm
    d = jnp.swapaxes(dest.reshape(nt, tm, 2), 1, 2).reshape(nt, 1, 2 * tm)
    return pl.pallas_call(
        functools.partial(_combine_kernel, final),
        grid=(nt,),
        in_specs=[
            pl.BlockSpec((None, 1, 2 * tm), lambda t: (t, 0, 0), memory_space=pltpu.SMEM),
            pl.BlockSpec((None, 1, 2 * tm), lambda t: (jnp.minimum(t + 1, nt - 1), 0, 0),
                         memory_space=pltpu.SMEM),
            pl.BlockSpec(memory_space=pl.ANY),
            pl.BlockSpec((tm, D_MODEL), lambda t: (t, 0)),
            pl.BlockSpec((tm, LANES), lambda t: (t, 0)),
            pl.BlockSpec((1, D_MODEL), lambda t: (0, 0)),
        ],
        out_specs=pl.BlockSpec((tm, D_MODEL), lambda t: (t, 0)),
        out_shape=jax.ShapeDtypeStruct((n, D_MODEL), F32),
        scratch_shapes=[pltpu.VMEM((2, 2 * tm, D_MODEL), F32), pltpu.SemaphoreType.DMA((2,))],
        compiler_params=pltpu.CompilerParams(
            dimension_semantics=("arbitrary",), vmem_limit_bytes=VMEM_LIMIT),
        name="combine",
    )(d, d, y, x, wt, fnw)


def _moe(x, nw, wr, br, wgu, wd, fnw, final, tm_tok, tm_exp):
    xn, idx, wt = _router(x, nw, wr, br, tm_tok)
    src, tile_e, tile_valid, dest = _moe_plan(idx, tm_exp)
    y = _experts(xn, src, tile_e, tile_valid, wgu, wd)
    return _combine(x, y, dest, wt, fnw, final, tm_exp)


def _seq_masks(n, lseq):
    r = lax.broadcasted_iota(jnp.int32, (n, n), 0)
    c = lax.broadcasted_iota(jnp.int32, (n, n), 1)
    sh = lseq.bit_length() - 1
    same = (r >> sh) == (c >> sh)
    return same & (r >= c), same & (r > c), r == c


def _valid_rows(n, lseq, t_lo, t_hi):
    row = lax.broadcasted_iota(jnp.int32, (n, 1), 0)
    t = (row & (lseq - 1)) + pl.program_id(1) * lseq
    return (t >= t_lo) & (t < t_hi)


def _pad_rows(a, n):
    return jnp.concatenate([a, jnp.zeros((n - a.shape[0], a.shape[1]), a.dtype)], axis=0)


def _gdn_kernel(Lseq, NS, t_lo, t_hi, conv_row0,
                qkv_ref, z_ref, sm_ref, cin_ref, sin_ref, cw_ref, hp_ref, na_ref, obuf_ref,
                o_ref, cout_ref, so_ref, ext_ref):
    del obuf_ref
    R = NS * Lseq
    R2 = 2 * R
    npairs = H_A // 2

    @pl.when(pl.program_id(1) == 0)
    def _():
        ext_ref[:, 0:SUBLANES, :] = cin_ref[...]
        so_ref[...] = sin_ref[...]

    valid = _valid_rows(R, Lseq, t_lo, t_hi)
    valid_l = _valid_rows(Lseq, Lseq, t_lo, t_hi)
    for s in range(NS):
        ext_ref[s, SUBLANES:SUBLANES + Lseq, :] = jnp.where(
            valid_l, qkv_ref[s * Lseq:(s + 1) * Lseq, :], 0.0)

    def conv_act(col0):
        pieces = []
        for s in range(NS):
            acc = None
            for i in range(CONV_W):
                r0 = SUBLANES - (CONV_W - 1) + i
                term = cw_ref[i:i + 1, col0:col0 + LANES] * ext_ref[s, r0:r0 + Lseq, col0:col0 + LANES]
                acc = term if acc is None else acc + term
            pieces.append(acc)
        return _silu(pieces[0] if NS == 1 else jnp.concatenate(pieces, axis=0))

    incl, _, _ = _seq_masks(R, Lseq)
    incl2, strict2, diag2 = _seq_masks(R2, Lseq)
    tri = jnp.where(incl, 1.0, 0.0).astype(BF16)
    eye2 = jnp.where(diag2, 1.0, 0.0)

    sm = sm_ref[...]
    beta_all = jnp.where(valid, _sigmoid(sm), 0.0)
    g_all = jnp.where(valid, -jnp.exp(hp_ref[0:1, :]) * _softplus(sm + hp_ref[1:2, :]), 0.0)
    G_all = _mask_dot(tri, g_all)
    GT = jnp.transpose(_pad_rows(G_all, LANES))

    q2, k2, v2, beta2, Gc2, Gr2 = [], [], [], [], [], []
    for p in range(npairs):
        qs, ks, vs = [], [], []
        for h in (2 * p, 2 * p + 1):
            q = conv_act(h * DK_A)
            k = conv_act(H_A * DK_A + h * DK_A)
            qs.append(q * lax.rsqrt(jnp.sum(q * q, axis=-1, keepdims=True) + EPS) * (DK_A ** -0.5))
            ks.append(jnp.where(valid, k * lax.rsqrt(jnp.sum(k * k, axis=-1, keepdims=True) + EPS), 0.0))
            vs.append(conv_act(2 * H_A * DK_A + h * DV_A))
        q2.append(jnp.concatenate(qs, axis=0))
        k2.append(jnp.concatenate(ks, axis=0))
        v2.append(jnp.concatenate(vs, axis=0))
        h0 = 2 * p
        beta2.append(jnp.concatenate([beta_all[:, S_BETA + h0:S_BETA + h0 + 1],
                                      beta_all[:, S_BETA + h0 + 1:S_BETA + h0 + 2]], axis=0))
        Gc2.append(jnp.concatenate([G_all[:, S_DECAY + h0:S_DECAY + h0 + 1],
                                    G_all[:, S_DECAY + h0 + 1:S_DECAY + h0 + 2]], axis=0))
        Gr2.append(jnp.concatenate([GT[S_DECAY + h0:S_DECAY + h0 + 1, :R],
                                    GT[S_DECAY + h0 + 1:S_DECAY + h0 + 2, :R]], axis=1))

    rng = range(npairs)
    kk = [_bdot_nt(k2[p], k2[p]) for p in rng]
    qk = [_bdot_nt(q2[p], k2[p]) for p in rng]
    dec = [jnp.where(incl2, jnp.exp(jnp.where(incl2, Gc2[p] - Gr2[p], 0.0)), 0.0) for p in rng]
    Bp = [-jnp.where(strict2, beta2[p] * kk[p] * dec[p], 0.0) for p in rng]
    P = [eye2 + Bp[p] for p in rng]
    n = 2
    while n < Lseq:
        Bp = [_dot3(b, b) for b in Bp]
        P = [pp + _dot3(pp, b) for pp, b in zip(P, Bp)]
        n *= 2
    eG = [jnp.exp(Gc2[p]) for p in rng]
    rhs = [jnp.concatenate([beta2[p] * v2[p], (beta2[p] * eG[p]) * k2[p]], axis=1) for p in rng]
    sol = [_dot3(P[p], rhs[p]) for p in rng]
    qk = [qk[p] * dec[p] for p in rng]
    qe = [q2[p] * eG[p] for p in rng]

    items = [(p, j, s) for p in rng for j in range(2) for s in range(NS)]

    def rows(j, s):
        return slice(j * R + s * Lseq, j * R + (s + 1) * Lseq)

    S = {it: so_ref[it[2], 2 * it[0] + it[1]] for it in items}
    both = {}
    for p, j, s in items:
        r = rows(j, s)
        both[(p, j, s)] = _bdot(jnp.concatenate([sol[p][r, DV_A:], qe[p][r]], axis=0), S[(p, j, s)])
    vnew = {}
    for p, j, s in items:
        vnew[(p, j, s)] = sol[p][rows(j, s), :DV_A] - both[(p, j, s)][:Lseq]
    for p, j, s in items:
        r = rows(j, s)
        last = j * R + (s + 1) * Lseq - 1
        GL = Gc2[p][last:last + 1]
        kd = k2[p][r] * jnp.exp(GL - Gc2[p][r])
        so_ref[s, 2 * p + j] = S[(p, j, s)] * jnp.exp(GL) + _bdot_tn(kd, vnew[(p, j, s)])
    for p in rng:
        order = [(p, j, s) for j in range(2) for s in range(NS)]
        vnew2 = jnp.concatenate([vnew[it] for it in order], axis=0)
        o2 = jnp.concatenate([both[it][Lseq:] for it in order], axis=0) + _bdot(qk[p], vnew2)
        for j in range(2):
            h = 2 * p + j
            zh = z_ref[:, h * DV_A:(h + 1) * DV_A]
            o_ref[:, h * DV_A:(h + 1) * DV_A] = _rms(o2[j * R:(j + 1) * R], na_ref[...]) * _silu(zh)

    for s in range(NS):
        cout_ref[s] = ext_ref[s, conv_row0:conv_row0 + SUBLANES, :]
        ext_ref[s, 0:SUBLANES, :] = ext_ref[s, Lseq:Lseq + SUBLANES, :]


def _gdn(big, small, cin, sin, cw, hp, na, obuf, *, nsteps, Lseq, NS, nchunks, row_block0, t_lo, t_hi):
    assert NS * Lseq == ROWS
    conv_row0 = t_hi - (nchunks - 1) * Lseq
    kern = functools.partial(_gdn_kernel, Lseq, NS, t_lo, t_hi, conv_row0)
    rb = lambda b, c: row_block0 + b * nchunks + c
    return pl.pallas_call(
        kern,
        grid=(nsteps, nchunks),
        in_specs=[
            pl.BlockSpec((ROWS, CONV_CH), lambda b, c: (rb(b, c), 0)),
            pl.BlockSpec((ROWS, H_A * DV_A), lambda b, c: (rb(b, c), C_Z // (H_A * DV_A))),
            pl.BlockSpec((ROWS, LANES), lambda b, c: (rb(b, c), 0)),
            pl.BlockSpec((NS, SUBLANES, CONV_CH), lambda b, c: (b, 0, 0)),
            pl.BlockSpec((NS, H_A, DK_A, DV_A), lambda b, c: (b, 0, 0, 0)),
            pl.BlockSpec((SUBLANES, CONV_CH), lambda b, c: (0, 0)),
            pl.BlockSpec((SUBLANES, LANES), lambda b, c: (0, 0)),
            pl.BlockSpec((1, DV_A), lambda b, c: (0, 0)),
            pl.BlockSpec(memory_space=pl.ANY),
        ],
        out_specs=[
            pl.BlockSpec((ROWS, H_A * DV_A), lambda b, c: (rb(b, c), 0)),
            pl.BlockSpec((NS, SUBLANES, CONV_CH), lambda b, c: (b, 0, 0)),
            pl.BlockSpec((NS, H_A, DK_A, DV_A), lambda b, c: (b, 0, 0, 0)),
        ],
        out_shape=[
            jax.ShapeDtypeStruct(obuf.shape, F32),
            jax.ShapeDtypeStruct((nsteps * NS, SUBLANES, CONV_CH), F32),
            jax.ShapeDtypeStruct((nsteps * NS, H_A, DK_A, DV_A), F32),
        ],
        scratch_shapes=[pltpu.VMEM((NS, SUBLANES + Lseq, CONV_CH), F32)],
        input_output_aliases={8: 0},
        compiler_params=pltpu.CompilerParams(
            dimension_semantics=("parallel", "arbitrary"), vmem_limit_bytes=VMEM_LIMIT),
        name="gdn_L%d" % Lseq,
    )(big, big, small, cin, sin, cw, hp, na, obuf)


def _gla_kernel(Lseq, NS, SB, t_lo, t_hi,
                qb_ref, kb_ref, vb_ref, rb_ref, sm_ref, sin_ref, wup_ref, bup_ref, nb_ref, obuf_ref,
                o_ref, so_ref):
    del obuf_ref
    R = NS * Lseq
    heads = range(H_B)

    @pl.when(pl.program_id(1) == 0)
    def _():
        so_ref[...] = sin_ref[...]

    nb = Lseq // SB
    assert nb == 1 or NS == 1
    n_off = SB * nb * (nb - 1) // 2
    sb_shift = SB.bit_length() - 1

    valid = _valid_rows(R, Lseq, t_lo, t_hi)
    r = lax.broadcasted_iota(jnp.int32, (R, R), 0)
    c = lax.broadcasted_iota(jnp.int32, (R, R), 1)
    same = (r | (Lseq - 1)) == (c | (Lseq - 1))
    m_row = jnp.where(same & (c <= r), 1.0, 0.0)
    m_first = jnp.where(same & (c <= (r & ~(SB - 1))), 1.0, 0.0)
    m_last = jnp.where(same, 1.0, 0.0)
    masks = jnp.concatenate([m_row, m_first, m_last], axis=0).astype(BF16)

    pre = _dot3(sm_ref[...], wup_ref[...]) + bup_ref[...]
    lg = jnp.where(valid, -_softplus(-pre) / GATE_TAU, 0.0)
    cums = _mask_dot(masks, lg)
    Bc_all, b0_all, BL_all = cums[:R], cums[R:2 * R], cums[2 * R:]

    re = lax.broadcasted_iota(jnp.int32, (R, n_off + R), 0)
    ce = lax.broadcasted_iota(jnp.int32, (R, n_off + R), 1)
    rblk = (re & (Lseq - 1)) >> sb_shift
    cd = ce - n_off
    see = ((ce >= (SB // 2) * rblk * (rblk - 1)) & (ce < (SB // 2) * rblk * (rblk + 1))) | (
        (cd >= 0) & ((cd >> sb_shift) == (re >> sb_shift)) & (cd <= re))

    def hs(a, h, w):
        return a[:, h * w:(h + 1) * w]

    q = [hs(qb_ref, h, DK_B) * (DK_B ** -0.5) for h in heads]
    k = [jnp.where(valid, hs(kb_ref, h, DK_B), 0.0) for h in heads]
    v = [hs(vb_ref, h, DV_B) for h in heads]
    Bc = [hs(Bc_all, h, DK_B) for h in heads]
    b0 = [hs(b0_all, h, DK_B) for h in heads]
    BL = [hs(BL_all, h, DK_B) for h in heads]
    qs = [q[h] * jnp.exp(Bc[h] - b0[h]) for h in heads]
    kx, vx = [], []
    for h in heads:
        kparts = [k[h][:i * SB] * jnp.exp(Bc[h][i * SB:i * SB + 1] - Bc[h][:i * SB]) for i in range(1, nb)]
        kparts.append(k[h] * jnp.exp(b0[h] - Bc[h]))
        vparts = [v[h][:i * SB] for i in range(1, nb)] + [v[h]]
        kx.append(kparts[0] if nb == 1 else jnp.concatenate(kparts, axis=0))
        vx.append(vparts[0] if nb == 1 else jnp.concatenate(vparts, axis=0))
    Ax = [jnp.where(see, _dot3(qs[h], kx[h], _dot_nt), 0.0) for h in heads]
    o_intra = [_bdot(Ax[h], vx[h]) for h in heads]
    qe = [q[h] * jnp.exp(Bc[h]) for h in heads]
    kd = [k[h] * jnp.exp(BL[h] - Bc[h]) for h in heads]
    ET = [jnp.transpose(_pad_rows(jnp.exp(BL[h]), LANES)) for h in heads]

    items = [(h, s) for h in heads for s in range(NS)]
    S = {it: so_ref[it[1], it[0]] for it in items}
    o_inter = {}
    for h, s in items:
        o_inter[(h, s)] = _bdot(qe[h][s * Lseq:(s + 1) * Lseq], S[(h, s)])
    for h, s in items:
        rows = slice(s * Lseq, (s + 1) * Lseq)
        so_ref[s, h] = S[(h, s)] * ET[h][:, s * Lseq:s * Lseq + 1] + _bdot_tn(kd[h][rows], v[h][rows])
    for h in heads:
        parts = [o_inter[(h, s)] for s in range(NS)]
        o = (parts[0] if NS == 1 else jnp.concatenate(parts, axis=0)) + o_intra[h]
        rh = hs(rb_ref, h, DV_B)
        o_ref[:, h * DV_B:(h + 1) * DV_B] = _rms(o, nb_ref[...]) * _silu(rh)


def _gla(big, small, sin, wup, bup, nb, obuf, *, nsteps, Lseq, NS, SB, nchunks, row_block0, t_lo, t_hi):
    assert NS * Lseq == ROWS
    kern = functools.partial(_gla_kernel, Lseq, NS, SB, t_lo, t_hi)
    r = lambda b, c: row_block0 + b * nchunks + c
    return pl.pallas_call(
        kern,
        grid=(nsteps, nchunks),
        in_specs=[
            pl.BlockSpec((ROWS, H_B * DK_B), lambda b, c: (r(b, c), C_QB // (H_B * DK_B))),
            pl.BlockSpec((ROWS, H_B * DK_B), lambda b, c: (r(b, c), C_KB // (H_B * DK_B))),
            pl.BlockSpec((ROWS, H_B * DV_B), lambda b, c: (r(b, c), C_VB // (H_B * DV_B))),
            pl.BlockSpec((ROWS, H_B * DV_B), lambda b, c: (r(b, c), C_RB // (H_B * DV_B))),
            pl.BlockSpec((ROWS, LANES), lambda b, c: (r(b, c), 0)),
            pl.BlockSpec((NS, H_B, DK_B, DV_B), lambda b, c: (b, 0, 0, 0)),
            pl.BlockSpec((LANES, H_B * DK_B), lambda b, c: (0, 0)),
            pl.BlockSpec((1, H_B * DK_B), lambda b, c: (0, 0)),
            pl.BlockSpec((1, DV_B), lambda b, c: (0, 0)),
            pl.BlockSpec(memory_space=pl.ANY),
        ],
        out_specs=[
            pl.BlockSpec((ROWS, H_B * DV_B), lambda b, c: (r(b, c), 0)),
            pl.BlockSpec((NS, H_B, DK_B, DV_B), lambda b, c: (b, 0, 0, 0)),
        ],
        out_shape=[
            jax.ShapeDtypeStruct(obuf.shape, F32),
            jax.ShapeDtypeStruct((nsteps * NS, H_B, DK_B, DV_B), F32),
        ],
        input_output_aliases={9: 0},
        compiler_params=pltpu.CompilerParams(
            dimension_semantics=("parallel", "arbitrary"), vmem_limit_bytes=VMEM_LIMIT),
        name="gla_L%d" % Lseq,
    )(big, big, big, big, small, sin, wup, bup, nb, obuf)


def _split_cols(w):
    offs, s = [], 0
    for c in COL_SIZES[:-1]:
        s += c
        offs.append(s)
    return jnp.split(w, offs, axis=-1)


def kernel(x_prompt, x_sample, state_delta, state_conv, state_gla, meta_tokens, norm_mix, w_in, conv_w, a_log, dt_bias, norm_a, gla_gate_up, gla_gate_bias, norm_b, w_proj_a, w_proj_b, w_out, norm_ffn, router_group_w, router_group_b, router_expert_w, router_expert_b, expert_w_gate, expert_w_up, expert_w_down, norm_final):
    Bp, Tp, D = x_prompt.shape
    Bs, Ts, _ = x_sample.shape
    depth = w_in.shape[0]
    pad_p = (-N_META) % CHUNK
    Tpp = pad_p + N_META + Tp
    ncp = Tpp // CHUNK
    np_rows = Bp * Tpp
    ns_rows = Bs * SAMPLE_L
    n = np_rows + ns_rows
    seq_per_step = ROWS // SAMPLE_L
    dt = x_prompt.dtype
    xp = jnp.concatenate([
        jnp.zeros((Bp, pad_p, D), dt),
        jnp.broadcast_to(meta_tokens.astype(dt)[None], (Bp, N_META, D)),
        x_prompt], axis=1).reshape(np_rows, D)
    xs = jnp.pad(x_sample, ((0, 0), (0, SAMPLE_L - Ts), (0, 0))).reshape(ns_rows, D)
    x = jnp.concatenate([xp, xs], axis=0).astype(F32)

    tm_big = 1280 if n % 1280 == 0 else 512
    tm = 512

    zeros_conv = jnp.zeros((Bp, SUBLANES, CONV_CH), F32)
    zeros_sa = jnp.zeros((Bp, H_A, DK_A, DV_A), F32)
    zeros_sb = jnp.zeros((Bp, H_B, DK_B, DV_B), F32)
    prompt_args = dict(nsteps=Bp, Lseq=CHUNK, NS=1, nchunks=ncp, row_block0=0, t_lo=pad_p, t_hi=Tpp)
    sample_args = dict(nsteps=Bs // seq_per_step, Lseq=SAMPLE_L, NS=seq_per_step, nchunks=1,
                       row_block0=np_rows // ROWS, t_lo=0, t_hi=Ts)

    sa_p, conv_p, sb_p, sa_s, conv_s, sb_s = [], [], [], [], [], []
    for l in range(depth):
        qkv, z, b_a, a_a, q_b, k_b, v_b, r_b, lr, g_a, g_b = _split_cols(w_in[l])
        wbig = jnp.concatenate([qkv, z, q_b, k_b, v_b, r_b, g_a, g_b], axis=1).astype(BF16)
        wsmall = jnp.concatenate(
            [b_a, a_a, lr, jnp.zeros((D, LANES - 2 * H_A - GATE_RANK), F32)], axis=1).astype(F32)
        big, small = _inproj(x, norm_mix[l][None].astype(F32), wbig, wsmall, tm_big, 512)

        cw = jnp.pad(conv_w[l].astype(F32), ((0, SUBLANES - CONV_W), (0, 0)))
        hp = jnp.zeros((SUBLANES, LANES), F32)
        hp = hp.at[0, S_DECAY:S_DECAY + H_A].set(a_log[l].astype(F32))
        hp = hp.at[1, S_DECAY:S_DECAY + H_A].set(dt_bias[l].astype(F32))
        na = norm_a[l][None].astype(F32)
        wup = jnp.zeros((LANES, H_B * DK_B), F32).at[S_LR:S_LR + GATE_RANK].set(gla_gate_up[l].astype(F32))
        bup = gla_gate_bias[l][None].astype(F32)
        nb = norm_b[l][None].astype(F32)

        cin_s = jnp.pad(state_conv[l].astype(F32), ((0, 0), (SUBLANES - (CONV_W - 1), 0), (0, 0)))
        oa = jnp.zeros((n, H_A * DV_A), F32)
        ob = jnp.zeros((n, H_B * DV_B), F32)
        oa, c_p, s_ap = _gdn(big, small, zeros_conv, zeros_sa, cw, hp, na, oa, **prompt_args)
        oa, c_s, s_as = _gdn(big, small, cin_s, state_delta[l].astype(F32), cw, hp, na, oa, **sample_args)
        ob, s_bp = _gla(big, small, zeros_sb, wup, bup, nb, ob, SB=GLA_SUB, **prompt_args)
        ob, s_bs = _gla(big, small, state_gla[l].astype(F32), wup, bup, nb, ob, SB=SAMPLE_L, **sample_args)
        x = _outproj(oa, ob, big, x, w_proj_a[l].astype(BF16), w_proj_b[l].astype(BF16),
                     w_out[l].astype(BF16), tm)

        wr = jnp.concatenate([router_expert_w[l], router_group_w[l],
                              jnp.zeros((D, LANES - N_EXPERTS - N_GROUPS), F32)], axis=1).astype(F32)
        br = jnp.concatenate([router_expert_b[l], router_group_b[l],
                              jnp.zeros((LANES - N_EXPERTS - N_GROUPS,), F32)])[None].astype(F32)
        wgu = jnp.concatenate([expert_w_gate[l], expert_w_up[l]], axis=-1).astype(BF16)
        wd = expert_w_down[l].astype(BF16)
        x = _moe(x, norm_ffn[l][None].astype(F32), wr, br, wgu, wd,
                 norm_final[None].astype(F32), l == depth - 1, tm, MOE_TILE)

        sa_p.append(s_ap)
        conv_p.append(c_p[:, SUBLANES - (CONV_W - 1):])
        sb_p.append(s_bp)
        sa_s.append(s_as)
        conv_s.append(c_s[:, SUBLANES - (CONV_W - 1):])
        sb_s.append(s_bs)

    y = x
    y_prompt = y[:np_rows].reshape(Bp, Tpp, D)[:, pad_p + N_META:].astype(dt)
    y_sample = y[np_rows:].reshape(Bs, SAMPLE_L, D)[:, :Ts].astype(x_sample.dtype)
    return (y_prompt, y_sample,
            jnp.stack(sa_p).astype(dt), jnp.stack(conv_p).astype(dt), jnp.stack(sb_p).astype(dt),
            jnp.stack(sa_s).astype(state_delta.dtype), jnp.stack(conv_s).astype(state_conv.dtype),
            jnp.stack(sb_s).astype(state_gla.dtype))
```

```python
import functools

import jax
import jax.numpy as jnp
from jax import lax
from jax.experimental import pallas as pl
from jax.experimental.pallas import tpu as pltpu

F32 = jnp.float32
BF16 = jnp.bfloat16

D_MODEL = 1024
N_META = 16
CHUNK = 64
EPS = 1e-6
H_A, DK_A, DV_A = 8, 128, 128
CONV_W = 4
CONV_CH = 2 * H_A * DK_A + H_A * DV_A
H_B, DK_B, DV_B = 4, 128, 256
GATE_RANK = 16
GATE_TAU = 16.0
N_GROUPS = 4
EXPERTS_PER_GROUP = 8
N_EXPERTS = N_GROUPS * EXPERTS_PER_GROUP
D_FF_E = 256
COL_SIZES = (CONV_CH, H_A * DV_A, H_A, H_A, H_B * DK_B, H_B * DK_B, H_B * DV_B, H_B * DV_B,
             GATE_RANK, D_MODEL, D_MODEL)

LANES = 128
SUBLANES = 8
VMEM_LIMIT = 48 * 1024 * 1024

C_QKV = 0
C_Z = C_QKV + CONV_CH
C_QB = C_Z + H_A * DV_A
C_KB = C_QB + H_B * DK_B
C_VB = C_KB + H_B * DK_B
C_RB = C_VB + H_B * DV_B
C_GA = C_RB + H_B * DV_B
C_GB = C_GA + D_MODEL
C_BIG = C_GB + D_MODEL
S_BETA = 0
S_DECAY = H_A
S_LR = 2 * H_A

SAMPLE_L = SUBLANES
ROWS = CHUNK
GLA_SUB = 16
MOE_TILE = 256


def _dot(a, b, prec=None):
    return jnp.dot(a, b, preferred_element_type=F32, precision=prec)


def _dot_nt(a, b, prec=None):
    return lax.dot_general(a, b, (((1,), (1,)), ((), ())), preferred_element_type=F32, precision=prec)


def _dot_tn(a, b, prec=None):
    return lax.dot_general(a, b, (((0,), (0,)), ((), ())), preferred_element_type=F32, precision=prec)


def _bdot(a, b):
    return _dot(a.astype(BF16), b.astype(BF16))


def _bdot_nt(a, b):
    return _dot_nt(a.astype(BF16), b.astype(BF16))


def _bdot_tn(a, b):
    return _dot_tn(a.astype(BF16), b.astype(BF16))


def _split2(a):
    hi = a.astype(BF16)
    return hi, (a - hi.astype(F32)).astype(BF16)


def _dot3(a, b, dot=_dot):
    ah, al = _split2(a)
    bh, bl = _split2(b)
    return dot(ah, bh) + dot(al, bh) + dot(ah, bl)


def _mask_dot(m, b):
    hi = b.astype(BF16)
    r = b - hi.astype(F32)
    mid = r.astype(BF16)
    lo = (r - mid.astype(F32)).astype(BF16)
    return _dot(m, hi) + _dot(m, mid) + _dot(m, lo)


def _sigmoid(x):
    return 1.0 / (1.0 + jnp.exp(-x))


def _silu(x):
    return x * _sigmoid(x)


def _softplus(x):
    return jnp.maximum(x, 0.0) + jnp.log(1.0 + jnp.exp(-jnp.abs(x)))


def _rms(x, w):
    return x * lax.rsqrt(jnp.mean(x * x, axis=-1, keepdims=True) + EPS) * w


def _inproj_kernel(x_ref, nw_ref, wb_ref, ws_ref, big_ref, small_ref, xn_ref):
    @pl.when(pl.program_id(1) == 0)
    def _():
        xn = _rms(x_ref[...], nw_ref[...])
        xn_ref[...] = xn.astype(BF16)
        small_ref[...] = _dot3(xn, ws_ref[...])

    big_ref[...] = _dot(xn_ref[...], wb_ref[...])


def _inproj(x, nw, wbig, wsmall, tm, tn):
    n = x.shape[0]
    return pl.pallas_call(
        _inproj_kernel,
        grid=(n // tm, C_BIG // tn),
        in_specs=[
            pl.BlockSpec((tm, D_MODEL), lambda i, j: (i, 0)),
            pl.BlockSpec((1, D_MODEL), lambda i, j: (0, 0)),
            pl.BlockSpec((D_MODEL, tn), lambda i, j: (0, j)),
            pl.BlockSpec((D_MODEL, LANES), lambda i, j: (0, 0)),
        ],
        out_specs=[
            pl.BlockSpec((tm, tn), lambda i, j: (i, j)),
            pl.BlockSpec((tm, LANES), lambda i, j: (i, 0)),
        ],
        out_shape=[
            jax.ShapeDtypeStruct((n, C_BIG), F32),
            jax.ShapeDtypeStruct((n, LANES), F32),
        ],
        scratch_shapes=[pltpu.VMEM((tm, D_MODEL), BF16)],
        compiler_params=pltpu.CompilerParams(
            dimension_semantics=("parallel", "arbitrary"), vmem_limit_bytes=VMEM_LIMIT),
        name="inproj",
    )(x, nw, wbig, wsmall)


def _outproj_kernel(oa_ref, ob_ref, ga_ref, gb_ref, x_ref, wa_ref, wb_ref, wo_ref, out_ref):
    ya = _dot(oa_ref[...].astype(BF16), wa_ref[...])
    yb = _dot(ob_ref[...].astype(BF16), wb_ref[...])
    merged = _sigmoid(ga_ref[...]) * ya + _sigmoid(gb_ref[...]) * yb
    out_ref[...] = x_ref[...] + _dot(merged.astype(BF16), wo_ref[...])


def _outproj(oa, ob, big, x, wa, wb, wo, tm):
    n = x.shape[0]
    row = lambda i: (i, 0)
    full = lambda i: (0, 0)
    return pl.pallas_call(
        _outproj_kernel,
        grid=(n // tm,),
        in_specs=[
            pl.BlockSpec((tm, D_MODEL), row),
            pl.BlockSpec((tm, D_MODEL), row),
            pl.BlockSpec((tm, D_MODEL), lambda i: (i, C_GA // D_MODEL)),
            pl.BlockSpec((tm, D_MODEL), lambda i: (i, C_GB // D_MODEL)),
            pl.BlockSpec((tm, D_MODEL), row),
            pl.BlockSpec((D_MODEL, D_MODEL), full),
            pl.BlockSpec((D_MODEL, D_MODEL), full),
            pl.BlockSpec((D_MODEL, D_MODEL), full),
        ],
        out_specs=pl.BlockSpec((tm, D_MODEL), row),
        out_shape=jax.ShapeDtypeStruct((n, D_MODEL), F32),
        compiler_params=pltpu.CompilerParams(
            dimension_semantics=("parallel",), vmem_limit_bytes=VMEM_LIMIT),
        name="outproj",
    )(oa, ob, big, big, x, wa, wb, wo)


def _route(logits):
    lane = lax.broadcasted_iota(jnp.int32, logits.shape, 1)
    neg = jnp.float32(-jnp.inf)
    big = jnp.int32(1 << 20)
    gmask = (lane >= N_EXPERTS) & (lane < N_EXPERTS + N_GROUPS)
    gl = jnp.where(gmask, logits, neg)
    gmax = jnp.max(gl, axis=-1, keepdims=True)
    gsel = jnp.min(jnp.where(gl == gmax, lane - N_EXPERTS, big), axis=-1, keepdims=True)
    pgsel = 1.0 / jnp.sum(jnp.where(gmask, jnp.exp(gl - gmax), 0.0), axis=-1, keepdims=True)
    lo = gsel * EXPERTS_PER_GROUP
    emask = (lane >= lo) & (lane < lo + EXPERTS_PER_GROUP)
    el = jnp.where(emask, logits, neg)
    m1 = jnp.max(el, axis=-1, keepdims=True)
    i1 = jnp.min(jnp.where(el == m1, lane, big), axis=-1, keepdims=True)
    el2 = jnp.where(lane == i1, neg, el)
    m2 = jnp.max(el2, axis=-1, keepdims=True)
    i2 = jnp.min(jnp.where(el2 == m2, lane, big), axis=-1, keepdims=True)
    r = jnp.exp(m2 - m1)
    w1 = pgsel / (1.0 + r)
    w2 = pgsel * r / (1.0 + r)
    return i1, i2, w1, w2


def _router_kernel(x_ref, nw_ref, wr_ref, br_ref, xn_ref, idx_ref, wt_ref, cnt_ref):
    tm = x_ref.shape[0]
    xn = _rms(x_ref[...], nw_ref[...])
    xn_ref[...] = xn
    i1, i2, w1, w2 = _route(_dot3(xn, wr_ref[...]) + br_ref[...])
    lane = lax.broadcasted_iota(jnp.int32, (tm, LANES), 1)
    onehot = jnp.where((lane == i1) | (lane == i2), 1.0, 0.0)
    r = lax.broadcasted_iota(jnp.int32, (tm, tm), 0)
    c = lax.broadcasted_iota(jnp.int32, (tm, tm), 1)
    earlier = jnp.where(c < r, 1.0, 0.0).astype(BF16)
    cum = _dot(earlier, onehot.astype(BF16))
    rank1 = jnp.sum(jnp.where(lane == i1, cum, 0.0), axis=-1, keepdims=True).astype(jnp.int32)
    rank2 = jnp.sum(jnp.where(lane == i2, cum, 0.0), axis=-1, keepdims=True).astype(jnp.int32)
    idx_ref[...] = jnp.where(lane == 0, i1, jnp.where(lane == 1, i2, jnp.where(
        lane == 2, rank1, jnp.where(lane == 3, rank2, 0))))
    wt_ref[...] = jnp.where(lane == 0, w1, jnp.where(lane == 1, w2, 0.0))
    counts = jnp.sum(onehot, axis=0, keepdims=True).astype(jnp.int32)
    cnt_ref[...] = jnp.broadcast_to(counts, cnt_ref.shape)


def _router(x, nw, wr, br, tm):
    n = x.shape[0]
    row = lambda i: (i, 0)
    full = lambda i: (0, 0)
    return pl.pallas_call(
        _router_kernel,
        grid=(n // tm,),
        in_specs=[
            pl.BlockSpec((tm, D_MODEL), row),
            pl.BlockSpec((1, D_MODEL), full),
            pl.BlockSpec((D_MODEL, LANES), full),
            pl.BlockSpec((1, LANES), full),
        ],
        out_specs=[
            pl.BlockSpec((tm, D_MODEL), row),
            pl.BlockSpec((tm, LANES), row),
            pl.BlockSpec((tm, LANES), row),
            pl.BlockSpec((None, SUBLANES, LANES), lambda i: (i, 0, 0)),
        ],
        out_shape=[
            jax.ShapeDtypeStruct((n, D_MODEL), F32),
            jax.ShapeDtypeStruct((n, LANES), jnp.int32),
            jax.ShapeDtypeStruct((n, LANES), F32),
            jax.ShapeDtypeStruct((n // tm, SUBLANES, LANES), jnp.int32),
        ],
        compiler_params=pltpu.CompilerParams(
            dimension_semantics=("parallel",), vmem_limit_bytes=VMEM_LIMIT),
        name="router",
    )(x, nw, wr, br)


def _moe_plan(idx, cnt, tm_tok, tm_exp):
    n = idx.shape[0]
    nt = 2 * n // tm_exp + N_EXPERTS
    counts = cnt[:, 0, :N_EXPERTS]
    cpad = (jnp.sum(counts, axis=0) + tm_exp - 1) // tm_exp * tm_exp
    ends = jnp.cumsum(cpad)
    base = (ends - cpad)[None] + jnp.cumsum(counts, axis=0) - counts
    base_tok = jnp.repeat(base, tm_tok, axis=0)[:, None, :]
    hit = idx[:, :2, None] == jnp.arange(N_EXPERTS, dtype=jnp.int32)
    dest = (jnp.sum(jnp.where(hit, base_tok, 0), axis=-1) + idx[:, 2:4]).astype(jnp.int32)
    tile_start = jnp.arange(nt, dtype=jnp.int32) * tm_exp
    tile_e = jnp.minimum(jnp.searchsorted(ends, tile_start, side="right"), N_EXPERTS - 1).astype(jnp.int32)
    tile_valid = (tile_start < ends[-1]).astype(jnp.int32)
    return dest, tile_e, tile_valid


def _slot_major(dest, tm):
    nt = dest.shape[0] // tm
    return jnp.swapaxes(dest.reshape(nt, tm, 2), 1, 2).reshape(nt, 1, 2 * tm)


def _dispatch_kernel(d_ref, xn_ref, xs_in_hbm, xs_hbm, sem):
    del xs_in_hbm
    tm = xn_ref.shape[0]

    def copy(i, j, dst_row):
        return pltpu.make_async_copy(xn_ref.at[pl.ds(i, 1), :], xs_hbm.at[pl.ds(dst_row, 1), :], sem.at[j])

    def start(i, carry):
        for j in range(2):
            copy(i, j, d_ref[0, j * tm + i]).start(priority=j)
        return carry

    def wait(i, carry):
        for j in range(2):
            copy(i, j, 0).wait()
        return carry

    lax.fori_loop(0, tm, start, 0, unroll=4)
    lax.fori_loop(0, tm, wait, 0, unroll=4)


def _dispatch(xn, dest, nrows, tm):
    n = xn.shape[0]
    return pl.pallas_call(
        _dispatch_kernel,
        grid=(n // tm,),
        in_specs=[
            pl.BlockSpec((None, 1, 2 * tm), lambda t: (t, 0, 0), memory_space=pltpu.SMEM),
            pl.BlockSpec((tm, D_MODEL), lambda t: (t, 0)),
            pl.BlockSpec(memory_space=pl.ANY),
        ],
        out_specs=pl.BlockSpec(memory_space=pl.ANY),
        out_shape=jax.ShapeDtypeStruct((nrows, D_MODEL), F32),
        scratch_shapes=[pltpu.SemaphoreType.DMA((2,))],
        input_output_aliases={2: 0},
        compiler_params=pltpu.CompilerParams(
            dimension_semantics=("arbitrary",), vmem_limit_bytes=VMEM_LIMIT),
        name="dispatch",
    )(_slot_major(dest, tm), xn, jnp.zeros((nrows, D_MODEL), F32))


def _row_gather(idx_ref, src_hbm, buf, sem, slot, nrows):
    def body(r, carry):
        for j in range(2):
            rr = 2 * r + j
            pltpu.make_async_copy(src_hbm.at[pl.ds(idx_ref[0, rr], 1), :],
                                  buf.at[slot, pl.ds(rr, 1), :], sem.at[slot]).start(priority=j)
        return carry
    lax.fori_loop(0, nrows // 2, body, 0, unroll=4)


def _row_gather_wait(src_hbm, buf, sem, slot, nrows):
    def body(r, carry):
        pltpu.make_async_copy(src_hbm.at[pl.ds(0, 1), :],
                              buf.at[slot, pl.ds(r, 1), :], sem.at[slot]).wait()
        return carry
    lax.fori_loop(0, nrows, body, 0, unroll=8)


def _expert_kernel(te_ref, tv_ref, xs_ref, wgu_ref, wd_ref, y_ref):
    t = pl.program_id(0)

    @pl.when(tv_ref[t] == 1)
    def _():
        h = _dot(xs_ref[...].astype(BF16), wgu_ref[...])
        a = _silu(h[:, :D_FF_E]) * h[:, D_FF_E:]
        y_ref[...] = _dot(a.astype(BF16), wd_ref[...])

    @pl.when(tv_ref[t] == 0)
    def _():
        y_ref[...] = jnp.zeros_like(y_ref)


def _experts(xs, tile_e, tile_valid, wgu, wd, tm):
    nt = xs.shape[0] // tm
    return pl.pallas_call(
        _expert_kernel,
        grid_spec=pltpu.PrefetchScalarGridSpec(
            num_scalar_prefetch=2,
            grid=(nt,),
            in_specs=[
                pl.BlockSpec((tm, D_MODEL), lambda t, te, tv: (t, 0)),
                pl.BlockSpec((None, D_MODEL, 2 * D_FF_E), lambda t, te, tv: (te[t], 0, 0)),
                pl.BlockSpec((None, D_FF_E, D_MODEL), lambda t, te, tv: (te[t], 0, 0)),
            ],
            out_specs=pl.BlockSpec((tm, D_MODEL), lambda t, te, tv: (t, 0)),
        ),
        out_shape=jax.ShapeDtypeStruct((nt * tm, D_MODEL), F32),
        compiler_params=pltpu.CompilerParams(
            dimension_semantics=("arbitrary",), vmem_limit_bytes=VMEM_LIMIT),
        name="experts",
    )(tile_e, tile_valid, xs, wgu, wd)


def _combine_kernel(final, d_ref, d_next_ref, y_hbm, x_ref, wt_ref, fnw_ref, out_ref, ybuf, sem):
    t = pl.program_id(0)
    nt = pl.num_programs(0)
    tm = x_ref.shape[0]
    slot = t & 1

    @pl.when(t == 0)
    def _():
        _row_gather(d_ref, y_hbm, ybuf, sem, 0, 2 * tm)

    @pl.when(t + 1 < nt)
    def _():
        _row_gather(d_next_ref, y_hbm, ybuf, sem, 1 - slot, 2 * tm)

    _row_gather_wait(y_hbm, ybuf, sem, slot, 2 * tm)
    wt = wt_ref[...]
    out = x_ref[...] + wt[:, 0:1] * ybuf[slot, 0:tm, :] + wt[:, 1:2] * ybuf[slot, tm:2 * tm, :]
    out_ref[...] = _rms(out, fnw_ref[...]) if final else out


def _combine(x, y, dest, wt, fnw, final, tm):
    n = x.shape[0]
    nt = n // tm
    d = _slot_major(dest, tm)
    return pl.pallas_call(
        functools.partial(_combine_kernel, final),
        grid=(nt,),
        in_specs=[
            pl.BlockSpec((None, 1, 2 * tm), lambda t: (t, 0, 0), memory_space=pltpu.SMEM),
            pl.BlockSpec((None, 1, 2 * tm), lambda t: (jnp.minimum(t + 1, nt - 1), 0, 0),
                         memory_space=pltpu.SMEM),
            pl.BlockSpec(memory_space=pl.ANY),
            pl.BlockSpec((tm, D_MODEL), lambda t: (t, 0)),
            pl.BlockSpec((tm, LANES), lambda t: (t, 0)),
            pl.BlockSpec((1, D_MODEL), lambda t: (0, 0)),
        ],
        out_specs=pl.BlockSpec((tm, D_MODEL), lambda t: (t, 0)),
        out_shape=jax.ShapeDtypeStruct((n, D_MODEL), F32),
        scratch_shapes=[pltpu.VMEM((2, 2 * tm, D_MODEL), F32), pltpu.SemaphoreType.DMA((2,))],
        compiler_params=pltpu.CompilerParams(
            dimension_semantics=("arbitrary",), vmem_limit_bytes=VMEM_LIMIT),
        name="combine",
    )(d, d, y, x, wt, fnw)


def _moe(x, nw, wr, br, wgu, wd, fnw, final, tm_tok, tm_exp):
    xn, idx, wt, cnt = _router(x, nw, wr, br, tm_tok)
    dest, tile_e, tile_valid = _moe_plan(idx, cnt, tm_tok, tm_exp)
    xs = _dispatch(xn, dest, tile_e.shape[0] * tm_exp, tm_tok)
    y = _experts(xs, tile_e, tile_valid, wgu, wd, tm_exp)
    return _combine(x, y, dest, wt, fnw, final, tm_exp)


def _seq_masks(n, lseq):
    r = lax.broadcasted_iota(jnp.int32, (n, n), 0)
    c = lax.broadcasted_iota(jnp.int32, (n, n), 1)
    sh = lseq.bit_length() - 1
    same = (r >> sh) == (c >> sh)
    return same & (r >= c), same & (r > c), r == c


def _valid_rows(n, lseq, t_lo, t_hi):
    row = lax.broadcasted_iota(jnp.int32, (n, 1), 0)
    t = (row & (lseq - 1)) + pl.program_id(1) * lseq
    return (t >= t_lo) & (t < t_hi)


def _pad_rows(a, n):
    return jnp.concatenate([a, jnp.zeros((n - a.shape[0], a.shape[1]), a.dtype)], axis=0)


def _gdn_kernel(Lseq, NS, t_lo, t_hi, conv_row0,
                qkv_ref, z_ref, sm_ref, cin_ref, sin_ref, cw_ref, hp_ref, na_ref, obuf_ref,
                o_ref, cout_ref, so_ref, ext_ref):
    del obuf_ref
    R = NS * Lseq
    R2 = 2 * R
    npairs = H_A // 2

    @pl.when(pl.program_id(1) == 0)
    def _():
        ext_ref[:, 0:SUBLANES, :] = cin_ref[...]
        so_ref[...] = sin_ref[...]

    valid = _valid_rows(R, Lseq, t_lo, t_hi)
    valid_l = _valid_rows(Lseq, Lseq, t_lo, t_hi)
    for s in range(NS):
        ext_ref[s, SUBLANES:SUBLANES + Lseq, :] = jnp.where(
            valid_l, qkv_ref[s * Lseq:(s + 1) * Lseq, :], 0.0)

    def conv_act(col0):
        pieces = []
        for s in range(NS):
            acc = None
            for i in range(CONV_W):
                r0 = SUBLANES - (CONV_W - 1) + i
                term = cw_ref[i:i + 1, col0:col0 + LANES] * ext_ref[s, r0:r0 + Lseq, col0:col0 + LANES]
                acc = term if acc is None else acc + term
            pieces.append(acc)
        return _silu(pieces[0] if NS == 1 else jnp.concatenate(pieces, axis=0))

    incl, _, _ = _seq_masks(R, Lseq)
    incl2, strict2, diag2 = _seq_masks(R2, Lseq)
    tri = jnp.where(incl, 1.0, 0.0).astype(BF16)
    eye2 = jnp.where(diag2, 1.0, 0.0)

    sm = sm_ref[...]
    beta_all = jnp.where(valid, _sigmoid(sm), 0.0)
    g_all = jnp.where(valid, -jnp.exp(hp_ref[0:1, :]) * _softplus(sm + hp_ref[1:2, :]), 0.0)
    G_all = _mask_dot(tri, g_all)
    GT = jnp.transpose(_pad_rows(G_all, LANES))

    q2, k2, v2, beta2, Gc2, Gr2 = [], [], [], [], [], []
    for p in range(npairs):
        qs, ks, vs = [], [], []
        for h in (2 * p, 2 * p + 1):
            q = conv_act(h * DK_A)
            k = conv_act(H_A * DK_A + h * DK_A)
            qs.append(q * lax.rsqrt(jnp.sum(q * q, axis=-1, keepdims=True) + EPS) * (DK_A ** -0.5))
            ks.append(jnp.where(valid, k * lax.rsqrt(jnp.sum(k * k, axis=-1, keepdims=True) + EPS), 0.0))
            vs.append(conv_act(2 * H_A * DK_A + h * DV_A))
        q2.append(jnp.concatenate(qs, axis=0))
        k2.append(jnp.concatenate(ks, axis=0))
        v2.append(jnp.concatenate(vs, axis=0))
        h0 = 2 * p
        beta2.append(jnp.concatenate([beta_all[:, S_BETA + h0:S_BETA + h0 + 1],
                                      beta_all[:, S_BETA + h0 + 1:S_BETA + h0 + 2]], axis=0))
        Gc2.append(jnp.concatenate([G_all[:, S_DECAY + h0:S_DECAY + h0 + 1],
                                    G_all[:, S_DECAY + h0 + 1:S_DECAY + h0 + 2]], axis=0))
        Gr2.append(jnp.concatenate([GT[S_DECAY + h0:S_DECAY + h0 + 1, :R],
                                    GT[S_DECAY + h0 + 1:S_DECAY + h0 + 2, :R]], axis=1))

    rng = range(npairs)
    kk = [_bdot_nt(k2[p], k2[p]) for p in rng]
    qk = [_bdot_nt(q2[p], k2[p]) for p in rng]
    dec = [jnp.where(incl2, jnp.exp(jnp.where(incl2, Gc2[p] - Gr2[p], 0.0)), 0.0) for p in rng]
    Bp = [-jnp.where(strict2, beta2[p] * kk[p] * dec[p], 0.0) for p in rng]
    P = [eye2 + Bp[p] for p in rng]
    n = 2
    while n < Lseq:
        Bp = [_dot3(b, b) for b in Bp]
        P = [pp + _dot3(pp, b) for pp, b in zip(P, Bp)]
        n *= 2
    eG = [jnp.exp(Gc2[p]) for p in rng]
    rhs = [jnp.concatenate([beta2[p] * v2[p], (beta2[p] * eG[p]) * k2[p]], axis=1) for p in rng]
    sol = [_dot3(P[p], rhs[p]) for p in rng]
    qk = [qk[p] * dec[p] for p in rng]
    qe = [q2[p] * eG[p] for p in rng]

    items = [(p, j, s) for p in rng for j in range(2) for s in range(NS)]

    def rows(j, s):
        return slice(j * R + s * Lseq, j * R + (s + 1) * Lseq)

    S = {it: so_ref[it[2], 2 * it[0] + it[1]] for it in items}
    both = {}
    for p, j, s in items:
        r = rows(j, s)
        both[(p, j, s)] = _bdot(jnp.concatenate([sol[p][r, DV_A:], qe[p][r]], axis=0), S[(p, j, s)])
    vnew = {}
    for p, j, s in items:
        vnew[(p, j, s)] = sol[p][rows(j, s), :DV_A] - both[(p, j, s)][:Lseq]
    for p, j, s in items:
        r = rows(j, s)
        last = j * R + (s + 1) * Lseq - 1
        GL = Gc2[p][last:last + 1]
        kd = k2[p][r] * jnp.exp(GL - Gc2[p][r])
        so_ref[s, 2 * p + j] = S[(p, j, s)] * jnp.exp(GL) + _bdot_tn(kd, vnew[(p, j, s)])
    for p in rng:
        order = [(p, j, s) for j in range(2) for s in range(NS)]
        vnew2 = jnp.concatenate([vnew[it] for it in order], axis=0)
        o2 = jnp.concatenate([both[it][Lseq:] for it in order], axis=0) + _bdot(qk[p], vnew2)
        for j in range(2):
            h = 2 * p + j
            zh = z_ref[:, h * DV_A:(h + 1) * DV_A]
            o_ref[:, h * DV_A:(h + 1) * DV_A] = _rms(o2[j * R:(j + 1) * R], na_ref[...]) * _silu(zh)

    for s in range(NS):
        cout_ref[s] = ext_ref[s, conv_row0:conv_row0 + SUBLANES, :]
        ext_ref[s, 0:SUBLANES, :] = ext_ref[s, Lseq:Lseq + SUBLANES, :]


def _gdn(big, small, cin, sin, cw, hp, na, obuf, *, nsteps, Lseq, NS, nchunks, row_block0, t_lo, t_hi):
    assert NS * Lseq == ROWS
    conv_row0 = t_hi - (nchunks - 1) * Lseq
    kern = functools.partial(_gdn_kernel, Lseq, NS, t_lo, t_hi, conv_row0)
    rb = lambda b, c: row_block0 + b * nchunks + c
    return pl.pallas_call(
        kern,
        grid=(nsteps, nchunks),
        in_specs=[
            pl.BlockSpec((ROWS, CONV_CH), lambda b, c: (rb(b, c), 0)),
            pl.BlockSpec((ROWS, H_A * DV_A), lambda b, c: (rb(b, c), C_Z // (H_A * DV_A))),
            pl.BlockSpec((ROWS, LANES), lambda b, c: (rb(b, c), 0)),
            pl.BlockSpec((NS, SUBLANES, CONV_CH), lambda b, c: (b, 0, 0)),
            pl.BlockSpec((NS, H_A, DK_A, DV_A), lambda b, c: (b, 0, 0, 0)),
            pl.BlockSpec((SUBLANES, CONV_CH), lambda b, c: (0, 0)),
            pl.BlockSpec((SUBLANES, LANES), lambda b, c: (0, 0)),
            pl.BlockSpec((1, DV_A), lambda b, c: (0, 0)),
            pl.BlockSpec(memory_space=pl.ANY),
        ],
        out_specs=[
            pl.BlockSpec((ROWS, H_A * DV_A), lambda b, c: (rb(b, c), 0)),
            pl.BlockSpec((NS, SUBLANES, CONV_CH), lambda b, c: (b, 0, 0)),
            pl.BlockSpec((NS, H_A, DK_A, DV_A), lambda b, c: (b, 0, 0, 0)),
        ],
        out_shape=[
            jax.ShapeDtypeStruct(obuf.shape, F32),
            jax.ShapeDtypeStruct((nsteps * NS, SUBLANES, CONV_CH), F32),
            jax.ShapeDtypeStruct((nsteps * NS, H_A, DK_A, DV_A), F32),
        ],
        scratch_shapes=[pltpu.VMEM((NS, SUBLANES + Lseq, CONV_CH), F32)],
        input_output_aliases={8: 0},
        compiler_params=pltpu.CompilerParams(
            dimension_semantics=("parallel", "arbitrary"), vmem_limit_bytes=VMEM_LIMIT),
        name="gdn_L%d" % Lseq,
    )(big, big, small, cin, sin, cw, hp, na, obuf)


def _gla_kernel(Lseq, NS, SB, t_lo, t_hi,
                qb_ref, kb_ref, vb_ref, rb_ref, sm_ref, sin_ref, wup_ref, bup_ref, nb_ref, obuf_ref,
                o_ref, so_ref):
    del obuf_ref
    R = NS * Lseq
    heads = range(H_B)

    @pl.when(pl.program_id(1) == 0)
    def _():
        so_ref[...] = sin_ref[...]

    nb = Lseq // SB
    assert nb == 1 or NS == 1
    n_off = SB * nb * (nb - 1) // 2
    sb_shift = SB.bit_length() - 1

    valid = _valid_rows(R, Lseq, t_lo, t_hi)
    r = lax.broadcasted_iota(jnp.int32, (R, R), 0)
    c = lax.broadcasted_iota(jnp.int32, (R, R), 1)
    same = (r | (Lseq - 1)) == (c | (Lseq - 1))
    m_row = jnp.where(same & (c <= r), 1.0, 0.0)
    m_first = jnp.where(same & (c <= (r & ~(SB - 1))), 1.0, 0.0)
    m_last = jnp.where(same, 1.0, 0.0)
    masks = jnp.concatenate([m_row, m_first, m_last], axis=0).astype(BF16)

    pre = _dot3(sm_ref[...], wup_ref[...]) + bup_ref[...]
    lg = jnp.where(valid, -_softplus(-pre) / GATE_TAU, 0.0)
    cums = _mask_dot(masks, lg)
    Bc_all, b0_all, BL_all = cums[:R], cums[R:2 * R], cums[2 * R:]

    re = lax.broadcasted_iota(jnp.int32, (R, n_off + R), 0)
    ce = lax.broadcasted_iota(jnp.int32, (R, n_off + R), 1)
    rblk = (re & (Lseq - 1)) >> sb_shift
    cd = ce - n_off
    see = ((ce >= (SB // 2) * rblk * (rblk - 1)) & (ce < (SB // 2) * rblk * (rblk + 1))) | (
        (cd >= 0) & ((cd >> sb_shift) == (re >> sb_shift)) & (cd <= re))

    def hs(a, h, w):
        return a[:, h * w:(h + 1) * w]

    q = [hs(qb_ref, h, DK_B) * (DK_B ** -0.5) for h in heads]
    k = [jnp.where(valid, hs(kb_ref, h, DK_B), 0.0) for h in heads]
    v = [hs(vb_ref, h, DV_B) for h in heads]
    Bc = [hs(Bc_all, h, DK_B) for h in heads]
    b0 = [hs(b0_all, h, DK_B) for h in heads]
    BL = [hs(BL_all, h, DK_B) for h in heads]
    qs = [q[h] * jnp.exp(Bc[h] - b0[h]) for h in heads]
    kx, vx = [], []
    for h in heads:
        kparts = [k[h][:i * SB] * jnp.exp(Bc[h][i * SB:i * SB + 1] - Bc[h][:i * SB]) for i in range(1, nb)]
        kparts.append(k[h] * jnp.exp(b0[h] - Bc[h]))
        vparts = [v[h][:i * SB] for i in range(1, nb)] + [v[h]]
        kx.append(kparts[0] if nb == 1 else jnp.concatenate(kparts, axis=0))
        vx.append(vparts[0] if nb == 1 else jnp.concatenate(vparts, axis=0))
    Ax = [jnp.where(see, _dot3(qs[h], kx[h], _dot_nt), 0.0) for h in heads]
    o_intra = [_bdot(Ax[h], vx[h]) for h in heads]
    qe = [q[h] * jnp.exp(Bc[h]) for h in heads]
    kd = [k[h] * jnp.exp(BL[h] - Bc[h]) for h in heads]
    ET = [jnp.transpose(_pad_rows(jnp.exp(BL[h]), LANES)) for h in heads]

    items = [(h, s) for h in heads for s in range(NS)]
    S = {it: so_ref[it[1], it[0]] for it in items}
    o_inter = {}
    for h, s in items:
        o_inter[(h, s)] = _bdot(qe[h][s * Lseq:(s + 1) * Lseq], S[(h, s)])
    for h, s in items:
        rows = slice(s * Lseq, (s + 1) * Lseq)
        so_ref[s, h] = S[(h, s)] * ET[h][:, s * Lseq:s * Lseq + 1] + _bdot_tn(kd[h][rows], v[h][rows])
    for h in heads:
        parts = [o_inter[(h, s)] for s in range(NS)]
        o = (parts[0] if NS == 1 else jnp.concatenate(parts, axis=0)) + o_intra[h]
        rh = hs(rb_ref, h, DV_B)
        o_ref[:, h * DV_B:(h + 1) * DV_B] = _rms(o, nb_ref[...]) * _silu(rh)


def _gla(big, small, sin, wup, bup, nb, obuf, *, nsteps, Lseq, NS, SB, nchunks, row_block0, t_lo, t_hi):
    assert NS * Lseq == ROWS
    kern = functools.partial(_gla_kernel, Lseq, NS, SB, t_lo, t_hi)
    r = lambda b, c: row_block0 + b * nchunks + c
    return pl.pallas_call(
        kern,
        grid=(nsteps, nchunks),
        in_specs=[
            pl.BlockSpec((ROWS, H_B * DK_B), lambda b, c: (r(b, c), C_QB // (H_B * DK_B))),
            pl.BlockSpec((ROWS, H_B * DK_B), lambda b, c: (r(b, c), C_KB // (H_B * DK_B))),
            pl.BlockSpec((ROWS, H_B * DV_B), lambda b, c: (r(b, c), C_VB // (H_B * DV_B))),
            pl.BlockSpec((ROWS, H_B * DV_B), lambda b, c: (r(b, c), C_RB // (H_B * DV_B))),
            pl.BlockSpec((ROWS, LANES), lambda b, c: (r(b, c), 0)),
            pl.BlockSpec((NS, H_B, DK_B, DV_B), lambda b, c: (b, 0, 0, 0)),
            pl.BlockSpec((LANES, H_B * DK_B), lambda b, c: (0, 0)),
            pl.BlockSpec((1, H_B * DK_B), lambda b, c: (0, 0)),
            pl.BlockSpec((1, DV_B), lambda b, c: (0, 0)),
            pl.BlockSpec(memory_space=pl.ANY),
        ],
        out_specs=[
            pl.BlockSpec((ROWS, H_B * DV_B), lambda b, c: (r(b, c), 0)),
            pl.BlockSpec((NS, H_B, DK_B, DV_B), lambda b, c: (b, 0, 0, 0)),
        ],
        out_shape=[
            jax.ShapeDtypeStruct(obuf.shape, F32),
            jax.ShapeDtypeStruct((nsteps * NS, H_B, DK_B, DV_B), F32),
        ],
        input_output_aliases={9: 0},
        compiler_params=pltpu.CompilerParams(
            dimension_semantics=("parallel", "arbitrary"), vmem_limit_bytes=VMEM_LIMIT),
        name="gla_L%d" % Lseq,
    )(big, big, big, big, small, sin, wup, bup, nb, obuf)


def _split_cols(w):
    offs, s = [], 0
    for c in COL_SIZES[:-1]:
        s += c
        offs.append(s)
    return jnp.split(w, offs, axis=-1)


def kernel(x_prompt, x_sample, state_delta, state_conv, state_gla, meta_tokens, norm_mix, w_in, conv_w, a_log, dt_bias, norm_a, gla_gate_up, gla_gate_bias, norm_b, w_proj_a, w_proj_b, w_out, norm_ffn, router_group_w, router_group_b, router_expert_w, router_expert_b, expert_w_gate, expert_w_up, expert_w_down, norm_final):
    Bp, Tp, D = x_prompt.shape
    Bs, Ts, _ = x_sample.shape
    depth = w_in.shape[0]
    pad_p = (-N_META) % CHUNK
    Tpp = pad_p + N_META + Tp
    ncp = Tpp // CHUNK
    np_rows = Bp * Tpp
    ns_rows = Bs * SAMPLE_L
    n = np_rows + ns_rows
    seq_per_step = ROWS // SAMPLE_L
    dt = x_prompt.dtype
    xp = jnp.concatenate([
        jnp.zeros((Bp, pad_p, D), dt),
        jnp.broadcast_to(meta_tokens.astype(dt)[None], (Bp, N_META, D)),
        x_prompt], axis=1).reshape(np_rows, D)
    xs = jnp.pad(x_sample, ((0, 0), (0, SAMPLE_L - Ts), (0, 0))).reshape(ns_rows, D)
    x = jnp.concatenate([xp, xs], axis=0).astype(F32)

    tm_big = 1280 if n % 1280 == 0 else 512
    tm = 512

    zeros_conv = jnp.zeros((Bp, SUBLANES, CONV_CH), F32)
    zeros_sa = jnp.zeros((Bp, H_A, DK_A, DV_A), F32)
    zeros_sb = jnp.zeros((Bp, H_B, DK_B, DV_B), F32)
    prompt_args = dict(nsteps=Bp, Lseq=CHUNK, NS=1, nchunks=ncp, row_block0=0, t_lo=pad_p, t_hi=Tpp)
    sample_args = dict(nsteps=Bs // seq_per_step, Lseq=SAMPLE_L, NS=seq_per_step, nchunks=1,
                       row_block0=np_rows // ROWS, t_lo=0, t_hi=Ts)

    sa_p, conv_p, sb_p, sa_s, conv_s, sb_s = [], [], [], [], [], []
    for l in range(depth):
        qkv, z, b_a, a_a, q_b, k_b, v_b, r_b, lr, g_a, g_b = _split_cols(w_in[l])
        wbig = jnp.concatenate([qkv, z, q_b, k_b, v_b, r_b, g_a, g_b], axis=1).astype(BF16)
        wsmall = jnp.concatenate(
            [b_a, a_a, lr, jnp.zeros((D, LANES - 2 * H_A - GATE_RANK), F32)], axis=1).astype(F32)
        big, small = _inproj(x, norm_mix[l][None].astype(F32), wbig, wsmall, tm_big, 512)

        cw = jnp.pad(conv_w[l].astype(F32), ((0, SUBLANES - CONV_W), (0, 0)))
        hp = jnp.zeros((SUBLANES, LANES), F32)
        hp = hp.at[0, S_DECAY:S_DECAY + H_A].set(a_log[l].astype(F32))
        hp = hp.at[1, S_DECAY:S_DECAY + H_A].set(dt_bias[l].astype(F32))
        na = norm_a[l][None].astype(F32)
        wup = jnp.zeros((LANES, H_B * DK_B), F32).at[S_LR:S_LR + GATE_RANK].set(gla_gate_up[l].astype(F32))
        bup = gla_gate_bias[l][None].astype(F32)
        nb = norm_b[l][None].astype(F32)

        cin_s = jnp.pad(state_conv[l].astype(F32), ((0, 0), (SUBLANES - (CONV_W - 1), 0), (0, 0)))
        oa = jnp.zeros((n, H_A * DV_A), F32)
        ob = jnp.zeros((n, H_B * DV_B), F32)
        oa, c_p, s_ap = _gdn(big, small, zeros_conv, zeros_sa, cw, hp, na, oa, **prompt_args)
        oa, c_s, s_as = _gdn(big, small, cin_s, state_delta[l].astype(F32), cw, hp, na, oa, **sample_args)
        ob, s_bp = _gla(big, small, zeros_sb, wup, bup, nb, ob, SB=GLA_SUB, **prompt_args)
        ob, s_bs = _gla(big, small, state_gla[l].astype(F32), wup, bup, nb, ob, SB=SAMPLE_L, **sample_args)
        x = _outproj(oa, ob, big, x, w_proj_a[l].astype(BF16), w_proj_b[l].astype(BF16),
                     w_out[l].astype(BF16), tm)

        wr = jnp.concatenate([router_expert_w[l], router_group_w[l],
                              jnp.zeros((D, LANES - N_EXPERTS - N_GROUPS), F32)], axis=1).astype(F32)
        br = jnp.concatenate([router_expert_b[l], router_group_b[l],
                              jnp.zeros((LANES - N_EXPERTS - N_GROUPS,), F32)])[None].astype(F32)
        wgu = jnp.concatenate([expert_w_gate[l], expert_w_up[l]], axis=-1).astype(BF16)
        wd = expert_w_down[l].astype(BF16)
        x = _moe(x, norm_ffn[l][None].astype(F32), wr, br, wgu, wd,
                 norm_final[None].astype(F32), l == depth - 1, tm, MOE_TILE)

        sa_p.append(s_ap)
        conv_p.append(c_p[:, SUBLANES - (CONV_W - 1):])
        sb_p.append(s_bp)
        sa_s.append(s_as)
        conv_s.append(c_s[:, SUBLANES - (CONV_W - 1):])
        sb_s.append(s_bs)

    y = x
    y_prompt = y[:np_rows].reshape(Bp, Tpp, D)[:, pad_p + N_META:].astype(dt)
    y_sample = y[np_rows:].reshape(Bs, SAMPLE_L, D)[:, :Ts].astype(x_sample.dtype)
    return (y_prompt, y_sample,
            jnp.stack(sa_p).astype(dt), jnp.stack(conv_p).astype(dt), jnp.stack(sb_p).astype(dt),
            jnp.stack(sa_s).astype(state_delta.dtype), jnp.stack(conv_s).astype(state_conv.dtype),
            jnp.stack(sb_s).astype(state_gla.dtype))
```

```python
import functools

import jax
import jax.numpy as jnp
from jax import lax
from jax.experimental import pallas as pl
from jax.experimental.pallas import tpu as pltpu

F32 = jnp.float32
BF16 = jnp.bfloat16

D_MODEL = 1024
N_META = 16
CHUNK = 64
EPS = 1e-6
H_A, DK_A, DV_A = 8, 128, 128
CONV_W = 4
CONV_CH = 2 * H_A * DK_A + H_A * DV_A
H_B, DK_B, DV_B = 4, 128, 256
GATE_RANK = 16
GATE_TAU = 16.0
N_GROUPS = 4
EXPERTS_PER_GROUP = 8
N_EXPERTS = N_GROUPS * EXPERTS_PER_GROUP
D_FF_E = 256

LANES = 128
SUBLANES = 8
VMEM_LIMIT = 48 * 1024 * 1024

C_QKV = 0
C_Z = C_QKV + CONV_CH
C_QB = C_Z + H_A * DV_A
C_KB = C_QB + H_B * DK_B
C_VB = C_KB + H_B * DK_B
C_RB = C_VB + H_B * DV_B
C_GA = C_RB + H_B * DV_B
C_GB = C_GA + D_MODEL
C_BIG = C_GB + D_MODEL
S_BETA = 0
S_DECAY = H_A
S_LR = 2 * H_A

SAMPLE_L = SUBLANES
ROWS = CHUNK
GLA_SUB = 16
MOE_TILE = 256


def _dot(a, b, prec=None):
    return jnp.dot(a, b, preferred_element_type=F32, precision=prec)


def _dot_nt(a, b, prec=None):
    return lax.dot_general(a, b, (((1,), (1,)), ((), ())), preferred_element_type=F32, precision=prec)


def _dot_tn(a, b, prec=None):
    return lax.dot_general(a, b, (((0,), (0,)), ((), ())), preferred_element_type=F32, precision=prec)


def _bdot(a, b):
    return _dot(a.astype(BF16), b.astype(BF16))


def _bdot_nt(a, b):
    return _dot_nt(a.astype(BF16), b.astype(BF16))


def _bdot_tn(a, b):
    return _dot_tn(a.astype(BF16), b.astype(BF16))


def _split2(a):
    hi = a.astype(BF16)
    return hi, (a - hi.astype(F32)).astype(BF16)


def _dot3(a, b, dot=_dot):
    ah, al = _split2(a)
    bh, bl = _split2(b)
    return dot(ah, bh) + dot(al, bh) + dot(ah, bl)


def _mask_dot(m, b):
    hi = b.astype(BF16)
    r = b - hi.astype(F32)
    mid = r.astype(BF16)
    lo = (r - mid.astype(F32)).astype(BF16)
    return _dot(m, hi) + _dot(m, mid) + _dot(m, lo)


def _sigmoid(x):
    return 1.0 / (1.0 + jnp.exp(-x))


def _silu(x):
    return x * _sigmoid(x)


def _softplus(x):
    return jnp.maximum(x, 0.0) + jnp.log(1.0 + jnp.exp(-jnp.abs(x)))


def _rms(x, w):
    return x * lax.rsqrt(jnp.mean(x * x, axis=-1, keepdims=True) + EPS) * w


def _inproj_kernel(x_ref, nw_ref, wb_ref, ws_ref, big_ref, small_ref, xn_ref):
    @pl.when(pl.program_id(1) == 0)
    def _():
        xn = _rms(x_ref[...], nw_ref[...])
        xn_ref[...] = xn.astype(BF16)
        small_ref[...] = _dot3(xn, ws_ref[...])

    big_ref[...] = _dot(xn_ref[...], wb_ref[...])


def _inproj(x, nw, wbig, wsmall, tm, tn):
    n = x.shape[0]
    return pl.pallas_call(
        _inproj_kernel,
        grid=(n // tm, C_BIG // tn),
        in_specs=[
            pl.BlockSpec((tm, D_MODEL), lambda i, j: (i, 0)),
            pl.BlockSpec((1, D_MODEL), lambda i, j: (0, 0)),
            pl.BlockSpec((D_MODEL, tn), lambda i, j: (0, j)),
            pl.BlockSpec((D_MODEL, LANES), lambda i, j: (0, 0)),
        ],
        out_specs=[
            pl.BlockSpec((tm, tn), lambda i, j: (i, j)),
            pl.BlockSpec((tm, LANES), lambda i, j: (i, 0)),
        ],
        out_shape=[
            jax.ShapeDtypeStruct((n, C_BIG), F32),
            jax.ShapeDtypeStruct((n, LANES), F32),
        ],
        scratch_shapes=[pltpu.VMEM((tm, D_MODEL), BF16)],
        compiler_params=pltpu.CompilerParams(
            dimension_semantics=("parallel", "arbitrary"), vmem_limit_bytes=VMEM_LIMIT),
        name="inproj",
    )(x, nw, wbig, wsmall)


def _outproj_kernel(oa_ref, ob_ref, ga_ref, gb_ref, x_ref, wa_ref, wb_ref, wo_ref, out_ref):
    ya = _dot(oa_ref[...].astype(BF16), wa_ref[...])
    yb = _dot(ob_ref[...].astype(BF16), wb_ref[...])
    merged = _sigmoid(ga_ref[...]) * ya + _sigmoid(gb_ref[...]) * yb
    out_ref[...] = x_ref[...] + _dot(merged.astype(BF16), wo_ref[...])


def _outproj(oa, ob, big, x, wa, wb, wo, tm):
    n = x.shape[0]
    row = lambda i: (i, 0)
    full = lambda i: (0, 0)
    return pl.pallas_call(
        _outproj_kernel,
        grid=(n // tm,),
        in_specs=[
            pl.BlockSpec((tm, D_MODEL), row),
            pl.BlockSpec((tm, D_MODEL), row),
            pl.BlockSpec((tm, D_MODEL), lambda i: (i, C_GA // D_MODEL)),
            pl.BlockSpec((tm, D_MODEL), lambda i: (i, C_GB // D_MODEL)),
            pl.BlockSpec((tm, D_MODEL), row),
            pl.BlockSpec((D_MODEL, D_MODEL), full),
            pl.BlockSpec((D_MODEL, D_MODEL), full),
            pl.BlockSpec((D_MODEL, D_MODEL), full),
        ],
        out_specs=pl.BlockSpec((tm, D_MODEL), row),
        out_shape=jax.ShapeDtypeStruct((n, D_MODEL), F32),
        compiler_params=pltpu.CompilerParams(
            dimension_semantics=("parallel",), vmem_limit_bytes=VMEM_LIMIT),
        name="outproj",
    )(oa, ob, big, big, x, wa, wb, wo)


def _route(logits):
    lane = lax.broadcasted_iota(jnp.int32, logits.shape, 1)
    neg = jnp.float32(-jnp.inf)
    big = jnp.int32(1 << 20)
    gmask = (lane >= N_EXPERTS) & (lane < N_EXPERTS + N_GROUPS)
    gl = jnp.where(gmask, logits, neg)
    gmax = jnp.max(gl, axis=-1, keepdims=True)
    gsel = jnp.min(jnp.where(gl == gmax, lane - N_EXPERTS, big), axis=-1, keepdims=True)
    pgsel = 1.0 / jnp.sum(jnp.where(gmask, jnp.exp(gl - gmax), 0.0), axis=-1, keepdims=True)
    lo = gsel * EXPERTS_PER_GROUP
    emask = (lane >= lo) & (lane < lo + EXPERTS_PER_GROUP)
    el = jnp.where(emask, logits, neg)
    m1 = jnp.max(el, axis=-1, keepdims=True)
    i1 = jnp.min(jnp.where(el == m1, lane, big), axis=-1, keepdims=True)
    el2 = jnp.where(lane == i1, neg, el)
    m2 = jnp.max(el2, axis=-1, keepdims=True)
    i2 = jnp.min(jnp.where(el2 == m2, lane, big), axis=-1, keepdims=True)
    r = jnp.exp(m2 - m1)
    w1 = pgsel / (1.0 + r)
    w2 = pgsel * r / (1.0 + r)
    return i1, i2, w1, w2


def _router_kernel(x_ref, nw_ref, wr_ref, br_ref, xn_ref, idx_ref, wt_ref, cnt_ref):
    tm = x_ref.shape[0]
    xn = _rms(x_ref[...], nw_ref[...])
    xn_ref[...] = xn
    i1, i2, w1, w2 = _route(_dot3(xn, wr_ref[...]) + br_ref[...])
    lane = lax.broadcasted_iota(jnp.int32, (tm, LANES), 1)
    onehot = jnp.where((lane == i1) | (lane == i2), 1.0, 0.0)
    r = lax.broadcasted_iota(jnp.int32, (tm, tm), 0)
    c = lax.broadcasted_iota(jnp.int32, (tm, tm), 1)
    earlier = jnp.where(c < r, 1.0, 0.0).astype(BF16)
    cum = _dot(earlier, onehot.astype(BF16))
    rank1 = jnp.sum(jnp.where(lane == i1, cum, 0.0), axis=-1, keepdims=True).astype(jnp.int32)
    rank2 = jnp.sum(jnp.where(lane == i2, cum, 0.0), axis=-1, keepdims=True).astype(jnp.int32)
    idx_ref[...] = jnp.where(lane == 0, i1, jnp.where(lane == 1, i2, jnp.where(
        lane == 2, rank1, jnp.where(lane == 3, rank2, 0))))
    wt_ref[...] = jnp.where(lane == 0, w1, jnp.where(lane == 1, w2, 0.0))
    counts = jnp.sum(onehot, axis=0, keepdims=True).astype(jnp.int32)
    cnt_ref[...] = jnp.broadcast_to(counts, cnt_ref.shape)


def _router(x, nw, wr, br, tm):
    n = x.shape[0]
    row = lambda i: (i, 0)
    full = lambda i: (0, 0)
    return pl.pallas_call(
        _router_kernel,
        grid=(n // tm,),
        in_specs=[
            pl.BlockSpec((tm, D_MODEL), row),
            pl.BlockSpec((1, D_MODEL), full),
            pl.BlockSpec((D_MODEL, LANES), full),
            pl.BlockSpec((1, LANES), full),
        ],
        out_specs=[
            pl.BlockSpec((tm, D_MODEL), row),
            pl.BlockSpec((tm, LANES), row),
            pl.BlockSpec((tm, LANES), row),
            pl.BlockSpec((None, SUBLANES, LANES), lambda i: (i, 0, 0)),
        ],
        out_shape=[
            jax.ShapeDtypeStruct((n, D_MODEL), F32),
            jax.ShapeDtypeStruct((n, LANES), jnp.int32),
            jax.ShapeDtypeStruct((n, LANES), F32),
            jax.ShapeDtypeStruct((n // tm, SUBLANES, LANES), jnp.int32),
        ],
        compiler_params=pltpu.CompilerParams(
            dimension_semantics=("parallel",), vmem_limit_bytes=VMEM_LIMIT),
        name="router",
    )(x, nw, wr, br)


def _moe_plan(idx, cnt, tm_tok, tm_exp):
    n = idx.shape[0]
    nt = 2 * n // tm_exp + N_EXPERTS
    counts = cnt[:, 0, :N_EXPERTS]
    cpad = (jnp.sum(counts, axis=0) + tm_exp - 1) // tm_exp * tm_exp
    ends = jnp.cumsum(cpad)
    base = (ends - cpad)[None] + jnp.cumsum(counts, axis=0) - counts
    base_tok = jnp.repeat(base, tm_tok, axis=0)[:, None, :]
    hit = idx[:, :2, None] == jnp.arange(N_EXPERTS, dtype=jnp.int32)
    dest = (jnp.sum(jnp.where(hit, base_tok, 0), axis=-1) + idx[:, 2:4]).astype(jnp.int32)
    tile_start = jnp.arange(nt, dtype=jnp.int32) * tm_exp
    tile_e = jnp.minimum(jnp.sum(tile_start[:, None] >= ends[None, :], axis=1), N_EXPERTS - 1).astype(jnp.int32)
    tile_valid = (tile_start < ends[-1]).astype(jnp.int32)
    return dest, tile_e, tile_valid


def _slot_major(dest, tm):
    nt = dest.shape[0] // tm
    return jnp.swapaxes(dest.reshape(nt, tm, 2), 1, 2).reshape(nt, 1, 2 * tm)


def _dispatch_kernel(d_ref, xn_ref, xs_in_hbm, xs_hbm, sem):
    del xs_in_hbm
    tm = xn_ref.shape[0]

    def copy(i, j, dst_row):
        return pltpu.make_async_copy(xn_ref.at[pl.ds(i, 1), :], xs_hbm.at[pl.ds(dst_row, 1), :], sem.at[j])

    def start(i, carry):
        for j in range(2):
            copy(i, j, d_ref[0, j * tm + i]).start(priority=j)
        return carry

    def wait(i, carry):
        for j in range(2):
            copy(i, j, 0).wait()
        return carry

    lax.fori_loop(0, tm, start, 0, unroll=4)
    lax.fori_loop(0, tm, wait, 0, unroll=4)


def _dispatch(xn, dest, xs_buf, tm):
    n = xn.shape[0]
    nrows = xs_buf.shape[0]
    return pl.pallas_call(
        _dispatch_kernel,
        grid=(n // tm,),
        in_specs=[
            pl.BlockSpec((None, 1, 2 * tm), lambda t: (t, 0, 0), memory_space=pltpu.SMEM),
            pl.BlockSpec((tm, D_MODEL), lambda t: (t, 0)),
            pl.BlockSpec(memory_space=pl.ANY),
        ],
        out_specs=pl.BlockSpec(memory_space=pl.ANY),
        out_shape=jax.ShapeDtypeStruct((nrows, D_MODEL), F32),
        scratch_shapes=[pltpu.SemaphoreType.DMA((2,))],
        input_output_aliases={2: 0},
        compiler_params=pltpu.CompilerParams(
            dimension_semantics=("arbitrary",), vmem_limit_bytes=VMEM_LIMIT),
        name="dispatch",
    )(_slot_major(dest, tm), xn, xs_buf)


def _row_gather(idx_ref, src_hbm, buf, sem, slot, nrows):
    def body(r, carry):
        for j in range(2):
            rr = 2 * r + j
            pltpu.make_async_copy(src_hbm.at[pl.ds(idx_ref[0, rr], 1), :],
                                  buf.at[slot, pl.ds(rr, 1), :], sem.at[slot]).start(priority=j)
        return carry
    lax.fori_loop(0, nrows // 2, body, 0, unroll=4)


def _row_gather_wait(src_hbm, buf, sem, slot, nrows):
    def body(r, carry):
        pltpu.make_async_copy(src_hbm.at[pl.ds(0, 1), :],
                              buf.at[slot, pl.ds(r, 1), :], sem.at[slot]).wait()
        return carry
    lax.fori_loop(0, nrows, body, 0, unroll=8)


def _expert_kernel(te_ref, tv_ref, xs_ref, wgu_ref, wd_ref, y_ref):
    t = pl.program_id(0)

    @pl.when(tv_ref[t] == 1)
    def _():
        h = _dot(xs_ref[...].astype(BF16), wgu_ref[...])
        a = _silu(h[:, :D_FF_E]) * h[:, D_FF_E:]
        y_ref[...] = _dot(a.astype(BF16), wd_ref[...])

    @pl.when(tv_ref[t] == 0)
    def _():
        y_ref[...] = jnp.zeros_like(y_ref)


def _experts(xs, tile_e, tile_valid, wgu, wd, tm):
    nt = xs.shape[0] // tm
    return pl.pallas_call(
        _expert_kernel,
        grid_spec=pltpu.PrefetchScalarGridSpec(
            num_scalar_prefetch=2,
            grid=(nt,),
            in_specs=[
                pl.BlockSpec((tm, D_MODEL), lambda t, te, tv: (t, 0)),
                pl.BlockSpec((None, D_MODEL, 2 * D_FF_E), lambda t, te, tv: (te[t], 0, 0)),
                pl.BlockSpec((None, D_FF_E, D_MODEL), lambda t, te, tv: (te[t], 0, 0)),
            ],
            out_specs=pl.BlockSpec((tm, D_MODEL), lambda t, te, tv: (t, 0)),
        ),
        out_shape=jax.ShapeDtypeStruct((nt * tm, D_MODEL), F32),
        compiler_params=pltpu.CompilerParams(
            dimension_semantics=("arbitrary",), vmem_limit_bytes=VMEM_LIMIT),
        name="experts",
    )(tile_e, tile_valid, xs, wgu, wd)


def _combine_kernel(final, d_ref, d_next_ref, y_hbm, x_ref, wt_ref, fnw_ref, out_ref, ybuf, sem):
    t = pl.program_id(0)
    nt = pl.num_programs(0)
    tm = x_ref.shape[0]
    slot = t & 1

    @pl.when(t == 0)
    def _():
        _row_gather(d_ref, y_hbm, ybuf, sem, 0, 2 * tm)

    @pl.when(t + 1 < nt)
    def _():
        _row_gather(d_next_ref, y_hbm, ybuf, sem, 1 - slot, 2 * tm)

    _row_gather_wait(y_hbm, ybuf, sem, slot, 2 * tm)
    wt = wt_ref[...]
    out = x_ref[...] + wt[:, 0:1] * ybuf[slot, 0:tm, :] + wt[:, 1:2] * ybuf[slot, tm:2 * tm, :]
    out_ref[...] = _rms(out, fnw_ref[...]) if final else out


def _combine(x, y, dest, wt, fnw, final, tm):
    n = x.shape[0]
    nt = n // tm
    d = _slot_major(dest, tm)
    return pl.pallas_call(
        functools.partial(_combine_kernel, final),
        grid=(nt,),
        in_specs=[
            pl.BlockSpec((None, 1, 2 * tm), lambda t: (t, 0, 0), memory_space=pltpu.SMEM),
            pl.BlockSpec((None, 1, 2 * tm), lambda t: (jnp.minimum(t + 1, nt - 1), 0, 0),
                         memory_space=pltpu.SMEM),
            pl.BlockSpec(memory_space=pl.ANY),
            pl.BlockSpec((tm, D_MODEL), lambda t: (t, 0)),
            pl.BlockSpec((tm, LANES), lambda t: (t, 0)),
            pl.BlockSpec((1, D_MODEL), lambda t: (0, 0)),
        ],
        out_specs=pl.BlockSpec((tm, D_MODEL), lambda t: (t, 0)),
        out_shape=jax.ShapeDtypeStruct((n, D_MODEL), F32),
        scratch_shapes=[pltpu.VMEM((2, 2 * tm, D_MODEL), F32), pltpu.SemaphoreType.DMA((2,))],
        compiler_params=pltpu.CompilerParams(
            dimension_semantics=("arbitrary",), vmem_limit_bytes=VMEM_LIMIT),
        name="combine",
    )(d, d, y, x, wt, fnw)


def _moe_rows(n, tm_exp):
    return (2 * n // tm_exp + N_EXPERTS) * tm_exp


def _moe(x, nw, wr, br, wgu, wd, fnw, xs_buf, final, tm_tok, tm_exp):
    xn, idx, wt, cnt = _router(x, nw, wr, br, tm_tok)
    dest, tile_e, tile_valid = _moe_plan(idx, cnt, tm_tok, tm_exp)
    xs = _dispatch(xn, dest, xs_buf, tm_tok)
    y = _experts(xs, tile_e, tile_valid, wgu, wd, tm_exp)
    return _combine(x, y, dest, wt, fnw, final, tm_exp), xs


def _seq_masks(n, lseq):
    r = lax.broadcasted_iota(jnp.int32, (n, n), 0)
    c = lax.broadcasted_iota(jnp.int32, (n, n), 1)
    sh = lseq.bit_length() - 1
    same = (r >> sh) == (c >> sh)
    return same & (r >= c), same & (r > c), r == c


def _valid_rows(n, lseq, t_lo, t_hi):
    row = lax.broadcasted_iota(jnp.int32, (n, 1), 0)
    t = (row & (lseq - 1)) + pl.program_id(1) * lseq
    return (t >= t_lo) & (t < t_hi)


def _pad_rows(a, n):
    return jnp.concatenate([a, jnp.zeros((n - a.shape[0], a.shape[1]), a.dtype)], axis=0)


def _gdn_kernel(Lseq, NS, t_lo, t_hi, conv_row0,
                qkv_ref, z_ref, sm_ref, cin_ref, sin_ref, cw_ref, hp_ref, na_ref, obuf_ref,
                o_ref, cout_ref, so_ref, ext_ref):
    del obuf_ref
    R = NS * Lseq
    R2 = 2 * R
    npairs = H_A // 2

    @pl.when(pl.program_id(1) == 0)
    def _():
        ext_ref[:, 0:SUBLANES, :] = cin_ref[...]
        so_ref[...] = sin_ref[...]

    valid = _valid_rows(R, Lseq, t_lo, t_hi)
    valid_l = _valid_rows(Lseq, Lseq, t_lo, t_hi)
    for s in range(NS):
        ext_ref[s, SUBLANES:SUBLANES + Lseq, :] = jnp.where(
            valid_l, qkv_ref[s * Lseq:(s + 1) * Lseq, :], 0.0)

    def conv_act(col0):
        pieces = []
        for s in range(NS):
            acc = None
            for i in range(CONV_W):
                r0 = SUBLANES - (CONV_W - 1) + i
                term = cw_ref[i:i + 1, col0:col0 + LANES] * ext_ref[s, r0:r0 + Lseq, col0:col0 + LANES]
                acc = term if acc is None else acc + term
            pieces.append(acc)
        return _silu(pieces[0] if NS == 1 else jnp.concatenate(pieces, axis=0))

    incl, _, _ = _seq_masks(R, Lseq)
    incl2, strict2, diag2 = _seq_masks(R2, Lseq)
    tri = jnp.where(incl, 1.0, 0.0).astype(BF16)
    eye2 = jnp.where(diag2, 1.0, 0.0)

    sm = sm_ref[...]
    beta_all = jnp.where(valid, _sigmoid(sm), 0.0)
    g_all = jnp.where(valid, -jnp.exp(hp_ref[0:1, :]) * _softplus(sm + hp_ref[1:2, :]), 0.0)
    G_all = _mask_dot(tri, g_all)
    GT = jnp.transpose(_pad_rows(G_all, LANES))

    q2, k2, v2, beta2, Gc2, Gr2 = [], [], [], [], [], []
    for p in range(npairs):
        qs, ks, vs = [], [], []
        for h in (2 * p, 2 * p + 1):
            q = conv_act(h * DK_A)
            k = conv_act(H_A * DK_A + h * DK_A)
            qs.append(q * lax.rsqrt(jnp.sum(q * q, axis=-1, keepdims=True) + EPS) * (DK_A ** -0.5))
            ks.append(jnp.where(valid, k * lax.rsqrt(jnp.sum(k * k, axis=-1, keepdims=True) + EPS), 0.0))
            vs.append(conv_act(2 * H_A * DK_A + h * DV_A))
        q2.append(jnp.concatenate(qs, axis=0))
        k2.append(jnp.concatenate(ks, axis=0))
        v2.append(jnp.concatenate(vs, axis=0))
        h0 = 2 * p
        beta2.append(jnp.concatenate([beta_all[:, S_BETA + h0:S_BETA + h0 + 1],
                                      beta_all[:, S_BETA + h0 + 1:S_BETA + h0 + 2]], axis=0))
        Gc2.append(jnp.concatenate([G_all[:, S_DECAY + h0:S_DECAY + h0 + 1],
                                    G_all[:, S_DECAY + h0 + 1:S_DECAY + h0 + 2]], axis=0))
        Gr2.append(jnp.concatenate([GT[S_DECAY + h0:S_DECAY + h0 + 1, :R],
                                    GT[S_DECAY + h0 + 1:S_DECAY + h0 + 2, :R]], axis=1))

    rng = range(npairs)
    kk = [_bdot_nt(k2[p], k2[p]) for p in rng]
    qk = [_bdot_nt(q2[p], k2[p]) for p in rng]
    dec = [jnp.where(incl2, jnp.exp(jnp.where(incl2, Gc2[p] - Gr2[p], 0.0)), 0.0) for p in rng]
    Bp = [-jnp.where(strict2, beta2[p] * kk[p] * dec[p], 0.0) for p in rng]
    P = [eye2 + Bp[p] for p in rng]
    n = 2
    while n < Lseq:
        Bp = [_dot3(b, b) for b in Bp]
        P = [pp + _dot3(pp, b) for pp, b in zip(P, Bp)]
        n *= 2
    eG = [jnp.exp(Gc2[p]) for p in rng]
    rhs = [jnp.concatenate([beta2[p] * v2[p], (beta2[p] * eG[p]) * k2[p]], axis=1) for p in rng]
    sol = [_dot3(P[p], rhs[p]) for p in rng]
    qk = [qk[p] * dec[p] for p in rng]
    qe = [q2[p] * eG[p] for p in rng]

    items = [(p, j, s) for p in rng for j in range(2) for s in range(NS)]

    def rows(j, s):
        return slice(j * R + s * Lseq, j * R + (s + 1) * Lseq)

    S = {it: so_ref[it[2], 2 * it[0] + it[1]] for it in items}
    both = {}
    for p, j, s in items:
        r = rows(j, s)
        both[(p, j, s)] = _bdot(jnp.concatenate([sol[p][r, DV_A:], qe[p][r]], axis=0), S[(p, j, s)])
    vnew = {}
    for p, j, s in items:
        vnew[(p, j, s)] = sol[p][rows(j, s), :DV_A] - both[(p, j, s)][:Lseq]
    for p, j, s in items:
        r = rows(j, s)
        last = j * R + (s + 1) * Lseq - 1
        GL = Gc2[p][last:last + 1]
        kd = k2[p][r] * jnp.exp(GL - Gc2[p][r])
        so_ref[s, 2 * p + j] = S[(p, j, s)] * jnp.exp(GL) + _bdot_tn(kd, vnew[(p, j, s)])
    for p in rng:
        order = [(p, j, s) for j in range(2) for s in range(NS)]
        vnew2 = jnp.concatenate([vnew[it] for it in order], axis=0)
        o2 = jnp.concatenate([both[it][Lseq:] for it in order], axis=0) + _bdot(qk[p], vnew2)
        for j in range(2):
            h = 2 * p + j
            zh = z_ref[:, h * DV_A:(h + 1) * DV_A]
            o_ref[:, h * DV_A:(h + 1) * DV_A] = _rms(o2[j * R:(j + 1) * R], na_ref[...]) * _silu(zh)

    for s in range(NS):
        cout_ref[s] = ext_ref[s, conv_row0:conv_row0 + SUBLANES, :]
        ext_ref[s, 0:SUBLANES, :] = ext_ref[s, Lseq:Lseq + SUBLANES, :]


def _gdn(big, small, cin, sin, cw, hp, na, obuf, *, nsteps, Lseq, NS, nchunks, row_block0, t_lo, t_hi):
    assert NS * Lseq == ROWS
    conv_row0 = t_hi - (nchunks - 1) * Lseq
    kern = functools.partial(_gdn_kernel, Lseq, NS, t_lo, t_hi, conv_row0)
    rb = lambda b, c: row_block0 + b * nchunks + c
    return pl.pallas_call(
        kern,
        grid=(nsteps, nchunks),
        in_specs=[
            pl.BlockSpec((ROWS, CONV_CH), lambda b, c: (rb(b, c), 0)),
            pl.BlockSpec((ROWS, H_A * DV_A), lambda b, c: (rb(b, c), C_Z // (H_A * DV_A))),
            pl.BlockSpec((ROWS, LANES), lambda b, c: (rb(b, c), 0)),
            pl.BlockSpec((NS, SUBLANES, CONV_CH), lambda b, c: (b, 0, 0)),
            pl.BlockSpec((NS, H_A, DK_A, DV_A), lambda b, c: (b, 0, 0, 0)),
            pl.BlockSpec((SUBLANES, CONV_CH), lambda b, c: (0, 0)),
            pl.BlockSpec((SUBLANES, LANES), lambda b, c: (0, 0)),
            pl.BlockSpec((1, DV_A), lambda b, c: (0, 0)),
            pl.BlockSpec(memory_space=pl.ANY),
        ],
        out_specs=[
            pl.BlockSpec((ROWS, H_A * DV_A), lambda b, c: (rb(b, c), 0)),
            pl.BlockSpec((NS, SUBLANES, CONV_CH), lambda b, c: (b, 0, 0)),
            pl.BlockSpec((NS, H_A, DK_A, DV_A), lambda b, c: (b, 0, 0, 0)),
        ],
        out_shape=[
            jax.ShapeDtypeStruct(obuf.shape, F32),
            jax.ShapeDtypeStruct((nsteps * NS, SUBLANES, CONV_CH), F32),
            jax.ShapeDtypeStruct((nsteps * NS, H_A, DK_A, DV_A), F32),
        ],
        scratch_shapes=[pltpu.VMEM((NS, SUBLANES + Lseq, CONV_CH), F32)],
        input_output_aliases={8: 0},
        compiler_params=pltpu.CompilerParams(
            dimension_semantics=("parallel", "arbitrary"), vmem_limit_bytes=VMEM_LIMIT),
        name="gdn_L%d" % Lseq,
    )(big, big, small, cin, sin, cw, hp, na, obuf)


def _gla_kernel(Lseq, NS, SB, t_lo, t_hi,
                qb_ref, kb_ref, vb_ref, rb_ref, sm_ref, sin_ref, wup_ref, bup_ref, nb_ref, obuf_ref,
                o_ref, so_ref):
    del obuf_ref
    R = NS * Lseq
    heads = range(H_B)

    @pl.when(pl.program_id(1) == 0)
    def _():
        so_ref[...] = sin_ref[...]

    nb = Lseq // SB
    assert nb == 1 or NS == 1
    n_off = SB * nb * (nb - 1) // 2
    sb_shift = SB.bit_length() - 1

    valid = _valid_rows(R, Lseq, t_lo, t_hi)
    r = lax.broadcasted_iota(jnp.int32, (R, R), 0)
    c = lax.broadcasted_iota(jnp.int32, (R, R), 1)
    same = (r | (Lseq - 1)) == (c | (Lseq - 1))
    m_row = jnp.where(same & (c <= r), 1.0, 0.0)
    m_first = jnp.where(same & (c <= (r & ~(SB - 1))), 1.0, 0.0)
    m_last = jnp.where(same, 1.0, 0.0)
    masks = jnp.concatenate([m_row, m_first, m_last], axis=0).astype(BF16)

    pre = _dot3(sm_ref[...], wup_ref[...]) + bup_ref[...]
    lg = jnp.where(valid, -_softplus(-pre) / GATE_TAU, 0.0)
    cums = _mask_dot(masks, lg)
    Bc_all, b0_all, BL_all = cums[:R], cums[R:2 * R], cums[2 * R:]

    re = lax.broadcasted_iota(jnp.int32, (R, n_off + R), 0)
    ce = lax.broadcasted_iota(jnp.int32, (R, n_off + R), 1)
    rblk = (re & (Lseq - 1)) >> sb_shift
    cd = ce - n_off
    see = ((ce >= (SB // 2) * rblk * (rblk - 1)) & (ce < (SB // 2) * rblk * (rblk + 1))) | (
        (cd >= 0) & ((cd >> sb_shift) == (re >> sb_shift)) & (cd <= re))

    def hs(a, h, w):
        return a[:, h * w:(h + 1) * w]

    q = [hs(qb_ref, h, DK_B) * (DK_B ** -0.5) for h in heads]
    k = [jnp.where(valid, hs(kb_ref, h, DK_B), 0.0) for h in heads]
    v = [hs(vb_ref, h, DV_B) for h in heads]
    Bc = [hs(Bc_all, h, DK_B) for h in heads]
    b0 = [hs(b0_all, h, DK_B) for h in heads]
    BL = [hs(BL_all, h, DK_B) for h in heads]
    qs = [q[h] * jnp.exp(Bc[h] - b0[h]) for h in heads]
    kx, vx = [], []
    for h in heads:
        kparts = [k[h][:i * SB] * jnp.exp(Bc[h][i * SB:i * SB + 1] - Bc[h][:i * SB]) for i in range(1, nb)]
        kparts.append(k[h] * jnp.exp(b0[h] - Bc[h]))
        vparts = [v[h][:i * SB] for i in range(1, nb)] + [v[h]]
        kx.append(kparts[0] if nb == 1 else jnp.concatenate(kparts, axis=0))
        vx.append(vparts[0] if nb == 1 else jnp.concatenate(vparts, axis=0))
    Ax = [jnp.where(see, _dot3(qs[h], kx[h], _dot_nt), 0.0) for h in heads]
    o_intra = [_bdot(Ax[h], vx[h]) for h in heads]
    qe = [q[h] * jnp.exp(Bc[h]) for h in heads]
    kd = [k[h] * jnp.exp(BL[h] - Bc[h]) for h in heads]
    ET = [jnp.transpose(_pad_rows(jnp.exp(BL[h]), LANES)) for h in heads]

    items = [(h, s) for h in heads for s in range(NS)]
    S = {it: so_ref[it[1], it[0]] for it in items}
    o_inter = {}
    for h, s in items:
        o_inter[(h, s)] = _bdot(qe[h][s * Lseq:(s + 1) * Lseq], S[(h, s)])
    for h, s in items:
        rows = slice(s * Lseq, (s + 1) * Lseq)
        so_ref[s, h] = S[(h, s)] * ET[h][:, s * Lseq:s * Lseq + 1] + _bdot_tn(kd[h][rows], v[h][rows])
    for h in heads:
        parts = [o_inter[(h, s)] for s in range(NS)]
        o = (parts[0] if NS == 1 else jnp.concatenate(parts, axis=0)) + o_intra[h]
        rh = hs(rb_ref, h, DV_B)
        o_ref[:, h * DV_B:(h + 1) * DV_B] = _rms(o, nb_ref[...]) * _silu(rh)


def _gla(big, small, sin, wup, bup, nb, obuf, *, nsteps, Lseq, NS, SB, nchunks, row_block0, t_lo, t_hi):
    assert NS * Lseq == ROWS
    kern = functools.partial(_gla_kernel, Lseq, NS, SB, t_lo, t_hi)
    r = lambda b, c: row_block0 + b * nchunks + c
    return pl.pallas_call(
        kern,
        grid=(nsteps, nchunks),
        in_specs=[
            pl.BlockSpec((ROWS, H_B * DK_B), lambda b, c: (r(b, c), C_QB // (H_B * DK_B))),
            pl.BlockSpec((ROWS, H_B * DK_B), lambda b, c: (r(b, c), C_KB // (H_B * DK_B))),
            pl.BlockSpec((ROWS, H_B * DV_B), lambda b, c: (r(b, c), C_VB // (H_B * DV_B))),
            pl.BlockSpec((ROWS, H_B * DV_B), lambda b, c: (r(b, c), C_RB // (H_B * DV_B))),
            pl.BlockSpec((ROWS, LANES), lambda b, c: (r(b, c), 0)),
            pl.BlockSpec((NS, H_B, DK_B, DV_B), lambda b, c: (b, 0, 0, 0)),
            pl.BlockSpec((LANES, H_B * DK_B), lambda b, c: (0, 0)),
            pl.BlockSpec((1, H_B * DK_B), lambda b, c: (0, 0)),
            pl.BlockSpec((1, DV_B), lambda b, c: (0, 0)),
            pl.BlockSpec(memory_space=pl.ANY),
        ],
        out_specs=[
            pl.BlockSpec((ROWS, H_B * DV_B), lambda b, c: (r(b, c), 0)),
            pl.BlockSpec((NS, H_B, DK_B, DV_B), lambda b, c: (b, 0, 0, 0)),
        ],
        out_shape=[
            jax.ShapeDtypeStruct(obuf.shape, F32),
            jax.ShapeDtypeStruct((nsteps * NS, H_B, DK_B, DV_B), F32),
        ],
        input_output_aliases={9: 0},
        compiler_params=pltpu.CompilerParams(
            dimension_semantics=("parallel", "arbitrary"), vmem_limit_bytes=VMEM_LIMIT),
        name="gla_L%d" % Lseq,
    )(big, big, big, big, small, sin, wup, bup, nb, obuf)


def kernel(x_prompt, x_sample, state_delta, state_conv, state_gla, meta_tokens, norm_mix, w_in, conv_w, a_log, dt_bias, norm_a, gla_gate_up, gla_gate_bias, norm_b, w_proj_a, w_proj_b, w_out, norm_ffn, router_group_w, router_group_b, router_expert_w, router_expert_b, expert_w_gate, expert_w_up, expert_w_down, norm_final):
    Bp, Tp, D = x_prompt.shape
    Bs, Ts, _ = x_sample.shape
    depth = w_in.shape[0]
    pad_p = (-N_META) % CHUNK
    Tpp = pad_p + N_META + Tp
    ncp = Tpp // CHUNK
    np_rows = Bp * Tpp
    ns_rows = Bs * SAMPLE_L
    n = np_rows + ns_rows
    seq_per_step = ROWS // SAMPLE_L
    dt = x_prompt.dtype
    xp = jnp.concatenate([
        jnp.zeros((Bp, pad_p, D), dt),
        jnp.broadcast_to(meta_tokens.astype(dt)[None], (Bp, N_META, D)),
        x_prompt], axis=1).reshape(np_rows, D)
    xs = jnp.pad(x_sample, ((0, 0), (0, SAMPLE_L - Ts), (0, 0))).reshape(ns_rows, D)
    x = jnp.concatenate([xp, xs], axis=0).astype(F32)

    tm_big = 1280 if n % 1280 == 0 else 512
    tm = 512

    zeros_conv = jnp.zeros((Bp, SUBLANES, CONV_CH), F32)
    zeros_sa = jnp.zeros((Bp, H_A, DK_A, DV_A), F32)
    zeros_sb = jnp.zeros((Bp, H_B, DK_B, DV_B), F32)
    prompt_args = dict(nsteps=Bp, Lseq=CHUNK, NS=1, nchunks=ncp, row_block0=0, t_lo=pad_p, t_hi=Tpp)
    sample_args = dict(nsteps=Bs // seq_per_step, Lseq=SAMPLE_L, NS=seq_per_step, nchunks=1,
                       row_block0=np_rows // ROWS, t_lo=0, t_hi=Ts)

    o_ba = C_Z + H_A * DV_A
    o_qb = o_ba + 2 * H_A
    o_lr = o_qb + 2 * H_B * DK_B + 2 * H_B * DV_B
    o_ga = o_lr + GATE_RANK
    wbig_all = jnp.concatenate([w_in[..., :o_ba], w_in[..., o_qb:o_lr], w_in[..., o_ga:]], axis=-1).astype(BF16)
    wsmall_all = jnp.concatenate(
        [w_in[..., o_ba:o_qb], w_in[..., o_lr:o_ga],
         jnp.zeros((depth, D, LANES - 2 * H_A - GATE_RANK), w_in.dtype)], axis=-1).astype(F32)
    wgu_all = jnp.concatenate([expert_w_gate, expert_w_up], axis=-1).astype(BF16)
    wd_all = expert_w_down.astype(BF16)
    oa = jnp.zeros((n, H_A * DV_A), F32)
    ob = jnp.zeros((n, H_B * DV_B), F32)
    xs_buf = jnp.zeros((_moe_rows(n, MOE_TILE), D), F32)

    sa_p, conv_p, sb_p, sa_s, conv_s, sb_s = [], [], [], [], [], []
    for l in range(depth):
        big, small = _inproj(x, norm_mix[l][None].astype(F32), wbig_all[l], wsmall_all[l], tm_big, 512)

        cw = jnp.pad(conv_w[l].astype(F32), ((0, SUBLANES - CONV_W), (0, 0)))
        hp = jnp.zeros((SUBLANES, LANES), F32)
        hp = hp.at[0, S_DECAY:S_DECAY + H_A].set(a_log[l].astype(F32))
        hp = hp.at[1, S_DECAY:S_DECAY + H_A].set(dt_bias[l].astype(F32))
        na = norm_a[l][None].astype(F32)
        wup = jnp.zeros((LANES, H_B * DK_B), F32).at[S_LR:S_LR + GATE_RANK].set(gla_gate_up[l].astype(F32))
        bup = gla_gate_bias[l][None].astype(F32)
        nb = norm_b[l][None].astype(F32)

        cin_s = jnp.pad(state_conv[l].astype(F32), ((0, 0), (SUBLANES - (CONV_W - 1), 0), (0, 0)))
        oa, c_p, s_ap = _gdn(big, small, zeros_conv, zeros_sa, cw, hp, na, oa, **prompt_args)
        oa, c_s, s_as = _gdn(big, small, cin_s, state_delta[l].astype(F32), cw, hp, na, oa, **sample_args)
        ob, s_bp = _gla(big, small, zeros_sb, wup, bup, nb, ob, SB=GLA_SUB, **prompt_args)
        ob, s_bs = _gla(big, small, state_gla[l].astype(F32), wup, bup, nb, ob, SB=SAMPLE_L, **sample_args)
        x = _outproj(oa, ob, big, x, w_proj_a[l].astype(BF16), w_proj_b[l].astype(BF16),
                     w_out[l].astype(BF16), tm)

        wr = jnp.concatenate([router_expert_w[l], router_group_w[l],
                              jnp.zeros((D, LANES - N_EXPERTS - N_GROUPS), F32)], axis=1).astype(F32)
        br = jnp.concatenate([router_expert_b[l], router_group_b[l],
                              jnp.zeros((LANES - N_EXPERTS - N_GROUPS,), F32)])[None].astype(F32)
        x, xs_buf = _moe(x, norm_ffn[l][None].astype(F32), wr, br, wgu_all[l], wd_all[l],
                         norm_final[None].astype(F32), xs_buf, l == depth - 1, tm, MOE_TILE)

        sa_p.append(s_ap)
        conv_p.append(c_p[:, SUBLANES - (CONV_W - 1):])
        sb_p.append(s_bp)
        sa_s.append(s_as)
        conv_s.append(c_s[:, SUBLANES - (CONV_W - 1):])
        sb_s.append(s_bs)

    y = x
    y_prompt = y[:np_rows].reshape(Bp, Tpp, D)[:, pad_p + N_META:].astype(dt)
    y_sample = y[np_rows:].reshape(Bs, SAMPLE_L, D)[:, :Ts].astype(x_sample.dtype)
    return (y_prompt, y_sample,
            jnp.stack(sa_p).astype(dt), jnp.stack(conv_p).astype(dt), jnp.stack(sb_p).astype(dt),
            jnp.stack(sa_s).astype(state_delta.dtype), jnp.stack(conv_s).astype(state_conv.dtype),
            jnp.stack(sb_s).astype(state_gla.dtype))
```

```python
import functools

import jax
import jax.numpy as jnp
from jax import lax
from jax.experimental import pallas as pl
from jax.experimental.pallas import tpu as pltpu

F32 = jnp.float32
BF16 = jnp.bfloat16

D_MODEL = 1024
N_META = 16
CHUNK = 64
EPS = 1e-6
H_A, DK_A, DV_A = 8, 128, 128
CONV_W = 4
CONV_CH = 2 * H_A * DK_A + H_A * DV_A
H_B, DK_B, DV_B = 4, 128, 256
GATE_RANK = 16
GATE_TAU = 16.0
N_GROUPS = 4
EXPERTS_PER_GROUP = 8
N_EXPERTS = N_GROUPS * EXPERTS_PER_GROUP
D_FF_E = 256

LANES = 128
SUBLANES = 8
VMEM_LIMIT = 48 * 1024 * 1024

C_QKV = 0
C_Z = C_QKV + CONV_CH
C_QB = C_Z + H_A * DV_A
C_KB = C_QB + H_B * DK_B
C_VB = C_KB + H_B * DK_B
C_RB = C_VB + H_B * DV_B
C_GA = C_RB + H_B * DV_B
C_GB = C_GA + D_MODEL
C_BIG = C_GB + D_MODEL
S_BETA = 0
S_DECAY = H_A
S_LR = 2 * H_A

SAMPLE_L = SUBLANES
ROWS = CHUNK
GLA_SUB = 16
MOE_TILE = 256


def _dot(a, b, prec=None):
    return jnp.dot(a, b, preferred_element_type=F32, precision=prec)


def _dot_nt(a, b, prec=None):
    return lax.dot_general(a, b, (((1,), (1,)), ((), ())), preferred_element_type=F32, precision=prec)


def _dot_tn(a, b, prec=None):
    return lax.dot_general(a, b, (((0,), (0,)), ((), ())), preferred_element_type=F32, precision=prec)


def _bdot(a, b):
    return _dot(a.astype(BF16), b.astype(BF16))


def _bdot_nt(a, b):
    return _dot_nt(a.astype(BF16), b.astype(BF16))


def _bdot_tn(a, b):
    return _dot_tn(a.astype(BF16), b.astype(BF16))


def _split2(a):
    hi = a.astype(BF16)
    return hi, (a - hi.astype(F32)).astype(BF16)


def _dot3(a, b, dot=_dot):
    ah, al = _split2(a)
    bh, bl = _split2(b)
    return dot(ah, bh) + dot(al, bh) + dot(ah, bl)


def _mask_dot(m, b):
    hi = b.astype(BF16)
    r = b - hi.astype(F32)
    mid = r.astype(BF16)
    lo = (r - mid.astype(F32)).astype(BF16)
    return _dot(m, hi) + _dot(m, mid) + _dot(m, lo)


def _sigmoid(x):
    return 1.0 / (1.0 + jnp.exp(-x))


def _silu(x):
    return x * _sigmoid(x)


def _softplus(x):
    return jnp.maximum(x, 0.0) + jnp.log(1.0 + jnp.exp(-jnp.abs(x)))


def _rms(x, w):
    return x * lax.rsqrt(jnp.mean(x * x, axis=-1, keepdims=True) + EPS) * w


def _inproj_kernel(x_ref, nw_ref, wb_ref, ws_ref, big_ref, small_ref, xn_ref):
    @pl.when(pl.program_id(1) == 0)
    def _():
        xn = _rms(x_ref[...], nw_ref[...])
        xn_ref[...] = xn.astype(BF16)
        small_ref[...] = _dot3(xn, ws_ref[...])

    big_ref[...] = _dot(xn_ref[...], wb_ref[...])


def _inproj(x, nw, wbig, wsmall, layer, tm, tn):
    n = x.shape[0]
    return pl.pallas_call(
        _inproj_kernel,
        grid=(n // tm, C_BIG // tn),
        in_specs=[
            pl.BlockSpec((tm, D_MODEL), lambda i, j: (i, 0)),
            pl.BlockSpec((1, D_MODEL), lambda i, j: (0, 0)),
            pl.BlockSpec((None, D_MODEL, tn), lambda i, j: (layer, 0, j)),
            pl.BlockSpec((None, D_MODEL, LANES), lambda i, j: (layer, 0, 0)),
        ],
        out_specs=[
            pl.BlockSpec((tm, tn), lambda i, j: (i, j)),
            pl.BlockSpec((tm, LANES), lambda i, j: (i, 0)),
        ],
        out_shape=[
            jax.ShapeDtypeStruct((n, C_BIG), F32),
            jax.ShapeDtypeStruct((n, LANES), F32),
        ],
        scratch_shapes=[pltpu.VMEM((tm, D_MODEL), BF16)],
        compiler_params=pltpu.CompilerParams(
            dimension_semantics=("parallel", "arbitrary"), vmem_limit_bytes=VMEM_LIMIT),
        name="inproj",
    )(x, nw, wbig, wsmall)


def _outproj_kernel(oa_ref, ob_ref, ga_ref, gb_ref, x_ref, wa_ref, wb_ref, wo_ref, out_ref):
    ya = _dot(oa_ref[...].astype(BF16), wa_ref[...])
    yb = _dot(ob_ref[...].astype(BF16), wb_ref[...])
    merged = _sigmoid(ga_ref[...]) * ya + _sigmoid(gb_ref[...]) * yb
    out_ref[...] = x_ref[...] + _dot(merged.astype(BF16), wo_ref[...])


def _outproj(oa, ob, big, x, wa, wb, wo, layer, tm):
    wspec = pl.BlockSpec((None, D_MODEL, D_MODEL), lambda i: (layer, 0, 0))
    n = x.shape[0]
    row = lambda i: (i, 0)
    return pl.pallas_call(
        _outproj_kernel,
        grid=(n // tm,),
        in_specs=[
            pl.BlockSpec((tm, D_MODEL), row),
            pl.BlockSpec((tm, D_MODEL), row),
            pl.BlockSpec((tm, D_MODEL), lambda i: (i, C_GA // D_MODEL)),
            pl.BlockSpec((tm, D_MODEL), lambda i: (i, C_GB // D_MODEL)),
            pl.BlockSpec((tm, D_MODEL), row),
            wspec,
            wspec,
            wspec,
        ],
        out_specs=pl.BlockSpec((tm, D_MODEL), row),
        out_shape=jax.ShapeDtypeStruct((n, D_MODEL), F32),
        compiler_params=pltpu.CompilerParams(
            dimension_semantics=("parallel",), vmem_limit_bytes=VMEM_LIMIT),
        name="outproj",
    )(oa, ob, big, big, x, wa, wb, wo)


def _route(logits):
    lane = lax.broadcasted_iota(jnp.int32, logits.shape, 1)
    neg = jnp.float32(-jnp.inf)
    big = jnp.int32(1 << 20)
    gmask = (lane >= N_EXPERTS) & (lane < N_EXPERTS + N_GROUPS)
    gl = jnp.where(gmask, logits, neg)
    gmax = jnp.max(gl, axis=-1, keepdims=True)
    gsel = jnp.min(jnp.where(gl == gmax, lane - N_EXPERTS, big), axis=-1, keepdims=True)
    pgsel = 1.0 / jnp.sum(jnp.where(gmask, jnp.exp(gl - gmax), 0.0), axis=-1, keepdims=True)
    lo = gsel * EXPERTS_PER_GROUP
    emask = (lane >= lo) & (lane < lo + EXPERTS_PER_GROUP)
    el = jnp.where(emask, logits, neg)
    m1 = jnp.max(el, axis=-1, keepdims=True)
    i1 = jnp.min(jnp.where(el == m1, lane, big), axis=-1, keepdims=True)
    el2 = jnp.where(lane == i1, neg, el)
    m2 = jnp.max(el2, axis=-1, keepdims=True)
    i2 = jnp.min(jnp.where(el2 == m2, lane, big), axis=-1, keepdims=True)
    r = jnp.exp(m2 - m1)
    w1 = pgsel / (1.0 + r)
    w2 = pgsel * r / (1.0 + r)
    return i1, i2, w1, w2


def _router_kernel(x_ref, nw_ref, wr_ref, br_ref, xn_ref, idx_ref, wt_ref, cnt_ref):
    tm = x_ref.shape[0]
    xn = _rms(x_ref[...], nw_ref[...])
    xn_ref[...] = xn
    i1, i2, w1, w2 = _route(_dot3(xn, wr_ref[...]) + br_ref[...])
    lane = lax.broadcasted_iota(jnp.int32, (tm, LANES), 1)
    onehot = jnp.where((lane == i1) | (lane == i2), 1.0, 0.0)
    r = lax.broadcasted_iota(jnp.int32, (tm, tm), 0)
    c = lax.broadcasted_iota(jnp.int32, (tm, tm), 1)
    earlier = jnp.where(c < r, 1.0, 0.0).astype(BF16)
    cum = _dot(earlier, onehot.astype(BF16))
    rank1 = jnp.sum(jnp.where(lane == i1, cum, 0.0), axis=-1, keepdims=True).astype(jnp.int32)
    rank2 = jnp.sum(jnp.where(lane == i2, cum, 0.0), axis=-1, keepdims=True).astype(jnp.int32)
    idx_ref[...] = jnp.where(lane == 0, i1, jnp.where(lane == 1, i2, jnp.where(
        lane == 2, rank1, jnp.where(lane == 3, rank2, 0))))
    wt_ref[...] = jnp.where(lane == 0, w1, jnp.where(lane == 1, w2, 0.0))
    counts = jnp.sum(onehot, axis=0, keepdims=True).astype(jnp.int32)
    cnt_ref[...] = jnp.broadcast_to(counts, cnt_ref.shape)


def _router(x, nw, wr, br, tm):
    n = x.shape[0]
    row = lambda i: (i, 0)
    full = lambda i: (0, 0)
    return pl.pallas_call(
        _router_kernel,
        grid=(n // tm,),
        in_specs=[
            pl.BlockSpec((tm, D_MODEL), row),
            pl.BlockSpec((1, D_MODEL), full),
            pl.BlockSpec((D_MODEL, LANES), full),
            pl.BlockSpec((1, LANES), full),
        ],
        out_specs=[
            pl.BlockSpec((tm, D_MODEL), row),
            pl.BlockSpec((tm, LANES), row),
            pl.BlockSpec((tm, LANES), row),
            pl.BlockSpec((None, SUBLANES, LANES), lambda i: (i, 0, 0)),
        ],
        out_shape=[
            jax.ShapeDtypeStruct((n, D_MODEL), F32),
            jax.ShapeDtypeStruct((n, LANES), jnp.int32),
            jax.ShapeDtypeStruct((n, LANES), F32),
            jax.ShapeDtypeStruct((n // tm, SUBLANES, LANES), jnp.int32),
        ],
        compiler_params=pltpu.CompilerParams(
            dimension_semantics=("parallel",), vmem_limit_bytes=VMEM_LIMIT),
        name="router",
    )(x, nw, wr, br)


def _moe_plan(idx, cnt, tm_tok, tm_exp):
    n = idx.shape[0]
    nt = 2 * n // tm_exp + N_EXPERTS
    counts = cnt[:, 0, :N_EXPERTS]
    cpad = (jnp.sum(counts, axis=0) + tm_exp - 1) // tm_exp * tm_exp
    ends = jnp.cumsum(cpad)
    base = (ends - cpad)[None] + jnp.cumsum(counts, axis=0) - counts
    base_tok = jnp.repeat(base, tm_tok, axis=0)[:, None, :]
    hit = idx[:, :2, None] == jnp.arange(N_EXPERTS, dtype=jnp.int32)
    dest = (jnp.sum(jnp.where(hit, base_tok, 0), axis=-1) + idx[:, 2:4]).astype(jnp.int32)
    tile_start = jnp.arange(nt, dtype=jnp.int32) * tm_exp
    tile_e = jnp.minimum(jnp.sum(tile_start[:, None] >= ends[None, :], axis=1), N_EXPERTS - 1).astype(jnp.int32)
    tile_valid = (tile_start < ends[-1]).astype(jnp.int32)
    return dest, tile_e, tile_valid


def _slot_major(dest, tm):
    nt = dest.shape[0] // tm
    return jnp.swapaxes(dest.reshape(nt, tm, 2), 1, 2).reshape(nt, 1, 2 * tm)


def _dispatch_kernel(d_ref, xn_ref, xs_in_hbm, xs_hbm, sem):
    del xs_in_hbm
    tm = xn_ref.shape[0]

    def copy(i, j, dst_row):
        return pltpu.make_async_copy(xn_ref.at[pl.ds(i, 1), :], xs_hbm.at[pl.ds(dst_row, 1), :], sem.at[j])

    def start(i, carry):
        for j in range(2):
            copy(i, j, d_ref[0, j * tm + i]).start(priority=j)
        return carry

    def wait(i, carry):
        for j in range(2):
            copy(i, j, 0).wait()
        return carry

    lax.fori_loop(0, tm, start, 0, unroll=4)
    lax.fori_loop(0, tm, wait, 0, unroll=4)


def _dispatch(xn, dest, xs_buf, tm):
    n = xn.shape[0]
    nrows = xs_buf.shape[0]
    return pl.pallas_call(
        _dispatch_kernel,
        grid=(n // tm,),
        in_specs=[
            pl.BlockSpec((None, 1, 2 * tm), lambda t: (t, 0, 0), memory_space=pltpu.SMEM),
            pl.BlockSpec((tm, D_MODEL), lambda t: (t, 0)),
            pl.BlockSpec(memory_space=pl.ANY),
        ],
        out_specs=pl.BlockSpec(memory_space=pl.ANY),
        out_shape=jax.ShapeDtypeStruct((nrows, D_MODEL), F32),
        scratch_shapes=[pltpu.SemaphoreType.DMA((2,))],
        input_output_aliases={2: 0},
        compiler_params=pltpu.CompilerParams(
            dimension_semantics=("arbitrary",), vmem_limit_bytes=VMEM_LIMIT),
        name="dispatch",
    )(_slot_major(dest, tm), xn, xs_buf)


def _row_gather(idx_ref, src_hbm, buf, sem, slot, nrows):
    def body(r, carry):
        for j in range(2):
            rr = 2 * r + j
            pltpu.make_async_copy(src_hbm.at[pl.ds(idx_ref[0, rr], 1), :],
                                  buf.at[slot, pl.ds(rr, 1), :], sem.at[slot]).start(priority=j)
        return carry
    lax.fori_loop(0, nrows // 2, body, 0, unroll=4)


def _row_gather_wait(src_hbm, buf, sem, slot, nrows):
    def body(r, carry):
        pltpu.make_async_copy(src_hbm.at[pl.ds(0, 1), :],
                              buf.at[slot, pl.ds(r, 1), :], sem.at[slot]).wait()
        return carry
    lax.fori_loop(0, nrows, body, 0, unroll=8)


def _expert_kernel(te_ref, tv_ref, xs_ref, wgu_ref, wd_ref, y_ref):
    t = pl.program_id(0)

    @pl.when(tv_ref[t] == 1)
    def _():
        h = _dot(xs_ref[...].astype(BF16), wgu_ref[...])
        a = _silu(h[:, :D_FF_E]) * h[:, D_FF_E:]
        y_ref[...] = _dot(a.astype(BF16), wd_ref[...])

    @pl.when(tv_ref[t] == 0)
    def _():
        y_ref[...] = jnp.zeros_like(y_ref)


def _experts(xs, tile_e, tile_valid, wgu, wd, layer, tm):
    nt = xs.shape[0] // tm
    return pl.pallas_call(
        _expert_kernel,
        grid_spec=pltpu.PrefetchScalarGridSpec(
            num_scalar_prefetch=2,
            grid=(nt,),
            in_specs=[
                pl.BlockSpec((tm, D_MODEL), lambda t, te, tv: (t, 0)),
                pl.BlockSpec((None, None, D_MODEL, 2 * D_FF_E), lambda t, te, tv: (layer, te[t], 0, 0)),
                pl.BlockSpec((None, None, D_FF_E, D_MODEL), lambda t, te, tv: (layer, te[t], 0, 0)),
            ],
            out_specs=pl.BlockSpec((tm, D_MODEL), lambda t, te, tv: (t, 0)),
        ),
        out_shape=jax.ShapeDtypeStruct((nt * tm, D_MODEL), F32),
        compiler_params=pltpu.CompilerParams(
            dimension_semantics=("arbitrary",), vmem_limit_bytes=VMEM_LIMIT),
        name="experts",
    )(tile_e, tile_valid, xs, wgu, wd)


def _combine_kernel(split_tile, d_ref, d_next_ref, y_hbm, x_ref, wt_ref, fnw_ref, *rest):
    out_refs, (ybuf, sem) = rest[:-2], rest[-2:]
    t = pl.program_id(0)
    nt = pl.num_programs(0)
    tm = x_ref.shape[0]
    slot = t & 1

    @pl.when(t == 0)
    def _():
        _row_gather(d_ref, y_hbm, ybuf, sem, 0, 2 * tm)

    @pl.when(t + 1 < nt)
    def _():
        _row_gather(d_next_ref, y_hbm, ybuf, sem, 1 - slot, 2 * tm)

    _row_gather_wait(y_hbm, ybuf, sem, slot, 2 * tm)
    wt = wt_ref[...]
    out = x_ref[...] + wt[:, 0:1] * ybuf[slot, 0:tm, :] + wt[:, 1:2] * ybuf[slot, tm:2 * tm, :]
    if split_tile is None:
        out_refs[0][...] = out
    else:
        res = _rms(out, fnw_ref[...])

        @pl.when(t < split_tile)
        def _():
            out_refs[0][...] = res

        @pl.when(t >= split_tile)
        def _():
            out_refs[1][...] = res


def _combine(x, y, dest, wt, fnw, split_rows, tm):
    n = x.shape[0]
    nt = n // tm
    d = _slot_major(dest, tm)
    if split_rows is None:
        split_tile = None
        out_specs = pl.BlockSpec((tm, D_MODEL), lambda t: (t, 0))
        out_shape = jax.ShapeDtypeStruct((n, D_MODEL), F32)
    else:
        split_tile = split_rows // tm
        assert split_tile * tm == split_rows and 0 < split_tile < nt
        out_specs = [pl.BlockSpec((tm, D_MODEL), lambda t: (jnp.minimum(t, split_tile - 1), 0)),
                     pl.BlockSpec((tm, D_MODEL), lambda t: (jnp.maximum(t - split_tile, 0), 0))]
        out_shape = [jax.ShapeDtypeStruct((split_rows, D_MODEL), F32),
                     jax.ShapeDtypeStruct((n - split_rows, D_MODEL), F32)]
    return pl.pallas_call(
        functools.partial(_combine_kernel, split_tile),
        grid=(nt,),
        in_specs=[
            pl.BlockSpec((None, 1, 2 * tm), lambda t: (t, 0, 0), memory_space=pltpu.SMEM),
            pl.BlockSpec((None, 1, 2 * tm), lambda t: (jnp.minimum(t + 1, nt - 1), 0, 0),
                         memory_space=pltpu.SMEM),
            pl.BlockSpec(memory_space=pl.ANY),
            pl.BlockSpec((tm, D_MODEL), lambda t: (t, 0)),
            pl.BlockSpec((tm, LANES), lambda t: (t, 0)),
            pl.BlockSpec((1, D_MODEL), lambda t: (0, 0)),
        ],
        out_specs=out_specs,
        out_shape=out_shape,
        scratch_shapes=[pltpu.VMEM((2, 2 * tm, D_MODEL), F32), pltpu.SemaphoreType.DMA((2,))],
        compiler_params=pltpu.CompilerParams(
            dimension_semantics=("arbitrary",), vmem_limit_bytes=VMEM_LIMIT),
        name="combine",
    )(d, d, y, x, wt, fnw)


def _moe_rows(n, tm_exp):
    return (2 * n // tm_exp + N_EXPERTS) * tm_exp


def _moe(x, nw, wr, br, wgu, wd, layer, fnw, xs_buf, split_rows, tm_tok, tm_exp):
    xn, idx, wt, cnt = _router(x, nw, wr, br, tm_tok)
    dest, tile_e, tile_valid = _moe_plan(idx, cnt, tm_tok, tm_exp)
    xs = _dispatch(xn, dest, xs_buf, tm_tok)
    y = _experts(xs, tile_e, tile_valid, wgu, wd, layer, tm_exp)
    return _combine(x, y, dest, wt, fnw, split_rows, tm_exp), xs


def _seq_masks(n, lseq):
    r = lax.broadcasted_iota(jnp.int32, (n, n), 0)
    c = lax.broadcasted_iota(jnp.int32, (n, n), 1)
    sh = lseq.bit_length() - 1
    same = (r >> sh) == (c >> sh)
    return same & (r >= c), same & (r > c), r == c


def _valid_rows(n, lseq, t_lo, t_hi):
    row = lax.broadcasted_iota(jnp.int32, (n, 1), 0)
    t = (row & (lseq - 1)) + pl.program_id(1) * lseq
    return (t >= t_lo) & (t < t_hi)


def _pad_rows(a, n):
    return jnp.concatenate([a, jnp.zeros((n - a.shape[0], a.shape[1]), a.dtype)], axis=0)


def _gdn_kernel(Lseq, NS, t_lo, t_hi, conv_row0,
                qkv_ref, z_ref, sm_ref, cin_ref, sin_ref, cw_ref, hp_ref, na_ref, obuf_ref,
                o_ref, cout_ref, so_ref, ext_ref):
    del obuf_ref
    R = NS * Lseq
    R2 = 2 * R
    npairs = H_A // 2

    @pl.when(pl.program_id(1) == 0)
    def _():
        ext_ref[:, 0:SUBLANES, :] = cin_ref[...]
        so_ref[...] = sin_ref[...]

    valid = _valid_rows(R, Lseq, t_lo, t_hi)
    valid_l = _valid_rows(Lseq, Lseq, t_lo, t_hi)
    for s in range(NS):
        ext_ref[s, SUBLANES:SUBLANES + Lseq, :] = jnp.where(
            valid_l, qkv_ref[s * Lseq:(s + 1) * Lseq, :], 0.0)

    def conv_act(col0):
        pieces = []
        for s in range(NS):
            acc = None
            for i in range(CONV_W):
                r0 = SUBLANES - (CONV_W - 1) + i
                term = cw_ref[i:i + 1, col0:col0 + LANES] * ext_ref[s, r0:r0 + Lseq, col0:col0 + LANES]
                acc = term if acc is None else acc + term
            pieces.append(acc)
        return _silu(pieces[0] if NS == 1 else jnp.concatenate(pieces, axis=0))

    incl, _, _ = _seq_masks(R, Lseq)
    incl2, strict2, diag2 = _seq_masks(R2, Lseq)
    tri = jnp.where(incl, 1.0, 0.0).astype(BF16)
    eye2 = jnp.where(diag2, 1.0, 0.0)

    sm = sm_ref[...]
    beta_all = jnp.where(valid, _sigmoid(sm), 0.0)
    g_all = jnp.where(valid, -jnp.exp(hp_ref[0:1, :]) * _softplus(sm + hp_ref[1:2, :]), 0.0)
    G_all = _mask_dot(tri, g_all)
    GT = jnp.transpose(_pad_rows(G_all, LANES))

    q2, k2, v2, beta2, Gc2, Gr2 = [], [], [], [], [], []
    for p in range(npairs):
        qs, ks, vs = [], [], []
        for h in (2 * p, 2 * p + 1):
            q = conv_act(h * DK_A)
            k = conv_act(H_A * DK_A + h * DK_A)
            qs.append(q * lax.rsqrt(jnp.sum(q * q, axis=-1, keepdims=True) + EPS) * (DK_A ** -0.5))
            ks.append(jnp.where(valid, k * lax.rsqrt(jnp.sum(k * k, axis=-1, keepdims=True) + EPS), 0.0))
            vs.append(conv_act(2 * H_A * DK_A + h * DV_A))
        q2.append(jnp.concatenate(qs, axis=0))
        k2.append(jnp.concatenate(ks, axis=0))
        v2.append(jnp.concatenate(vs, axis=0))
        h0 = 2 * p
        beta2.append(jnp.concatenate([beta_all[:, S_BETA + h0:S_BETA + h0 + 1],
                                      beta_all[:, S_BETA + h0 + 1:S_BETA + h0 + 2]], axis=0))
        Gc2.append(jnp.concatenate([G_all[:, S_DECAY + h0:S_DECAY + h0 + 1],
                                    G_all[:, S_DECAY + h0 + 1:S_DECAY + h0 + 2]], axis=0))
        Gr2.append(jnp.concatenate([GT[S_DECAY + h0:S_DECAY + h0 + 1, :R],
                                    GT[S_DECAY + h0 + 1:S_DECAY + h0 + 2, :R]], axis=1))

    rng = range(npairs)
    kk = [_bdot_nt(k2[p], k2[p]) for p in rng]
    qk = [_bdot_nt(q2[p], k2[p]) for p in rng]
    dec = [jnp.where(incl2, jnp.exp(jnp.where(incl2, Gc2[p] - Gr2[p], 0.0)), 0.0) for p in rng]
    Bp = [-jnp.where(strict2, beta2[p] * kk[p] * dec[p], 0.0) for p in rng]
    P = [eye2 + Bp[p] for p in rng]
    n = 2
    while n < Lseq:
        Bp = [_dot3(b, b) for b in Bp]
        P = [pp + _dot3(pp, b) for pp, b in zip(P, Bp)]
        n *= 2
    eG = [jnp.exp(Gc2[p]) for p in rng]
    rhs = [jnp.concatenate([beta2[p] * v2[p], (beta2[p] * eG[p]) * k2[p]], axis=1) for p in rng]
    sol = [_dot3(P[p], rhs[p]) for p in rng]
    qk = [qk[p] * dec[p] for p in rng]
    qe = [q2[p] * eG[p] for p in rng]

    items = [(p, j, s) for p in rng for j in range(2) for s in range(NS)]

    def rows(j, s):
        return slice(j * R + s * Lseq, j * R + (s + 1) * Lseq)

    S = {it: so_ref[it[2], 2 * it[0] + it[1]] for it in items}
    both = {}
    for p, j, s in items:
        r = rows(j, s)
        both[(p, j, s)] = _bdot(jnp.concatenate([sol[p][r, DV_A:], qe[p][r]], axis=0), S[(p, j, s)])
    vnew = {}
    for p, j, s in items:
        vnew[(p, j, s)] = sol[p][rows(j, s), :DV_A] - both[(p, j, s)][:Lseq]
    for p, j, s in items:
        r = rows(j, s)
        last = j * R + (s + 1) * Lseq - 1
        GL = Gc2[p][last:last + 1]
        kd = k2[p][r] * jnp.exp(GL - Gc2[p][r])
        so_ref[s, 2 * p + j] = S[(p, j, s)] * jnp.exp(GL) + _bdot_tn(kd, vnew[(p, j, s)])
    for p in rng:
        order = [(p, j, s) for j in range(2) for s in range(NS)]
        vnew2 = jnp.concatenate([vnew[it] for it in order], axis=0)
        o2 = jnp.concatenate([both[it][Lseq:] for it in order], axis=0) + _bdot(qk[p], vnew2)
        for j in range(2):
            h = 2 * p + j
            zh = z_ref[:, h * DV_A:(h + 1) * DV_A]
            o_ref[:, h * DV_A:(h + 1) * DV_A] = _rms(o2[j * R:(j + 1) * R], na_ref[...]) * _silu(zh)

    for s in range(NS):
        cout_ref[s] = ext_ref[s, conv_row0:conv_row0 + SUBLANES, :]
        ext_ref[s, 0:SUBLANES, :] = ext_ref[s, Lseq:Lseq + SUBLANES, :]


def _gdn(big, small, cin, sin, cw, hp, na, obuf, *, nsteps, Lseq, NS, nchunks, row_block, t_lo, t_hi):
    assert NS * Lseq == ROWS
    conv_row0 = t_hi - (nchunks - 1) * Lseq
    kern = functools.partial(_gdn_kernel, Lseq, NS, t_lo, t_hi, conv_row0)
    rb = row_block
    return pl.pallas_call(
        kern,
        grid=(nsteps, nchunks),
        in_specs=[
            pl.BlockSpec((ROWS, CONV_CH), lambda b, c: (rb(b, c), 0)),
            pl.BlockSpec((ROWS, H_A * DV_A), lambda b, c: (rb(b, c), C_Z // (H_A * DV_A))),
            pl.BlockSpec((ROWS, LANES), lambda b, c: (rb(b, c), 0)),
            pl.BlockSpec((NS, SUBLANES, CONV_CH), lambda b, c: (b, 0, 0)),
            pl.BlockSpec((NS, H_A, DK_A, DV_A), lambda b, c: (b, 0, 0, 0)),
            pl.BlockSpec((SUBLANES, CONV_CH), lambda b, c: (0, 0)),
            pl.BlockSpec((SUBLANES, LANES), lambda b, c: (0, 0)),
            pl.BlockSpec((1, DV_A), lambda b, c: (0, 0)),
            pl.BlockSpec(memory_space=pl.ANY),
        ],
        out_specs=[
            pl.BlockSpec((ROWS, H_A * DV_A), lambda b, c: (rb(b, c), 0)),
            pl.BlockSpec((NS, SUBLANES, CONV_CH), lambda b, c: (b, 0, 0)),
            pl.BlockSpec((NS, H_A, DK_A, DV_A), lambda b, c: (b, 0, 0, 0)),
        ],
        out_shape=[
            jax.ShapeDtypeStruct(obuf.shape, F32),
            jax.ShapeDtypeStruct((nsteps * NS, SUBLANES, CONV_CH), F32),
            jax.ShapeDtypeStruct((nsteps * NS, H_A, DK_A, DV_A), F32),
        ],
        scratch_shapes=[pltpu.VMEM((NS, SUBLANES + Lseq, CONV_CH), F32)],
        input_output_aliases={8: 0},
        compiler_params=pltpu.CompilerParams(
            dimension_semantics=("parallel", "arbitrary"), vmem_limit_bytes=VMEM_LIMIT),
        name="gdn_L%d" % Lseq,
    )(big, big, small, cin, sin, cw, hp, na, obuf)


def _gla_kernel(Lseq, NS, SB, t_lo, t_hi,
                qb_ref, kb_ref, vb_ref, rb_ref, sm_ref, sin_ref, wup_ref, bup_ref, nb_ref, obuf_ref,
                o_ref, so_ref):
    del obuf_ref
    R = NS * Lseq
    heads = range(H_B)

    @pl.when(pl.program_id(1) == 0)
    def _():
        so_ref[...] = sin_ref[...]

    nb = Lseq // SB
    assert nb == 1 or NS == 1
    n_off = SB * nb * (nb - 1) // 2
    sb_shift = SB.bit_length() - 1

    valid = _valid_rows(R, Lseq, t_lo, t_hi)
    r = lax.broadcasted_iota(jnp.int32, (R, R), 0)
    c = lax.broadcasted_iota(jnp.int32, (R, R), 1)
    same = (r | (Lseq - 1)) == (c | (Lseq - 1))
    m_row = jnp.where(same & (c <= r), 1.0, 0.0)
    m_first = jnp.where(same & (c <= (r & ~(SB - 1))), 1.0, 0.0)
    m_last = jnp.where(same, 1.0, 0.0)
    masks = jnp.concatenate([m_row, m_first, m_last], axis=0).astype(BF16)

    pre = _dot3(sm_ref[...], wup_ref[...]) + bup_ref[...]
    lg = jnp.where(valid, -_softplus(-pre) / GATE_TAU, 0.0)
    cums = _mask_dot(masks, lg)
    Bc_all, b0_all, BL_all = cums[:R], cums[R:2 * R], cums[2 * R:]

    re = lax.broadcasted_iota(jnp.int32, (R, n_off + R), 0)
    ce = lax.broadcasted_iota(jnp.int32, (R, n_off + R), 1)
    rblk = (re & (Lseq - 1)) >> sb_shift
    cd = ce - n_off
    see = ((ce >= (SB // 2) * rblk * (rblk - 1)) & (ce < (SB // 2) * rblk * (rblk + 1))) | (
        (cd >= 0) & ((cd >> sb_shift) == (re >> sb_shift)) & (cd <= re))

    def hs(a, h, w):
        return a[:, h * w:(h + 1) * w]

    q = [hs(qb_ref, h, DK_B) * (DK_B ** -0.5) for h in heads]
    k = [jnp.where(valid, hs(kb_ref, h, DK_B), 0.0) for h in heads]
    v = [hs(vb_ref, h, DV_B) for h in heads]
    Bc = [hs(Bc_all, h, DK_B) for h in heads]
    b0 = [hs(b0_all, h, DK_B) for h in heads]
    BL = [hs(BL_all, h, DK_B) for h in heads]
    qs = [q[h] * jnp.exp(Bc[h] - b0[h]) for h in heads]
    kx, vx = [], []
    for h in heads:
        kparts = [k[h][:i * SB] * jnp.exp(Bc[h][i * SB:i * SB + 1] - Bc[h][:i * SB]) for i in range(1, nb)]
        kparts.append(k[h] * jnp.exp(b0[h] - Bc[h]))
        vparts = [v[h][:i * SB] for i in range(1, nb)] + [v[h]]
        kx.append(kparts[0] if nb == 1 else jnp.concatenate(kparts, axis=0))
        vx.append(vparts[0] if nb == 1 else jnp.concatenate(vparts, axis=0))
    Ax = [jnp.where(see, _dot3(qs[h], kx[h], _dot_nt), 0.0) for h in heads]
    o_intra = [_bdot(Ax[h], vx[h]) for h in heads]
    qe = [q[h] * jnp.exp(Bc[h]) for h in heads]
    kd = [k[h] * jnp.exp(BL[h] - Bc[h]) for h in heads]
    ET = [jnp.transpose(_pad_rows(jnp.exp(BL[h]), LANES)) for h in heads]

    items = [(h, s) for h in heads for s in range(NS)]
    S = {it: so_ref[it[1], it[0]] for it in items}
    o_inter = {}
    for h, s in items:
        o_inter[(h, s)] = _bdot(qe[h][s * Lseq:(s + 1) * Lseq], S[(h, s)])
    for h, s in items:
        rows = slice(s * Lseq, (s + 1) * Lseq)
        so_ref[s, h] = S[(h, s)] * ET[h][:, s * Lseq:s * Lseq + 1] + _bdot_tn(kd[h][rows], v[h][rows])
    for h in heads:
        parts = [o_inter[(h, s)] for s in range(NS)]
        o = (parts[0] if NS == 1 else jnp.concatenate(parts, axis=0)) + o_intra[h]
        rh = hs(rb_ref, h, DV_B)
        o_ref[:, h * DV_B:(h + 1) * DV_B] = _rms(o, nb_ref[...]) * _silu(rh)


def _gla(big, small, sin, wup, bup, nb, obuf, *, nsteps, Lseq, NS, SB, nchunks, row_block, t_lo, t_hi):
    assert NS * Lseq == ROWS
    kern = functools.partial(_gla_kernel, Lseq, NS, SB, t_lo, t_hi)
    r = row_block
    return pl.pallas_call(
        kern,
        grid=(nsteps, nchunks),
        in_specs=[
            pl.BlockSpec((ROWS, H_B * DK_B), lambda b, c: (r(b, c), C_QB // (H_B * DK_B))),
            pl.BlockSpec((ROWS, H_B * DK_B), lambda b, c: (r(b, c), C_KB // (H_B * DK_B))),
            pl.BlockSpec((ROWS, H_B * DV_B), lambda b, c: (r(b, c), C_VB // (H_B * DV_B))),
            pl.BlockSpec((ROWS, H_B * DV_B), lambda b, c: (r(b, c), C_RB // (H_B * DV_B))),
            pl.BlockSpec((ROWS, LANES), lambda b, c: (r(b, c), 0)),
            pl.BlockSpec((NS, H_B, DK_B, DV_B), lambda b, c: (b, 0, 0, 0)),
            pl.BlockSpec((LANES, H_B * DK_B), lambda b, c: (0, 0)),
            pl.BlockSpec((1, H_B * DK_B), lambda b, c: (0, 0)),
            pl.BlockSpec((1, DV_B), lambda b, c: (0, 0)),
            pl.BlockSpec(memory_space=pl.ANY),
        ],
        out_specs=[
            pl.BlockSpec((ROWS, H_B * DV_B), lambda b, c: (r(b, c), 0)),
            pl.BlockSpec((NS, H_B, DK_B, DV_B), lambda b, c: (b, 0, 0, 0)),
        ],
        out_shape=[
            jax.ShapeDtypeStruct(obuf.shape, F32),
            jax.ShapeDtypeStruct((nsteps * NS, H_B, DK_B, DV_B), F32),
        ],
        input_output_aliases={9: 0},
        compiler_params=pltpu.CompilerParams(
            dimension_semantics=("parallel", "arbitrary"), vmem_limit_bytes=VMEM_LIMIT),
        name="gla_L%d" % Lseq,
    )(big, big, big, big, small, sin, wup, bup, nb, obuf)


def kernel(x_prompt, x_sample, state_delta, state_conv, state_gla, meta_tokens, norm_mix, w_in, conv_w, a_log, dt_bias, norm_a, gla_gate_up, gla_gate_bias, norm_b, w_proj_a, w_proj_b, w_out, norm_ffn, router_group_w, router_group_b, router_expert_w, router_expert_b, expert_w_gate, expert_w_up, expert_w_down, norm_final):
    Bp, Tp, D = x_prompt.shape
    Bs, Ts, _ = x_sample.shape
    depth = w_in.shape[0]
    assert N_META <= CHUNK and Tp % CHUNK == 0 and CHUNK == ROWS
    pad_p = CHUNK - N_META
    Tpp = CHUNK + Tp
    ncp = Tpp // CHUNK
    n_prompt = Bp * Tp
    n_head = Bp * CHUNK
    ns_rows = Bs * SAMPLE_L
    n = n_prompt + n_head + ns_rows
    seq_per_step = ROWS // SAMPLE_L
    dt = x_prompt.dtype
    head = jnp.concatenate([
        jnp.zeros((Bp, pad_p, D), dt),
        jnp.broadcast_to(meta_tokens.astype(dt)[None], (Bp, N_META, D))], axis=1).reshape(n_head, D)
    xs = jnp.pad(x_sample, ((0, 0), (0, SAMPLE_L - Ts), (0, 0))).reshape(ns_rows, D)
    x = jnp.concatenate([x_prompt.reshape(n_prompt, D), head, xs.astype(dt)], axis=0).astype(F32)

    tm_big = 1280 if n % 1280 == 0 else 512
    tm = 512

    zeros_conv = jnp.zeros((Bp, SUBLANES, CONV_CH), F32)
    zeros_sa = jnp.zeros((Bp, H_A, DK_A, DV_A), F32)
    zeros_sb = jnp.zeros((Bp, H_B, DK_B, DV_B), F32)
    head_block0 = n_prompt // ROWS
    sample_block0 = (n_prompt + n_head) // ROWS
    prompt_args = dict(
        nsteps=Bp, Lseq=CHUNK, NS=1, nchunks=ncp, t_lo=pad_p, t_hi=Tpp,
        row_block=lambda b, c: jnp.where(c == 0, head_block0 + b, b * (ncp - 1) + c - 1))
    sample_args = dict(
        nsteps=Bs // seq_per_step, Lseq=SAMPLE_L, NS=seq_per_step, nchunks=1, t_lo=0, t_hi=Ts,
        row_block=lambda b, c: sample_block0 + b)

    o_ba = C_Z + H_A * DV_A
    o_qb = o_ba + 2 * H_A
    o_lr = o_qb + 2 * H_B * DK_B + 2 * H_B * DV_B
    o_ga = o_lr + GATE_RANK
    wbig_all = jnp.concatenate([w_in[..., :o_ba], w_in[..., o_qb:o_lr], w_in[..., o_ga:]], axis=-1).astype(BF16)
    wsmall_all = jnp.concatenate(
        [w_in[..., o_ba:o_qb], w_in[..., o_lr:o_ga],
         jnp.zeros((depth, D, LANES - 2 * H_A - GATE_RANK), w_in.dtype)], axis=-1).astype(F32)
    wgu_all = jnp.concatenate([expert_w_gate, expert_w_up], axis=-1).astype(BF16)
    wd_all = expert_w_down.astype(BF16)
    wa_all, wb_all, wo_all = w_proj_a.astype(BF16), w_proj_b.astype(BF16), w_out.astype(BF16)
    oa = jnp.zeros((n, H_A * DV_A), F32)
    ob = jnp.zeros((n, H_B * DV_B), F32)
    xs_buf = jnp.zeros((_moe_rows(n, MOE_TILE), D), F32)

    sa_p, conv_p, sb_p, sa_s, conv_s, sb_s = [], [], [], [], [], []
    for l in range(depth):
        big, small = _inproj(x, norm_mix[l][None].astype(F32), wbig_all, wsmall_all, l, tm_big, 512)

        cw = jnp.pad(conv_w[l].astype(F32), ((0, SUBLANES - CONV_W), (0, 0)))
        hp = jnp.zeros((SUBLANES, LANES), F32)
        hp = hp.at[0, S_DECAY:S_DECAY + H_A].set(a_log[l].astype(F32))
        hp = hp.at[1, S_DECAY:S_DECAY + H_A].set(dt_bias[l].astype(F32))
        na = norm_a[l][None].astype(F32)
        wup = jnp.zeros((LANES, H_B * DK_B), F32).at[S_LR:S_LR + GATE_RANK].set(gla_gate_up[l].astype(F32))
        bup = gla_gate_bias[l][None].astype(F32)
        nb = norm_b[l][None].astype(F32)

        cin_s = jnp.pad(state_conv[l].astype(F32), ((0, 0), (SUBLANES - (CONV_W - 1), 0), (0, 0)))
        oa, c_p, s_ap = _gdn(big, small, zeros_conv, zeros_sa, cw, hp, na, oa, **prompt_args)
        oa, c_s, s_as = _gdn(big, small, cin_s, state_delta[l].astype(F32), cw, hp, na, oa, **sample_args)
        ob, s_bp = _gla(big, small, zeros_sb, wup, bup, nb, ob, SB=GLA_SUB, **prompt_args)
        ob, s_bs = _gla(big, small, state_gla[l].astype(F32), wup, bup, nb, ob, SB=SAMPLE_L, **sample_args)
        x = _outproj(oa, ob, big, x, wa_all, wb_all, wo_all, l, tm)

        wr = jnp.concatenate([router_expert_w[l], router_group_w[l],
                              jnp.zeros((D, LANES - N_EXPERTS - N_GROUPS), F32)], axis=1).astype(F32)
        br = jnp.concatenate([router_expert_b[l], router_group_b[l],
                              jnp.zeros((LANES - N_EXPERTS - N_GROUPS,), F32)])[None].astype(F32)
        x, xs_buf = _moe(x, norm_ffn[l][None].astype(F32), wr, br, wgu_all, wd_all, l,
                         norm_final[None].astype(F32), xs_buf,
                         n_prompt if l == depth - 1 else None, tm, MOE_TILE)

        sa_p.append(s_ap)
        conv_p.append(c_p[:, SUBLANES - (CONV_W - 1):])
        sb_p.append(s_bp)
        sa_s.append(s_as)
        conv_s.append(c_s[:, SUBLANES - (CONV_W - 1):])
        sb_s.append(s_bs)

    y_first, y_rest = x
    y_prompt = y_first.reshape(Bp, Tp, D).astype(dt)
    y_sample = y_rest[n_head:].reshape(Bs, SAMPLE_L, D)[:, :Ts].astype(x_sample.dtype)
    return (y_prompt, y_sample,
            jnp.stack(sa_p).astype(dt), jnp.stack(conv_p).astype(dt), jnp.stack(sb_p).astype(dt),
            jnp.stack(sa_s).astype(state_delta.dtype), jnp.stack(conv_s).astype(state_conv.dtype),
            jnp.stack(sb_s).astype(state_gla.dtype))
```

```python
import functools

import jax
import jax.numpy as jnp
from jax import lax
from jax.experimental import pallas as pl
from jax.experimental.pallas import tpu as pltpu

F32 = jnp.float32
BF16 = jnp.bfloat16

D_MODEL = 1024
N_META = 16
CHUNK = 64
EPS = 1e-6
H_A, DK_A, DV_A = 8, 128, 128
CONV_W = 4
CONV_CH = 2 * H_A * DK_A + H_A * DV_A
H_B, DK_B, DV_B = 4, 128, 256
GATE_RANK = 16
GATE_TAU = 16.0
N_GROUPS = 4
EXPERTS_PER_GROUP = 8
N_EXPERTS = N_GROUPS * EXPERTS_PER_GROUP
D_FF_E = 256

LANES = 128
SUBLANES = 8
VMEM_LIMIT = 48 * 1024 * 1024

C_QKV = 0
C_Z = C_QKV + CONV_CH
C_QB = C_Z + H_A * DV_A
C_KB = C_QB + H_B * DK_B
C_VB = C_KB + H_B * DK_B
C_RB = C_VB + H_B * DV_B
C_GA = C_RB + H_B * DV_B
C_GB = C_GA + D_MODEL
C_BIG = C_GB + D_MODEL
S_BETA = 0
S_DECAY = H_A
S_LR = 2 * H_A

SAMPLE_L = SUBLANES
ROWS = CHUNK
GLA_SUB = 16
MOE_TILE = 256


def _dot(a, b, prec=None):
    return jnp.dot(a, b, preferred_element_type=F32, precision=prec)


def _dot_nt(a, b, prec=None):
    return lax.dot_general(a, b, (((1,), (1,)), ((), ())), preferred_element_type=F32, precision=prec)


def _dot_tn(a, b, prec=None):
    return lax.dot_general(a, b, (((0,), (0,)), ((), ())), preferred_element_type=F32, precision=prec)


def _bdot(a, b):
    return _dot(a.astype(BF16), b.astype(BF16))


def _bdot_nt(a, b):
    return _dot_nt(a.astype(BF16), b.astype(BF16))


def _bdot_tn(a, b):
    return _dot_tn(a.astype(BF16), b.astype(BF16))


def _split2(a):
    hi = a.astype(BF16)
    return hi, (a - hi.astype(F32)).astype(BF16)


def _dot3(a, b, dot=_dot):
    ah, al = _split2(a)
    bh, bl = _split2(b)
    return dot(ah, bh) + dot(al, bh) + dot(ah, bl)


def _mask_dot(m, b):
    hi = b.astype(BF16)
    r = b - hi.astype(F32)
    mid = r.astype(BF16)
    lo = (r - mid.astype(F32)).astype(BF16)
    return _dot(m, hi) + _dot(m, mid) + _dot(m, lo)


def _sigmoid(x):
    return 1.0 / (1.0 + jnp.exp(-x))


def _silu(x):
    return x * _sigmoid(x)


def _softplus(x):
    return jnp.maximum(x, 0.0) + jnp.log(1.0 + jnp.exp(-jnp.abs(x)))


def _rms(x, w):
    return x * lax.rsqrt(jnp.mean(x * x, axis=-1, keepdims=True) + EPS) * w


O_BA = C_Z + H_A * DV_A
O_QB = O_BA + 2 * H_A
O_LR = O_QB + 2 * H_B * DK_B + 2 * H_B * DV_B
O_GA = O_LR + GATE_RANK
W_ROWS = 128


def _wprep_kernel(w_ref, big_ref, small_ref):
    w = w_ref[...]
    big_ref[...] = jnp.concatenate([w[:, :O_BA], w[:, O_QB:O_LR], w[:, O_GA:]], axis=1).astype(BF16)
    pad = jnp.zeros((w.shape[0], LANES - 2 * H_A - GATE_RANK), F32)
    small_ref[...] = jnp.concatenate([w[:, O_BA:O_QB], w[:, O_LR:O_GA], pad], axis=1)


def _wprep(w_in):
    depth, d, ncols = w_in.shape
    return pl.pallas_call(
        _wprep_kernel,
        grid=(depth, d // W_ROWS),
        in_specs=[pl.BlockSpec((None, W_ROWS, ncols), lambda l, i: (l, i, 0))],
        out_specs=[
            pl.BlockSpec((None, W_ROWS, C_BIG), lambda l, i: (l, i, 0)),
            pl.BlockSpec((None, W_ROWS, LANES), lambda l, i: (l, i, 0)),
        ],
        out_shape=[
            jax.ShapeDtypeStruct((depth, d, C_BIG), BF16),
            jax.ShapeDtypeStruct((depth, d, LANES), F32),
        ],
        compiler_params=pltpu.CompilerParams(
            dimension_semantics=("parallel", "parallel"), vmem_limit_bytes=VMEM_LIMIT),
        name="wprep",
    )(w_in.astype(F32))

def _inproj_kernel(x_ref, nw_ref, wb_ref, ws_ref, big_ref, small_ref, xn_ref):
    @pl.when(pl.program_id(1) == 0)
    def _():
        xn = _rms(x_ref[...], nw_ref[...])
        xn_ref[...] = xn.astype(BF16)
        small_ref[...] = _dot3(xn, ws_ref[...])

    big_ref[...] = _dot(xn_ref[...], wb_ref[...])


def _inproj(x, nw, wbig, wsmall, layer, tm, tn):
    n = x.shape[0]
    return pl.pallas_call(
        _inproj_kernel,
        grid=(n // tm, C_BIG // tn),
        in_specs=[
            pl.BlockSpec((tm, D_MODEL), lambda i, j: (i, 0)),
            pl.BlockSpec((1, D_MODEL), lambda i, j: (0, 0)),
            pl.BlockSpec((None, D_MODEL, tn), lambda i, j: (layer, 0, j)),
            pl.BlockSpec((None, D_MODEL, LANES), lambda i, j: (layer, 0, 0)),
        ],
        out_specs=[
            pl.BlockSpec((tm, tn), lambda i, j: (i, j)),
            pl.BlockSpec((tm, LANES), lambda i, j: (i, 0)),
        ],
        out_shape=[
            jax.ShapeDtypeStruct((n, C_BIG), F32),
            jax.ShapeDtypeStruct((n, LANES), F32),
        ],
        scratch_shapes=[pltpu.VMEM((tm, D_MODEL), BF16)],
        compiler_params=pltpu.CompilerParams(
            dimension_semantics=("parallel", "arbitrary"), vmem_limit_bytes=VMEM_LIMIT),
        name="inproj",
    )(x, nw, wbig, wsmall)


def _outproj_kernel(oa_ref, ob_ref, ga_ref, gb_ref, x_ref, wa_ref, wb_ref, wo_ref, out_ref):
    ya = _dot(oa_ref[...].astype(BF16), wa_ref[...])
    yb = _dot(ob_ref[...].astype(BF16), wb_ref[...])
    merged = _sigmoid(ga_ref[...]) * ya + _sigmoid(gb_ref[...]) * yb
    out_ref[...] = x_ref[...] + _dot(merged.astype(BF16), wo_ref[...])


def _outproj(oa, ob, big, x, wa, wb, wo, layer, tm):
    wspec = pl.BlockSpec((None, D_MODEL, D_MODEL), lambda i: (layer, 0, 0))
    n = x.shape[0]
    row = lambda i: (i, 0)
    return pl.pallas_call(
        _outproj_kernel,
        grid=(n // tm,),
        in_specs=[
            pl.BlockSpec((tm, D_MODEL), row),
            pl.BlockSpec((tm, D_MODEL), row),
            pl.BlockSpec((tm, D_MODEL), lambda i: (i, C_GA // D_MODEL)),
            pl.BlockSpec((tm, D_MODEL), lambda i: (i, C_GB // D_MODEL)),
            pl.BlockSpec((tm, D_MODEL), row),
            wspec,
            wspec,
            wspec,
        ],
        out_specs=pl.BlockSpec((tm, D_MODEL), row),
        out_shape=jax.ShapeDtypeStruct((n, D_MODEL), F32),
        compiler_params=pltpu.CompilerParams(
            dimension_semantics=("parallel",), vmem_limit_bytes=VMEM_LIMIT),
        name="outproj",
    )(oa, ob, big, big, x, wa, wb, wo)


def _route(logits):
    lane = lax.broadcasted_iota(jnp.int32, logits.shape, 1)
    neg = jnp.float32(-jnp.inf)
    big = jnp.int32(1 << 20)
    gmask = (lane >= N_EXPERTS) & (lane < N_EXPERTS + N_GROUPS)
    gl = jnp.where(gmask, logits, neg)
    gmax = jnp.max(gl, axis=-1, keepdims=True)
    gsel = jnp.min(jnp.where(gl == gmax, lane - N_EXPERTS, big), axis=-1, keepdims=True)
    pgsel = 1.0 / jnp.sum(jnp.where(gmask, jnp.exp(gl - gmax), 0.0), axis=-1, keepdims=True)
    lo = gsel * EXPERTS_PER_GROUP
    emask = (lane >= lo) & (lane < lo + EXPERTS_PER_GROUP)
    el = jnp.where(emask, logits, neg)
    m1 = jnp.max(el, axis=-1, keepdims=True)
    i1 = jnp.min(jnp.where(el == m1, lane, big), axis=-1, keepdims=True)
    el2 = jnp.where(lane == i1, neg, el)
    m2 = jnp.max(el2, axis=-1, keepdims=True)
    i2 = jnp.min(jnp.where(el2 == m2, lane, big), axis=-1, keepdims=True)
    r = jnp.exp(m2 - m1)
    w1 = pgsel / (1.0 + r)
    w2 = pgsel * r / (1.0 + r)
    return i1, i2, w1, w2


def _router_kernel(x_ref, nw_ref, wr_ref, br_ref, xn_ref, idx_ref, wt_ref, cnt_ref):
    tm = x_ref.shape[0]
    xn = _rms(x_ref[...], nw_ref[...])
    xn_ref[...] = xn
    i1, i2, w1, w2 = _route(_dot3(xn, wr_ref[...]) + br_ref[...])
    lane = lax.broadcasted_iota(jnp.int32, (tm, LANES), 1)
    onehot = jnp.where((lane == i1) | (lane == i2), 1.0, 0.0)
    r = lax.broadcasted_iota(jnp.int32, (tm, tm), 0)
    c = lax.broadcasted_iota(jnp.int32, (tm, tm), 1)
    earlier = jnp.where(c < r, 1.0, 0.0).astype(BF16)
    cum = _dot(earlier, onehot.astype(BF16))
    rank1 = jnp.sum(jnp.where(lane == i1, cum, 0.0), axis=-1, keepdims=True).astype(jnp.int32)
    rank2 = jnp.sum(jnp.where(lane == i2, cum, 0.0), axis=-1, keepdims=True).astype(jnp.int32)
    idx_ref[...] = jnp.where(lane == 0, i1, jnp.where(lane == 1, i2, jnp.where(
        lane == 2, rank1, jnp.where(lane == 3, rank2, 0))))
    wt_ref[...] = jnp.where(lane == 0, w1, jnp.where(lane == 1, w2, 0.0))
    counts = jnp.sum(onehot, axis=0, keepdims=True).astype(jnp.int32)
    cnt_ref[...] = jnp.broadcast_to(counts, cnt_ref.shape)


def _router(x, nw, wr, br, tm):
    n = x.shape[0]
    row = lambda i: (i, 0)
    full = lambda i: (0, 0)
    return pl.pallas_call(
        _router_kernel,
        grid=(n // tm,),
        in_specs=[
            pl.BlockSpec((tm, D_MODEL), row),
            pl.BlockSpec((1, D_MODEL), full),
            pl.BlockSpec((D_MODEL, LANES), full),
            pl.BlockSpec((1, LANES), full),
        ],
        out_specs=[
            pl.BlockSpec((tm, D_MODEL), row),
            pl.BlockSpec((tm, LANES), row),
            pl.BlockSpec((tm, LANES), row),
            pl.BlockSpec((None, SUBLANES, LANES), lambda i: (i, 0, 0)),
        ],
        out_shape=[
            jax.ShapeDtypeStruct((n, D_MODEL), F32),
            jax.ShapeDtypeStruct((n, LANES), jnp.int32),
            jax.ShapeDtypeStruct((n, LANES), F32),
            jax.ShapeDtypeStruct((n // tm, SUBLANES, LANES), jnp.int32),
        ],
        compiler_params=pltpu.CompilerParams(
            dimension_semantics=("parallel",), vmem_limit_bytes=VMEM_LIMIT),
        name="router",
    )(x, nw, wr, br)


def _moe_plan(idx, cnt, tm_tok, tm_exp):
    n = idx.shape[0]
    nt = 2 * n // tm_exp + N_EXPERTS
    counts = cnt[:, 0, :N_EXPERTS]
    cpad = (jnp.sum(counts, axis=0) + tm_exp - 1) // tm_exp * tm_exp
    ends = jnp.cumsum(cpad)
    base = (ends - cpad)[None] + jnp.cumsum(counts, axis=0) - counts
    base_tok = jnp.repeat(base, tm_tok, axis=0)[:, None, :]
    hit = idx[:, :2, None] == jnp.arange(N_EXPERTS, dtype=jnp.int32)
    dest = (jnp.sum(jnp.where(hit, base_tok, 0), axis=-1) + idx[:, 2:4]).astype(jnp.int32)
    tile_start = jnp.arange(nt, dtype=jnp.int32) * tm_exp
    tile_e = jnp.minimum(jnp.sum(tile_start[:, None] >= ends[None, :], axis=1), N_EXPERTS - 1).astype(jnp.int32)
    tile_valid = (tile_start < ends[-1]).astype(jnp.int32)
    return dest, tile_e, tile_valid


def _slot_major(dest, tm):
    nt = dest.shape[0] // tm
    return jnp.swapaxes(dest.reshape(nt, tm, 2), 1, 2).reshape(nt, 1, 2 * tm)


def _dispatch_kernel(d_ref, xn_ref, xs_in_hbm, xs_hbm, sem):
    del xs_in_hbm
    tm = xn_ref.shape[0]

    def copy(i, j, dst_row):
        return pltpu.make_async_copy(xn_ref.at[pl.ds(i, 1), :], xs_hbm.at[pl.ds(dst_row, 1), :], sem.at[j])

    def start(i, carry):
        for j in range(2):
            copy(i, j, d_ref[0, j * tm + i]).start(priority=j)
        return carry

    def wait(i, carry):
        for j in range(2):
            copy(i, j, 0).wait()
        return carry

    lax.fori_loop(0, tm, start, 0, unroll=4)
    lax.fori_loop(0, tm, wait, 0, unroll=4)


def _dispatch(xn, dest, xs_buf, tm):
    n = xn.shape[0]
    nrows = xs_buf.shape[0]
    return pl.pallas_call(
        _dispatch_kernel,
        grid=(n // tm,),
        in_specs=[
            pl.BlockSpec((None, 1, 2 * tm), lambda t: (t, 0, 0), memory_space=pltpu.SMEM),
            pl.BlockSpec((tm, D_MODEL), lambda t: (t, 0)),
            pl.BlockSpec(memory_space=pl.ANY),
        ],
        out_specs=pl.BlockSpec(memory_space=pl.ANY),
        out_shape=jax.ShapeDtypeStruct((nrows, D_MODEL), F32),
        scratch_shapes=[pltpu.SemaphoreType.DMA((2,))],
        input_output_aliases={2: 0},
        compiler_params=pltpu.CompilerParams(
            dimension_semantics=("arbitrary",), vmem_limit_bytes=VMEM_LIMIT),
        name="dispatch",
    )(_slot_major(dest, tm), xn, xs_buf)


def _row_gather(idx_ref, src_hbm, buf, sem, slot, nrows):
    def body(r, carry):
        for j in range(2):
            rr = 2 * r + j
            pltpu.make_async_copy(src_hbm.at[pl.ds(idx_ref[0, rr], 1), :],
                                  buf.at[slot, pl.ds(rr, 1), :], sem.at[slot]).start(priority=j)
        return carry
    lax.fori_loop(0, nrows // 2, body, 0, unroll=4)


def _row_gather_wait(src_hbm, buf, sem, slot, nrows):
    def body(r, carry):
        pltpu.make_async_copy(src_hbm.at[pl.ds(0, 1), :],
                              buf.at[slot, pl.ds(r, 1), :], sem.at[slot]).wait()
        return carry
    lax.fori_loop(0, nrows, body, 0, unroll=8)


def _expert_kernel(te_ref, tv_ref, xs_ref, wg_ref, wu_ref, wd_ref, y_ref):
    t = pl.program_id(0)

    @pl.when(tv_ref[t] == 1)
    def _():
        xb = xs_ref[...].astype(BF16)
        a = _silu(_dot(xb, wg_ref[...])) * _dot(xb, wu_ref[...])
        y_ref[...] = _dot(a.astype(BF16), wd_ref[...])

    @pl.when(tv_ref[t] == 0)
    def _():
        y_ref[...] = jnp.zeros_like(y_ref)


def _experts(xs, tile_e, tile_valid, wg, wu, wd, layer, tm):
    w_in_spec = pl.BlockSpec((None, None, D_MODEL, D_FF_E), lambda t, te, tv: (layer, te[t], 0, 0))
    nt = xs.shape[0] // tm
    return pl.pallas_call(
        _expert_kernel,
        grid_spec=pltpu.PrefetchScalarGridSpec(
            num_scalar_prefetch=2,
            grid=(nt,),
            in_specs=[
                pl.BlockSpec((tm, D_MODEL), lambda t, te, tv: (t, 0)),
                w_in_spec,
                w_in_spec,
                pl.BlockSpec((None, None, D_FF_E, D_MODEL), lambda t, te, tv: (layer, te[t], 0, 0)),
            ],
            out_specs=pl.BlockSpec((tm, D_MODEL), lambda t, te, tv: (t, 0)),
        ),
        out_shape=jax.ShapeDtypeStruct((nt * tm, D_MODEL), F32),
        compiler_params=pltpu.CompilerParams(
            dimension_semantics=("arbitrary",), vmem_limit_bytes=VMEM_LIMIT),
        name="experts",
    )(tile_e, tile_valid, xs, wg, wu, wd)


def _combine_kernel(split_tile, d_ref, d_next_ref, y_hbm, x_ref, wt_ref, fnw_ref, *rest):
    out_refs, (ybuf, sem) = rest[:-2], rest[-2:]
    t = pl.program_id(0)
    nt = pl.num_programs(0)
    tm = x_ref.shape[0]
    slot = t & 1

    @pl.when(t == 0)
    def _():
        _row_gather(d_ref, y_hbm, ybuf, sem, 0, 2 * tm)

    @pl.when(t + 1 < nt)
    def _():
        _row_gather(d_next_ref, y_hbm, ybuf, sem, 1 - slot, 2 * tm)

    _row_gather_wait(y_hbm, ybuf, sem, slot, 2 * tm)
    wt = wt_ref[...]
    out = x_ref[...] + wt[:, 0:1] * ybuf[slot, 0:tm, :] + wt[:, 1:2] * ybuf[slot, tm:2 * tm, :]
    if split_tile is None:
        out_refs[0][...] = out
    else:
        res = _rms(out, fnw_ref[...])

        @pl.when(t < split_tile)
        def _():
            out_refs[0][...] = res

        @pl.when(t >= split_tile)
        def _():
            out_refs[1][...] = res


def _combine(x, y, dest, wt, fnw, split_rows, tm):
    n = x.shape[0]
    nt = n // tm
    d = _slot_major(dest, tm)
    if split_rows is None:
        split_tile = None
        out_specs = pl.BlockSpec((tm, D_MODEL), lambda t: (t, 0))
        out_shape = jax.ShapeDtypeStruct((n, D_MODEL), F32)
    else:
        split_tile = split_rows // tm
        assert split_tile * tm == split_rows and 0 < split_tile < nt
        out_specs = [pl.BlockSpec((tm, D_MODEL), lambda t: (jnp.minimum(t, split_tile - 1), 0)),
                     pl.BlockSpec((tm, D_MODEL), lambda t: (jnp.maximum(t - split_tile, 0), 0))]
        out_shape = [jax.ShapeDtypeStruct((split_rows, D_MODEL), F32),
                     jax.ShapeDtypeStruct((n - split_rows, D_MODEL), F32)]
    return pl.pallas_call(
        functools.partial(_combine_kernel, split_tile),
        grid=(nt,),
        in_specs=[
            pl.BlockSpec((None, 1, 2 * tm), lambda t: (t, 0, 0), memory_space=pltpu.SMEM),
            pl.BlockSpec((None, 1, 2 * tm), lambda t: (jnp.minimum(t + 1, nt - 1), 0, 0),
                         memory_space=pltpu.SMEM),
            pl.BlockSpec(memory_space=pl.ANY),
            pl.BlockSpec((tm, D_MODEL), lambda t: (t, 0)),
            pl.BlockSpec((tm, LANES), lambda t: (t, 0)),
            pl.BlockSpec((1, D_MODEL), lambda t: (0, 0)),
        ],
        out_specs=out_specs,
        out_shape=out_shape,
        scratch_shapes=[pltpu.VMEM((2, 2 * tm, D_MODEL), F32), pltpu.SemaphoreType.DMA((2,))],
        compiler_params=pltpu.CompilerParams(
            dimension_semantics=("arbitrary",), vmem_limit_bytes=VMEM_LIMIT),
        name="combine",
    )(d, d, y, x, wt, fnw)


def _moe_rows(n, tm_exp):
    return (2 * n // tm_exp + N_EXPERTS) * tm_exp


def _moe(x, nw, wr, br, wg, wu, wd, layer, fnw, xs_buf, split_rows, tm_tok, tm_exp):
    xn, idx, wt, cnt = _router(x, nw, wr, br, tm_tok)
    dest, tile_e, tile_valid = _moe_plan(idx, cnt, tm_tok, tm_exp)
    xs = _dispatch(xn, dest, xs_buf, tm_tok)
    y = _experts(xs, tile_e, tile_valid, wg, wu, wd, layer, tm_exp)
    return _combine(x, y, dest, wt, fnw, split_rows, tm_exp), xs


def _seq_masks(n, lseq):
    r = lax.broadcasted_iota(jnp.int32, (n, n), 0)
    c = lax.broadcasted_iota(jnp.int32, (n, n), 1)
    sh = lseq.bit_length() - 1
    same = (r >> sh) == (c >> sh)
    return same & (r >= c), same & (r > c), r == c


def _valid_rows(n, lseq, t_lo, t_hi):
    row = lax.broadcasted_iota(jnp.int32, (n, 1), 0)
    t = (row & (lseq - 1)) + pl.program_id(1) * lseq
    return (t >= t_lo) & (t < t_hi)


def _pad_rows(a, n):
    return jnp.concatenate([a, jnp.zeros((n - a.shape[0], a.shape[1]), a.dtype)], axis=0)


def _gdn_kernel(Lseq, NS, t_lo, t_hi, conv_row0,
                qkv_ref, z_ref, sm_ref, cin_ref, sin_ref, cw_ref, hp_ref, na_ref, obuf_ref, sbuf_ref,
                o_ref, cout_ref, so_ref, ext_ref):
    del obuf_ref, sbuf_ref
    R = NS * Lseq
    R2 = 2 * R
    npairs = H_A // 2

    @pl.when(pl.program_id(1) == 0)
    def _():
        ext_ref[:, 0:SUBLANES, :] = cin_ref[...]
        so_ref[...] = sin_ref[...]

    valid = _valid_rows(R, Lseq, t_lo, t_hi)
    valid_l = _valid_rows(Lseq, Lseq, t_lo, t_hi)
    for s in range(NS):
        ext_ref[s, SUBLANES:SUBLANES + Lseq, :] = jnp.where(
            valid_l, qkv_ref[s * Lseq:(s + 1) * Lseq, :], 0.0)

    def conv_act(col0):
        pieces = []
        for s in range(NS):
            acc = None
            for i in range(CONV_W):
                r0 = SUBLANES - (CONV_W - 1) + i
                term = cw_ref[i:i + 1, col0:col0 + LANES] * ext_ref[s, r0:r0 + Lseq, col0:col0 + LANES]
                acc = term if acc is None else acc + term
            pieces.append(acc)
        return _silu(pieces[0] if NS == 1 else jnp.concatenate(pieces, axis=0))

    incl, _, _ = _seq_masks(R, Lseq)
    incl2, strict2, diag2 = _seq_masks(R2, Lseq)
    tri = jnp.where(incl, 1.0, 0.0).astype(BF16)
    eye2 = jnp.where(diag2, 1.0, 0.0)

    sm = sm_ref[...]
    beta_all = jnp.where(valid, _sigmoid(sm), 0.0)
    g_all = jnp.where(valid, -jnp.exp(hp_ref[0:1, :]) * _softplus(sm + hp_ref[1:2, :]), 0.0)
    G_all = _mask_dot(tri, g_all)
    GT = jnp.transpose(_pad_rows(G_all, LANES))

    q2, k2, v2, beta2, Gc2, Gr2 = [], [], [], [], [], []
    for p in range(npairs):
        qs, ks, vs = [], [], []
        for h in (2 * p, 2 * p + 1):
            q = conv_act(h * DK_A)
            k = conv_act(H_A * DK_A + h * DK_A)
            qs.append(q * lax.rsqrt(jnp.sum(q * q, axis=-1, keepdims=True) + EPS) * (DK_A ** -0.5))
            ks.append(jnp.where(valid, k * lax.rsqrt(jnp.sum(k * k, axis=-1, keepdims=True) + EPS), 0.0))
            vs.append(conv_act(2 * H_A * DK_A + h * DV_A))
        q2.append(jnp.concatenate(qs, axis=0))
        k2.append(jnp.concatenate(ks, axis=0))
        v2.append(jnp.concatenate(vs, axis=0))
        h0 = 2 * p
        beta2.append(jnp.concatenate([beta_all[:, S_BETA + h0:S_BETA + h0 + 1],
                                      beta_all[:, S_BETA + h0 + 1:S_BETA + h0 + 2]], axis=0))
        Gc2.append(jnp.concatenate([G_all[:, S_DECAY + h0:S_DECAY + h0 + 1],
                                    G_all[:, S_DECAY + h0 + 1:S_DECAY + h0 + 2]], axis=0))
        Gr2.append(jnp.concatenate([GT[S_DECAY + h0:S_DECAY + h0 + 1, :R],
                                    GT[S_DECAY + h0 + 1:S_DECAY + h0 + 2, :R]], axis=1))

    rng = range(npairs)
    kk = [_bdot_nt(k2[p], k2[p]) for p in rng]
    qk = [_bdot_nt(q2[p], k2[p]) for p in rng]
    dec = [jnp.where(incl2, jnp.exp(jnp.where(incl2, Gc2[p] - Gr2[p], 0.0)), 0.0) for p in rng]
    Bp = [-jnp.where(strict2, beta2[p] * kk[p] * dec[p], 0.0) for p in rng]
    P = [eye2 + Bp[p] for p in rng]
    n = 2
    while n < Lseq:
        Bp = [_dot3(b, b) for b in Bp]
        P = [pp + _dot3(pp, b) for pp, b in zip(P, Bp)]
        n *= 2
    eG = [jnp.exp(Gc2[p]) for p in rng]
    rhs = [jnp.concatenate([beta2[p] * v2[p], (beta2[p] * eG[p]) * k2[p]], axis=1) for p in rng]
    sol = [_dot3(P[p], rhs[p]) for p in rng]
    qk = [qk[p] * dec[p] for p in rng]
    qe = [q2[p] * eG[p] for p in rng]

    items = [(p, j, s) for p in rng for j in range(2) for s in range(NS)]

    def rows(j, s):
        return slice(j * R + s * Lseq, j * R + (s + 1) * Lseq)

    S = {it: so_ref[it[2], 2 * it[0] + it[1]] for it in items}
    both = {}
    for p, j, s in items:
        r = rows(j, s)
        both[(p, j, s)] = _bdot(jnp.concatenate([sol[p][r, DV_A:], qe[p][r]], axis=0), S[(p, j, s)])
    vnew = {}
    for p, j, s in items:
        vnew[(p, j, s)] = sol[p][rows(j, s), :DV_A] - both[(p, j, s)][:Lseq]
    for p, j, s in items:
        r = rows(j, s)
        last = j * R + (s + 1) * Lseq - 1
        GL = Gc2[p][last:last + 1]
        kd = k2[p][r] * jnp.exp(GL - Gc2[p][r])
        so_ref[s, 2 * p + j] = S[(p, j, s)] * jnp.exp(GL) + _bdot_tn(kd, vnew[(p, j, s)])
    for p in rng:
        order = [(p, j, s) for j in range(2) for s in range(NS)]
        vnew2 = jnp.concatenate([vnew[it] for it in order], axis=0)
        o2 = jnp.concatenate([both[it][Lseq:] for it in order], axis=0) + _bdot(qk[p], vnew2)
        for j in range(2):
            h = 2 * p + j
            zh = z_ref[:, h * DV_A:(h + 1) * DV_A]
            o_ref[:, h * DV_A:(h + 1) * DV_A] = _rms(o2[j * R:(j + 1) * R], na_ref[...]) * _silu(zh)

    for s in range(NS):
        cout_ref[s] = ext_ref[s, conv_row0:conv_row0 + SUBLANES, :]
        ext_ref[s, 0:SUBLANES, :] = ext_ref[s, Lseq:Lseq + SUBLANES, :]


def _gdn(big, small, cin, sin, layer_in, cw, hp, na, obuf, sbuf, layer, *,
         nsteps, Lseq, NS, nchunks, row_block, t_lo, t_hi):
    assert NS * Lseq == ROWS
    conv_row0 = t_hi - (nchunks - 1) * Lseq
    kern = functools.partial(_gdn_kernel, Lseq, NS, t_lo, t_hi, conv_row0)
    rb = row_block
    return pl.pallas_call(
        kern,
        grid=(nsteps, nchunks),
        in_specs=[
            pl.BlockSpec((ROWS, CONV_CH), lambda b, c: (rb(b, c), 0)),
            pl.BlockSpec((ROWS, H_A * DV_A), lambda b, c: (rb(b, c), C_Z // (H_A * DV_A))),
            pl.BlockSpec((ROWS, LANES), lambda b, c: (rb(b, c), 0)),
            pl.BlockSpec((None, NS, SUBLANES, CONV_CH), lambda b, c: (layer_in, b, 0, 0)),
            pl.BlockSpec((None, NS, H_A, DK_A, DV_A), lambda b, c: (layer_in, b, 0, 0, 0)),
            pl.BlockSpec((SUBLANES, CONV_CH), lambda b, c: (0, 0)),
            pl.BlockSpec((SUBLANES, LANES), lambda b, c: (0, 0)),
            pl.BlockSpec((1, DV_A), lambda b, c: (0, 0)),
            pl.BlockSpec(memory_space=pl.ANY),
            pl.BlockSpec(memory_space=pl.ANY),
        ],
        out_specs=[
            pl.BlockSpec((ROWS, H_A * DV_A), lambda b, c: (rb(b, c), 0)),
            pl.BlockSpec((NS, SUBLANES, CONV_CH), lambda b, c: (b, 0, 0)),
            pl.BlockSpec((None, NS, H_A, DK_A, DV_A), lambda b, c: (layer, b, 0, 0, 0)),
        ],
        out_shape=[
            jax.ShapeDtypeStruct(obuf.shape, F32),
            jax.ShapeDtypeStruct((nsteps * NS, SUBLANES, CONV_CH), F32),
            jax.ShapeDtypeStruct(sbuf.shape, F32),
        ],
        scratch_shapes=[pltpu.VMEM((NS, SUBLANES + Lseq, CONV_CH), F32)],
        input_output_aliases={8: 0, 9: 2},
        compiler_params=pltpu.CompilerParams(
            dimension_semantics=("parallel", "arbitrary"), vmem_limit_bytes=VMEM_LIMIT),
        name="gdn_L%d" % Lseq,
    )(big, big, small, cin, sin, cw, hp, na, obuf, sbuf)


def _gla_kernel(Lseq, NS, SB, t_lo, t_hi,
                qb_ref, kb_ref, vb_ref, rb_ref, sm_ref, sin_ref, wup_ref, bup_ref, nb_ref, obuf_ref,
                sbuf_ref, o_ref, so_ref):
    del obuf_ref, sbuf_ref
    R = NS * Lseq
    heads = range(H_B)

    @pl.when(pl.program_id(1) == 0)
    def _():
        so_ref[...] = sin_ref[...]

    nb = Lseq // SB
    assert nb == 1 or NS == 1
    n_off = SB * nb * (nb - 1) // 2
    sb_shift = SB.bit_length() - 1

    valid = _valid_rows(R, Lseq, t_lo, t_hi)
    r = lax.broadcasted_iota(jnp.int32, (R, R), 0)
    c = lax.broadcasted_iota(jnp.int32, (R, R), 1)
    same = (r | (Lseq - 1)) == (c | (Lseq - 1))
    m_row = jnp.where(same & (c <= r), 1.0, 0.0)
    m_first = jnp.where(same & (c <= (r & ~(SB - 1))), 1.0, 0.0)
    m_last = jnp.where(same, 1.0, 0.0)
    masks = jnp.concatenate([m_row, m_first, m_last], axis=0).astype(BF16)

    pre = _dot3(sm_ref[...], wup_ref[...]) + bup_ref[...]
    lg = jnp.where(valid, -_softplus(-pre) / GATE_TAU, 0.0)
    cums = _mask_dot(masks, lg)
    Bc_all, b0_all, BL_all = cums[:R], cums[R:2 * R], cums[2 * R:]

    re = lax.broadcasted_iota(jnp.int32, (R, n_off + R), 0)
    ce = lax.broadcasted_iota(jnp.int32, (R, n_off + R), 1)
    rblk = (re & (Lseq - 1)) >> sb_shift
    cd = ce - n_off
    see = ((ce >= (SB // 2) * rblk * (rblk - 1)) & (ce < (SB // 2) * rblk * (rblk + 1))) | (
        (cd >= 0) & ((cd >> sb_shift) == (re >> sb_shift)) & (cd <= re))

    def hs(a, h, w):
        return a[:, h * w:(h + 1) * w]

    q = [hs(qb_ref, h, DK_B) * (DK_B ** -0.5) for h in heads]
    k = [jnp.where(valid, hs(kb_ref, h, DK_B), 0.0) for h in heads]
    v = [hs(vb_ref, h, DV_B) for h in heads]
    Bc = [hs(Bc_all, h, DK_B) for h in heads]
    b0 = [hs(b0_all, h, DK_B) for h in heads]
    BL = [hs(BL_all, h, DK_B) for h in heads]
    qs = [q[h] * jnp.exp(Bc[h] - b0[h]) for h in heads]
    kx, vx = [], []
    for h in heads:
        kparts = [k[h][:i * SB] * jnp.exp(Bc[h][i * SB:i * SB + 1] - Bc[h][:i * SB]) for i in range(1, nb)]
        kparts.append(k[h] * jnp.exp(b0[h] - Bc[h]))
        vparts = [v[h][:i * SB] for i in range(1, nb)] + [v[h]]
        kx.append(kparts[0] if nb == 1 else jnp.concatenate(kparts, axis=0))
        vx.append(vparts[0] if nb == 1 else jnp.concatenate(vparts, axis=0))
    Ax = [jnp.where(see, _dot3(qs[h], kx[h], _dot_nt), 0.0) for h in heads]
    o_intra = [_bdot(Ax[h], vx[h]) for h in heads]
    qe = [q[h] * jnp.exp(Bc[h]) for h in heads]
    kd = [k[h] * jnp.exp(BL[h] - Bc[h]) for h in heads]
    ET = [jnp.transpose(_pad_rows(jnp.exp(BL[h]), LANES)) for h in heads]

    items = [(h, s) for h in heads for s in range(NS)]
    S = {it: so_ref[it[1], it[0]] for it in items}
    o_inter = {}
    for h, s in items:
        o_inter[(h, s)] = _bdot(qe[h][s * Lseq:(s + 1) * Lseq], S[(h, s)])
    for h, s in items:
        rows = slice(s * Lseq, (s + 1) * Lseq)
        so_ref[s, h] = S[(h, s)] * ET[h][:, s * Lseq:s * Lseq + 1] + _bdot_tn(kd[h][rows], v[h][rows])
    for h in heads:
        parts = [o_inter[(h, s)] for s in range(NS)]
        o = (parts[0] if NS == 1 else jnp.concatenate(parts, axis=0)) + o_intra[h]
        rh = hs(rb_ref, h, DV_B)
        o_ref[:, h * DV_B:(h + 1) * DV_B] = _rms(o, nb_ref[...]) * _silu(rh)


def _gla(big, small, sin, layer_in, wup, bup, nb, obuf, sbuf, layer, *,
         nsteps, Lseq, NS, SB, nchunks, row_block, t_lo, t_hi):
    assert NS * Lseq == ROWS
    kern = functools.partial(_gla_kernel, Lseq, NS, SB, t_lo, t_hi)
    r = row_block
    return pl.pallas_call(
        kern,
        grid=(nsteps, nchunks),
        in_specs=[
            pl.BlockSpec((ROWS, H_B * DK_B), lambda b, c: (r(b, c), C_QB // (H_B * DK_B))),
            pl.BlockSpec((ROWS, H_B * DK_B), lambda b, c: (r(b, c), C_KB // (H_B * DK_B))),
            pl.BlockSpec((ROWS, H_B * DV_B), lambda b, c: (r(b, c), C_VB // (H_B * DV_B))),
            pl.BlockSpec((ROWS, H_B * DV_B), lambda b, c: (r(b, c), C_RB // (H_B * DV_B))),
            pl.BlockSpec((ROWS, LANES), lambda b, c: (r(b, c), 0)),
            pl.BlockSpec((None, NS, H_B, DK_B, DV_B), lambda b, c: (layer_in, b, 0, 0, 0)),
            pl.BlockSpec((LANES, H_B * DK_B), lambda b, c: (0, 0)),
            pl.BlockSpec((1, H_B * DK_B), lambda b, c: (0, 0)),
            pl.BlockSpec((1, DV_B), lambda b, c: (0, 0)),
            pl.BlockSpec(memory_space=pl.ANY),
            pl.BlockSpec(memory_space=pl.ANY),
        ],
        out_specs=[
            pl.BlockSpec((ROWS, H_B * DV_B), lambda b, c: (r(b, c), 0)),
            pl.BlockSpec((None, NS, H_B, DK_B, DV_B), lambda b, c: (layer, b, 0, 0, 0)),
        ],
        out_shape=[
            jax.ShapeDtypeStruct(obuf.shape, F32),
            jax.ShapeDtypeStruct(sbuf.shape, F32),
        ],
        input_output_aliases={9: 0, 10: 1},
        compiler_params=pltpu.CompilerParams(
            dimension_semantics=("parallel", "arbitrary"), vmem_limit_bytes=VMEM_LIMIT),
        name="gla_L%d" % Lseq,
    )(big, big, big, big, small, sin, wup, bup, nb, obuf, sbuf)


def kernel(x_prompt, x_sample, state_delta, state_conv, state_gla, meta_tokens, norm_mix, w_in, conv_w, a_log, dt_bias, norm_a, gla_gate_up, gla_gate_bias, norm_b, w_proj_a, w_proj_b, w_out, norm_ffn, router_group_w, router_group_b, router_expert_w, router_expert_b, expert_w_gate, expert_w_up, expert_w_down, norm_final):
    Bp, Tp, D = x_prompt.shape
    Bs, Ts, _ = x_sample.shape
    depth = w_in.shape[0]
    assert N_META <= CHUNK and Tp % CHUNK == 0 and CHUNK == ROWS
    pad_p = CHUNK - N_META
    Tpp = CHUNK + Tp
    ncp = Tpp // CHUNK
    n_prompt = Bp * Tp
    n_head = Bp * CHUNK
    ns_rows = Bs * SAMPLE_L
    n = n_prompt + n_head + ns_rows
    seq_per_step = ROWS // SAMPLE_L
    dt = x_prompt.dtype
    head = jnp.concatenate([
        jnp.zeros((Bp, pad_p, D), dt),
        jnp.broadcast_to(meta_tokens.astype(dt)[None], (Bp, N_META, D))], axis=1).reshape(n_head, D)
    xs = jnp.pad(x_sample, ((0, 0), (0, SAMPLE_L - Ts), (0, 0))).reshape(ns_rows, D)
    x = jnp.concatenate([x_prompt.reshape(n_prompt, D), head, xs.astype(dt)], axis=0).astype(F32)

    tm_big = 1280 if n % 1280 == 0 else 512
    tm = 512

    zeros_conv = jnp.zeros((1, Bp, SUBLANES, CONV_CH), F32)
    zeros_sa = jnp.zeros((1, Bp, H_A, DK_A, DV_A), F32)
    zeros_sb = jnp.zeros((1, Bp, H_B, DK_B, DV_B), F32)
    head_block0 = n_prompt // ROWS
    sample_block0 = (n_prompt + n_head) // ROWS
    prompt_args = dict(
        nsteps=Bp, Lseq=CHUNK, NS=1, nchunks=ncp, t_lo=pad_p, t_hi=Tpp,
        row_block=lambda b, c: jnp.where(c == 0, head_block0 + b, b * (ncp - 1) + c - 1))
    sample_args = dict(
        nsteps=Bs // seq_per_step, Lseq=SAMPLE_L, NS=seq_per_step, nchunks=1, t_lo=0, t_hi=Ts,
        row_block=lambda b, c: sample_block0 + b)

    wbig_all, wsmall_all = _wprep(w_in)
    wg_all, wu_all = expert_w_gate.astype(BF16), expert_w_up.astype(BF16)
    wd_all = expert_w_down.astype(BF16)
    wa_all, wb_all, wo_all = w_proj_a.astype(BF16), w_proj_b.astype(BF16), w_out.astype(BF16)
    oa = jnp.zeros((n, H_A * DV_A), F32)
    ob = jnp.zeros((n, H_B * DV_B), F32)
    xs_buf = jnp.zeros((_moe_rows(n, MOE_TILE), D), F32)
    sa_p = jnp.zeros((depth, Bp, H_A, DK_A, DV_A), F32)
    sb_p = jnp.zeros((depth, Bp, H_B, DK_B, DV_B), F32)
    sa_s = jnp.zeros((depth, Bs, H_A, DK_A, DV_A), F32)
    sb_s = jnp.zeros((depth, Bs, H_B, DK_B, DV_B), F32)
    sa_in, sb_in = state_delta.astype(F32), state_gla.astype(F32)
    cin_s = jnp.pad(state_conv.astype(F32), ((0, 0), (0, 0), (SUBLANES - (CONV_W - 1), 0), (0, 0)))

    conv_p, conv_s = [], []
    for l in range(depth):
        big, small = _inproj(x, norm_mix[l][None].astype(F32), wbig_all, wsmall_all, l, tm_big, 512)

        cw = jnp.pad(conv_w[l].astype(F32), ((0, SUBLANES - CONV_W), (0, 0)))
        hp = jnp.zeros((SUBLANES, LANES), F32)
        hp = hp.at[0, S_DECAY:S_DECAY + H_A].set(a_log[l].astype(F32))
        hp = hp.at[1, S_DECAY:S_DECAY + H_A].set(dt_bias[l].astype(F32))
        na = norm_a[l][None].astype(F32)
        wup = jnp.zeros((LANES, H_B * DK_B), F32).at[S_LR:S_LR + GATE_RANK].set(gla_gate_up[l].astype(F32))
        bup = gla_gate_bias[l][None].astype(F32)
        nb = norm_b[l][None].astype(F32)

        oa, c_p, sa_p = _gdn(big, small, zeros_conv, zeros_sa, 0, cw, hp, na, oa, sa_p, l, **prompt_args)
        oa, c_s, sa_s = _gdn(big, small, cin_s, sa_in, l, cw, hp, na, oa, sa_s, l, **sample_args)
        ob, sb_p = _gla(big, small, zeros_sb, 0, wup, bup, nb, ob, sb_p, l, SB=GLA_SUB, **prompt_args)
        ob, sb_s = _gla(big, small, sb_in, l, wup, bup, nb, ob, sb_s, l, SB=SAMPLE_L, **sample_args)
        x = _outproj(oa, ob, big, x, wa_all, wb_all, wo_all, l, tm)

        wr = jnp.concatenate([router_expert_w[l], router_group_w[l],
                              jnp.zeros((D, LANES - N_EXPERTS - N_GROUPS), F32)], axis=1).astype(F32)
        br = jnp.concatenate([router_expert_b[l], router_group_b[l],
                              jnp.zeros((LANES - N_EXPERTS - N_GROUPS,), F32)])[None].astype(F32)
        x, xs_buf = _moe(x, norm_ffn[l][None].astype(F32), wr, br, wg_all, wu_all, wd_all, l,
                         norm_final[None].astype(F32), xs_buf,
                         n_prompt if l == depth - 1 else None, tm, MOE_TILE)

        conv_p.append(c_p[:, SUBLANES - (CONV_W - 1):])
        conv_s.append(c_s[:, SUBLANES - (CONV_W - 1):])

    y_first, y_rest = x
    y_prompt = y_first.reshape(Bp, Tp, D).astype(dt)
    y_sample = y_rest[n_head:].reshape(Bs, SAMPLE_L, D)[:, :Ts].astype(x_sample.dtype)
    return (y_prompt, y_sample,
            sa_p.astype(dt), jnp.stack(conv_p).astype(dt), sb_p.astype(dt),
            sa_s.astype(state_delta.dtype), jnp.stack(conv_s).astype(state_conv.dtype),
            sb_s.astype(state_gla.dtype))
```

```python
import functools

import jax
import jax.numpy as jnp
from jax import lax
from jax.experimental import pallas as pl
from jax.experimental.pallas import tpu as pltpu

F32 = jnp.float32
BF16 = jnp.bfloat16

D_MODEL = 1024
N_META = 16
CHUNK = 64
EPS = 1e-6
H_A, DK_A, DV_A = 8, 128, 128
CONV_W = 4
CONV_CH = 2 * H_A * DK_A + H_A * DV_A
H_B, DK_B, DV_B = 4, 128, 256
GATE_RANK = 16
GATE_TAU = 16.0
N_GROUPS = 4
EXPERTS_PER_GROUP = 8
N_EXPERTS = N_GROUPS * EXPERTS_PER_GROUP
D_FF_E = 256

LANES = 128
SUBLANES = 8
VMEM_LIMIT = 48 * 1024 * 1024

C_QKV = 0
C_Z = C_QKV + CONV_CH
C_QB = C_Z + H_A * DV_A
C_KB = C_QB + H_B * DK_B
C_VB = C_KB + H_B * DK_B
C_RB = C_VB + H_B * DV_B
C_GA = C_RB + H_B * DV_B
C_GB = C_GA + D_MODEL
C_BIG = C_GB + D_MODEL
S_BETA = 0
S_DECAY = H_A
S_LR = 2 * H_A

SAMPLE_L = SUBLANES
ROWS = CHUNK
GLA_SUB = 16
MOE_TILE = 256


def _dot(a, b, prec=None):
    return jnp.dot(a, b, preferred_element_type=F32, precision=prec)


def _dot_nt(a, b, prec=None):
    return lax.dot_general(a, b, (((1,), (1,)), ((), ())), preferred_element_type=F32, precision=prec)


def _dot_tn(a, b, prec=None):
    return lax.dot_general(a, b, (((0,), (0,)), ((), ())), preferred_element_type=F32, precision=prec)


def _bdot(a, b):
    return _dot(a.astype(BF16), b.astype(BF16))


def _bdot_nt(a, b):
    return _dot_nt(a.astype(BF16), b.astype(BF16))


def _bdot_tn(a, b):
    return _dot_tn(a.astype(BF16), b.astype(BF16))


def _split2(a):
    hi = a.astype(BF16)
    return hi, (a - hi.astype(F32)).astype(BF16)


def _dot3(a, b, dot=_dot):
    ah, al = _split2(a)
    bh, bl = _split2(b)
    return dot(ah, bh) + dot(al, bh) + dot(ah, bl)


def _mask_dot(m, b):
    hi = b.astype(BF16)
    r = b - hi.astype(F32)
    mid = r.astype(BF16)
    lo = (r - mid.astype(F32)).astype(BF16)
    return _dot(m, hi) + _dot(m, mid) + _dot(m, lo)


def _sigmoid(x):
    return 1.0 / (1.0 + jnp.exp(-x))


def _silu(x):
    return x * _sigmoid(x)


def _softplus(x):
    return jnp.maximum(x, 0.0) + jnp.log(1.0 + jnp.exp(-jnp.abs(x)))


def _rms(x, w):
    return x * lax.rsqrt(jnp.mean(x * x, axis=-1, keepdims=True) + EPS) * w


O_BA = C_Z + H_A * DV_A
O_QB = O_BA + 2 * H_A
O_LR = O_QB + 2 * H_B * DK_B + 2 * H_B * DV_B
O_GA = O_LR + GATE_RANK
W_ROWS = 128


def _wprep_kernel(w_ref, big_ref, small_ref):
    w = w_ref[...]
    big_ref[...] = jnp.concatenate([w[:, :O_BA], w[:, O_QB:O_LR], w[:, O_GA:]], axis=1).astype(BF16)
    pad = jnp.zeros((w.shape[0], LANES - 2 * H_A - GATE_RANK), F32)
    small_ref[...] = jnp.concatenate([w[:, O_BA:O_QB], w[:, O_LR:O_GA], pad], axis=1)


def _wprep(w_in):
    depth, d, ncols = w_in.shape
    return pl.pallas_call(
        _wprep_kernel,
        grid=(depth, d // W_ROWS),
        in_specs=[pl.BlockSpec((None, W_ROWS, ncols), lambda l, i: (l, i, 0))],
        out_specs=[
            pl.BlockSpec((None, W_ROWS, C_BIG), lambda l, i: (l, i, 0)),
            pl.BlockSpec((None, W_ROWS, LANES), lambda l, i: (l, i, 0)),
        ],
        out_shape=[
            jax.ShapeDtypeStruct((depth, d, C_BIG), BF16),
            jax.ShapeDtypeStruct((depth, d, LANES), F32),
        ],
        compiler_params=pltpu.CompilerParams(
            dimension_semantics=("parallel", "parallel"), vmem_limit_bytes=VMEM_LIMIT),
        name="wprep",
    )(w_in.astype(F32))

def _inproj_kernel(x_ref, nw_ref, wb_ref, ws_ref, big_ref, small_ref, xn_ref):
    @pl.when(pl.program_id(1) == 0)
    def _():
        xn = _rms(x_ref[...], nw_ref[...])
        xn_ref[...] = xn.astype(BF16)
        small_ref[...] = _dot3(xn, ws_ref[...])

    big_ref[...] = _dot(xn_ref[...], wb_ref[...])


def _inproj(x, nw, wbig, wsmall, layer, tm, tn):
    n = x.shape[0]
    return pl.pallas_call(
        _inproj_kernel,
        grid=(n // tm, C_BIG // tn),
        in_specs=[
            pl.BlockSpec((tm, D_MODEL), lambda i, j: (i, 0)),
            pl.BlockSpec((1, D_MODEL), lambda i, j: (0, 0)),
            pl.BlockSpec((None, D_MODEL, tn), lambda i, j: (layer, 0, j)),
            pl.BlockSpec((None, D_MODEL, LANES), lambda i, j: (layer, 0, 0)),
        ],
        out_specs=[
            pl.BlockSpec((tm, tn), lambda i, j: (i, j)),
            pl.BlockSpec((tm, LANES), lambda i, j: (i, 0)),
        ],
        out_shape=[
            jax.ShapeDtypeStruct((n, C_BIG), F32),
            jax.ShapeDtypeStruct((n, LANES), F32),
        ],
        scratch_shapes=[pltpu.VMEM((tm, D_MODEL), BF16)],
        compiler_params=pltpu.CompilerParams(
            dimension_semantics=("parallel", "arbitrary"), vmem_limit_bytes=VMEM_LIMIT),
        name="inproj",
    )(x, nw, wbig, wsmall)


def _outproj_kernel(oa_ref, ob_ref, ga_ref, gb_ref, x_ref, wa_ref, wb_ref, wo_ref, out_ref):
    ya = _dot(oa_ref[...].astype(BF16), wa_ref[...])
    yb = _dot(ob_ref[...].astype(BF16), wb_ref[...])
    merged = _sigmoid(ga_ref[...]) * ya + _sigmoid(gb_ref[...]) * yb
    out_ref[...] = x_ref[...] + _dot(merged.astype(BF16), wo_ref[...])


def _outproj(oa, ob, big, x, wa, wb, wo, layer, tm):
    wspec = pl.BlockSpec((None, D_MODEL, D_MODEL), lambda i: (layer, 0, 0))
    n = x.shape[0]
    row = lambda i: (i, 0)
    return pl.pallas_call(
        _outproj_kernel,
        grid=(n // tm,),
        in_specs=[
            pl.BlockSpec((tm, D_MODEL), row),
            pl.BlockSpec((tm, D_MODEL), row),
            pl.BlockSpec((tm, D_MODEL), lambda i: (i, C_GA // D_MODEL)),
            pl.BlockSpec((tm, D_MODEL), lambda i: (i, C_GB // D_MODEL)),
            pl.BlockSpec((tm, D_MODEL), row),
            wspec,
            wspec,
            wspec,
        ],
        out_specs=pl.BlockSpec((tm, D_MODEL), row),
        out_shape=jax.ShapeDtypeStruct((n, D_MODEL), F32),
        compiler_params=pltpu.CompilerParams(
            dimension_semantics=("parallel",), vmem_limit_bytes=VMEM_LIMIT),
        name="outproj",
    )(oa, ob, big, big, x, wa, wb, wo)


def _route(logits):
    lane = lax.broadcasted_iota(jnp.int32, logits.shape, 1)
    neg = jnp.float32(-jnp.inf)
    big = jnp.int32(1 << 20)
    gmask = (lane >= N_EXPERTS) & (lane < N_EXPERTS + N_GROUPS)
    gl = jnp.where(gmask, logits, neg)
    gmax = jnp.max(gl, axis=-1, keepdims=True)
    gsel = jnp.min(jnp.where(gl == gmax, lane - N_EXPERTS, big), axis=-1, keepdims=True)
    pgsel = 1.0 / jnp.sum(jnp.where(gmask, jnp.exp(gl - gmax), 0.0), axis=-1, keepdims=True)
    lo = gsel * EXPERTS_PER_GROUP
    emask = (lane >= lo) & (lane < lo + EXPERTS_PER_GROUP)
    el = jnp.where(emask, logits, neg)
    m1 = jnp.max(el, axis=-1, keepdims=True)
    i1 = jnp.min(jnp.where(el == m1, lane, big), axis=-1, keepdims=True)
    el2 = jnp.where(lane == i1, neg, el)
    m2 = jnp.max(el2, axis=-1, keepdims=True)
    i2 = jnp.min(jnp.where(el2 == m2, lane, big), axis=-1, keepdims=True)
    r = jnp.exp(m2 - m1)
    w1 = pgsel / (1.0 + r)
    w2 = pgsel * r / (1.0 + r)
    return i1, i2, w1, w2


def _router_kernel(x_ref, nw_ref, wr_ref, br_ref, xn_ref, idx_ref, wt_ref, cnt_ref):
    tm = x_ref.shape[0]
    xn = _rms(x_ref[...], nw_ref[...])
    xn_ref[...] = xn
    i1, i2, w1, w2 = _route(_dot3(xn, wr_ref[...]) + br_ref[...])
    lane = lax.broadcasted_iota(jnp.int32, (tm, LANES), 1)
    onehot = jnp.where((lane == i1) | (lane == i2), 1.0, 0.0)
    r = lax.broadcasted_iota(jnp.int32, (tm, tm), 0)
    c = lax.broadcasted_iota(jnp.int32, (tm, tm), 1)
    earlier = jnp.where(c < r, 1.0, 0.0).astype(BF16)
    cum = _dot(earlier, onehot.astype(BF16))
    rank1 = jnp.sum(jnp.where(lane == i1, cum, 0.0), axis=-1, keepdims=True).astype(jnp.int32)
    rank2 = jnp.sum(jnp.where(lane == i2, cum, 0.0), axis=-1, keepdims=True).astype(jnp.int32)
    idx_ref[...] = jnp.where(lane == 0, i1, jnp.where(lane == 1, i2, jnp.where(
        lane == 2, rank1, jnp.where(lane == 3, rank2, 0))))
    wt_ref[...] = jnp.where(lane == 0, w1, jnp.where(lane == 1, w2, 0.0))
    counts = jnp.sum(onehot, axis=0, keepdims=True).astype(jnp.int32)
    cnt_ref[...] = jnp.broadcast_to(counts, cnt_ref.shape)


def _router(x, nw, wr, br, tm):
    n = x.shape[0]
    row = lambda i: (i, 0)
    full = lambda i: (0, 0)
    return pl.pallas_call(
        _router_kernel,
        grid=(n // tm,),
        in_specs=[
            pl.BlockSpec((tm, D_MODEL), row),
            pl.BlockSpec((1, D_MODEL), full),
            pl.BlockSpec((D_MODEL, LANES), full),
            pl.BlockSpec((1, LANES), full),
        ],
        out_specs=[
            pl.BlockSpec((tm, D_MODEL), row),
            pl.BlockSpec((tm, LANES), row),
            pl.BlockSpec((tm, LANES), row),
            pl.BlockSpec((None, SUBLANES, LANES), lambda i: (i, 0, 0)),
        ],
        out_shape=[
            jax.ShapeDtypeStruct((n, D_MODEL), F32),
            jax.ShapeDtypeStruct((n, LANES), jnp.int32),
            jax.ShapeDtypeStruct((n, LANES), F32),
            jax.ShapeDtypeStruct((n // tm, SUBLANES, LANES), jnp.int32),
        ],
        compiler_params=pltpu.CompilerParams(
            dimension_semantics=("parallel",), vmem_limit_bytes=VMEM_LIMIT),
        name="router",
    )(x, nw, wr, br)


def _moe_plan(idx, cnt, tm_tok, tm_exp):
    n = idx.shape[0]
    nt = 2 * n // tm_exp + N_EXPERTS
    counts = cnt[:, 0, :N_EXPERTS]
    cpad = (jnp.sum(counts, axis=0) + tm_exp - 1) // tm_exp * tm_exp
    ends = jnp.cumsum(cpad)
    base = (ends - cpad)[None] + jnp.cumsum(counts, axis=0) - counts
    base_tok = jnp.repeat(base, tm_tok, axis=0)[:, None, :]
    hit = idx[:, :2, None] == jnp.arange(N_EXPERTS, dtype=jnp.int32)
    dest = (jnp.sum(jnp.where(hit, base_tok, 0), axis=-1) + idx[:, 2:4]).astype(jnp.int32)
    tile_start = jnp.arange(nt, dtype=jnp.int32) * tm_exp
    tile_e = jnp.minimum(jnp.sum(tile_start[:, None] >= ends[None, :], axis=1), N_EXPERTS - 1).astype(jnp.int32)
    tile_valid = (tile_start < ends[-1]).astype(jnp.int32)
    return dest, tile_e, tile_valid


def _slot_major(dest, tm):
    nt = dest.shape[0] // tm
    return jnp.swapaxes(dest.reshape(nt, tm, 2), 1, 2).reshape(nt, 1, 2 * tm)


def _dispatch_kernel(d_ref, xn_ref, xs_in_hbm, xs_hbm, sem):
    del xs_in_hbm
    tm = xn_ref.shape[0]

    def copy(i, j, dst_row):
        return pltpu.make_async_copy(xn_ref.at[pl.ds(i, 1), :], xs_hbm.at[pl.ds(dst_row, 1), :], sem.at[j])

    def start(i, carry):
        for j in range(2):
            copy(i, j, d_ref[0, j * tm + i]).start(priority=j)
        return carry

    def wait(i, carry):
        for j in range(2):
            copy(i, j, 0).wait()
        return carry

    lax.fori_loop(0, tm, start, 0, unroll=8)
    lax.fori_loop(0, tm, wait, 0, unroll=8)


def _dispatch(xn, dest, xs_buf, tm):
    n = xn.shape[0]
    nrows = xs_buf.shape[0]
    return pl.pallas_call(
        _dispatch_kernel,
        grid=(n // tm,),
        in_specs=[
            pl.BlockSpec((None, 1, 2 * tm), lambda t: (t, 0, 0), memory_space=pltpu.SMEM),
            pl.BlockSpec((tm, D_MODEL), lambda t: (t, 0)),
            pl.BlockSpec(memory_space=pl.ANY),
        ],
        out_specs=pl.BlockSpec(memory_space=pl.ANY),
        out_shape=jax.ShapeDtypeStruct((nrows, D_MODEL), F32),
        scratch_shapes=[pltpu.SemaphoreType.DMA((2,))],
        input_output_aliases={2: 0},
        compiler_params=pltpu.CompilerParams(
            dimension_semantics=("arbitrary",), vmem_limit_bytes=VMEM_LIMIT),
        name="dispatch",
    )(_slot_major(dest, tm), xn, xs_buf)


def _row_gather(idx_ref, src_hbm, buf, sem, slot, nrows):
    def body(r, carry):
        for j in range(2):
            rr = 2 * r + j
            pltpu.make_async_copy(src_hbm.at[pl.ds(idx_ref[0, rr], 1), :],
                                  buf.at[slot, pl.ds(rr, 1), :], sem.at[slot]).start(priority=j)
        return carry
    lax.fori_loop(0, nrows // 2, body, 0, unroll=8)


def _row_gather_wait(src_hbm, buf, sem, slot, nrows):
    def body(r, carry):
        pltpu.make_async_copy(src_hbm.at[pl.ds(0, 1), :],
                              buf.at[slot, pl.ds(r, 1), :], sem.at[slot]).wait()
        return carry
    lax.fori_loop(0, nrows, body, 0, unroll=8)


def _expert_kernel(te_ref, tv_ref, xs_ref, wg_ref, wu_ref, wd_ref, y_ref):
    t = pl.program_id(0)

    @pl.when(tv_ref[t] == 1)
    def _():
        xb = xs_ref[...].astype(BF16)
        a = _silu(_dot(xb, wg_ref[...])) * _dot(xb, wu_ref[...])
        y_ref[...] = _dot(a.astype(BF16), wd_ref[...])

    @pl.when(tv_ref[t] == 0)
    def _():
        y_ref[...] = jnp.zeros_like(y_ref)


def _experts(xs, tile_e, tile_valid, wg, wu, wd, layer, tm):
    w_in_spec = pl.BlockSpec((None, None, D_MODEL, D_FF_E), lambda t, te, tv: (layer, te[t], 0, 0))
    nt = xs.shape[0] // tm
    return pl.pallas_call(
        _expert_kernel,
        grid_spec=pltpu.PrefetchScalarGridSpec(
            num_scalar_prefetch=2,
            grid=(nt,),
            in_specs=[
                pl.BlockSpec((tm, D_MODEL), lambda t, te, tv: (t, 0)),
                w_in_spec,
                w_in_spec,
                pl.BlockSpec((None, None, D_FF_E, D_MODEL), lambda t, te, tv: (layer, te[t], 0, 0)),
            ],
            out_specs=pl.BlockSpec((tm, D_MODEL), lambda t, te, tv: (t, 0)),
        ),
        out_shape=jax.ShapeDtypeStruct((nt * tm, D_MODEL), F32),
        compiler_params=pltpu.CompilerParams(
            dimension_semantics=("arbitrary",), vmem_limit_bytes=VMEM_LIMIT),
        name="experts",
    )(tile_e, tile_valid, xs, wg, wu, wd)


def _combine_kernel(split_tile, d_ref, d_next_ref, y_hbm, x_ref, wt_ref, fnw_ref, *rest):
    out_refs, (ybuf, sem) = rest[:-2], rest[-2:]
    t = pl.program_id(0)
    nt = pl.num_programs(0)
    tm = x_ref.shape[0]
    slot = t & 1

    @pl.when(t == 0)
    def _():
        _row_gather(d_ref, y_hbm, ybuf, sem, 0, 2 * tm)

    @pl.when(t + 1 < nt)
    def _():
        _row_gather(d_next_ref, y_hbm, ybuf, sem, 1 - slot, 2 * tm)

    _row_gather_wait(y_hbm, ybuf, sem, slot, 2 * tm)
    wt = wt_ref[...]
    out = x_ref[...] + wt[:, 0:1] * ybuf[slot, 0:tm, :] + wt[:, 1:2] * ybuf[slot, tm:2 * tm, :]
    if split_tile is None:
        out_refs[0][...] = out
    else:
        res = _rms(out, fnw_ref[...])

        @pl.when(t < split_tile)
        def _():
            out_refs[0][...] = res

        @pl.when(t >= split_tile)
        def _():
            out_refs[1][...] = res


def _combine(x, y, dest, wt, fnw, split_rows, tm):
    n = x.shape[0]
    nt = n // tm
    d = _slot_major(dest, tm)
    if split_rows is None:
        split_tile = None
        out_specs = pl.BlockSpec((tm, D_MODEL), lambda t: (t, 0))
        out_shape = jax.ShapeDtypeStruct((n, D_MODEL), F32)
    else:
        split_tile = split_rows // tm
        assert split_tile * tm == split_rows and 0 < split_tile < nt
        out_specs = [pl.BlockSpec((tm, D_MODEL), lambda t: (jnp.minimum(t, split_tile - 1), 0)),
                     pl.BlockSpec((tm, D_MODEL), lambda t: (jnp.maximum(t - split_tile, 0), 0))]
        out_shape = [jax.ShapeDtypeStruct((split_rows, D_MODEL), F32),
                     jax.ShapeDtypeStruct((n - split_rows, D_MODEL), F32)]
    return pl.pallas_call(
        functools.partial(_combine_kernel, split_tile),
        grid=(nt,),
        in_specs=[
            pl.BlockSpec((None, 1, 2 * tm), lambda t: (t, 0, 0), memory_space=pltpu.SMEM),
            pl.BlockSpec((None, 1, 2 * tm), lambda t: (jnp.minimum(t + 1, nt - 1), 0, 0),
                         memory_space=pltpu.SMEM),
            pl.BlockSpec(memory_space=pl.ANY),
            pl.BlockSpec((tm, D_MODEL), lambda t: (t, 0)),
            pl.BlockSpec((tm, LANES), lambda t: (t, 0)),
            pl.BlockSpec((1, D_MODEL), lambda t: (0, 0)),
        ],
        out_specs=out_specs,
        out_shape=out_shape,
        scratch_shapes=[pltpu.VMEM((2, 2 * tm, D_MODEL), F32), pltpu.SemaphoreType.DMA((2,))],
        compiler_params=pltpu.CompilerParams(
            dimension_semantics=("arbitrary",), vmem_limit_bytes=VMEM_LIMIT),
        name="combine",
    )(d, d, y, x, wt, fnw)


def _moe_rows(n, tm_exp):
    return (2 * n // tm_exp + N_EXPERTS) * tm_exp


def _moe(x, nw, wr, br, wg, wu, wd, layer, fnw, xs_buf, split_rows, tm_tok, tm_exp):
    xn, idx, wt, cnt = _router(x, nw, wr, br, tm_tok)
    dest, tile_e, tile_valid = _moe_plan(idx, cnt, tm_tok, tm_exp)
    xs = _dispatch(xn, dest, xs_buf, tm_tok)
    y = _experts(xs, tile_e, tile_valid, wg, wu, wd, layer, tm_exp)
    return _combine(x, y, dest, wt, fnw, split_rows, tm_exp), xs


def _seq_masks(n, lseq):
    r = lax.broadcasted_iota(jnp.int32, (n, n), 0)
    c = lax.broadcasted_iota(jnp.int32, (n, n), 1)
    sh = lseq.bit_length() - 1
    same = (r >> sh) == (c >> sh)
    return same & (r >= c), same & (r > c), r == c


def _valid_rows(n, lseq, t_lo, t_hi):
    row = lax.broadcasted_iota(jnp.int32, (n, 1), 0)
    t = (row & (lseq - 1)) + pl.program_id(1) * lseq
    return (t >= t_lo) & (t < t_hi)


def _pad_rows(a, n):
    return jnp.concatenate([a, jnp.zeros((n - a.shape[0], a.shape[1]), a.dtype)], axis=0)


def _gdn_kernel(Lseq, NS, t_lo, t_hi, conv_row0,
                qkv_ref, z_ref, sm_ref, cin_ref, sin_ref, cw_ref, hp_ref, na_ref, obuf_ref, sbuf_ref,
                o_ref, cout_ref, so_ref, ext_ref):
    del obuf_ref, sbuf_ref
    R = NS * Lseq
    R2 = 2 * R
    npairs = H_A // 2

    @pl.when(pl.program_id(1) == 0)
    def _():
        ext_ref[:, 0:SUBLANES, :] = cin_ref[...]
        so_ref[...] = sin_ref[...]

    valid = _valid_rows(R, Lseq, t_lo, t_hi)
    valid_l = _valid_rows(Lseq, Lseq, t_lo, t_hi)
    for s in range(NS):
        ext_ref[s, SUBLANES:SUBLANES + Lseq, :] = jnp.where(
            valid_l, qkv_ref[s * Lseq:(s + 1) * Lseq, :], 0.0)

    def conv_act(col0):
        pieces = []
        for s in range(NS):
            acc = None
            for i in range(CONV_W):
                r0 = SUBLANES - (CONV_W - 1) + i
                term = cw_ref[i:i + 1, col0:col0 + LANES] * ext_ref[s, r0:r0 + Lseq, col0:col0 + LANES]
                acc = term if acc is None else acc + term
            pieces.append(acc)
        return _silu(pieces[0] if NS == 1 else jnp.concatenate(pieces, axis=0))

    incl, _, _ = _seq_masks(R, Lseq)
    incl2, strict2, diag2 = _seq_masks(R2, Lseq)
    tri = jnp.where(incl, 1.0, 0.0).astype(BF16)
    eye2 = jnp.where(diag2, 1.0, 0.0)

    sm = sm_ref[...]
    beta_all = jnp.where(valid, _sigmoid(sm), 0.0)
    g_all = jnp.where(valid, -jnp.exp(hp_ref[0:1, :]) * _softplus(sm + hp_ref[1:2, :]), 0.0)
    G_all = _mask_dot(tri, g_all)
    GT = jnp.transpose(_pad_rows(G_all, LANES))

    q2, k2, v2, beta2, Gc2, Gr2 = [], [], [], [], [], []
    for p in range(npairs):
        qs, ks, vs = [], [], []
        for h in (2 * p, 2 * p + 1):
            q = conv_act(h * DK_A)
            k = conv_act(H_A * DK_A + h * DK_A)
            qs.append(q * lax.rsqrt(jnp.sum(q * q, axis=-1, keepdims=True) + EPS) * (DK_A ** -0.5))
            ks.append(jnp.where(valid, k * lax.rsqrt(jnp.sum(k * k, axis=-1, keepdims=True) + EPS), 0.0))
            vs.append(conv_act(2 * H_A * DK_A + h * DV_A))
        q2.append(jnp.concatenate(qs, axis=0))
        k2.append(jnp.concatenate(ks, axis=0))
        v2.append(jnp.concatenate(vs, axis=0))
        h0 = 2 * p
        beta2.append(jnp.concatenate([beta_all[:, S_BETA + h0:S_BETA + h0 + 1],
                                      beta_all[:, S_BETA + h0 + 1:S_BETA + h0 + 2]], axis=0))
        Gc2.append(jnp.concatenate([G_all[:, S_DECAY + h0:S_DECAY + h0 + 1],
                                    G_all[:, S_DECAY + h0 + 1:S_DECAY + h0 + 2]], axis=0))
        Gr2.append(jnp.concatenate([GT[S_DECAY + h0:S_DECAY + h0 + 1, :R],
                                    GT[S_DECAY + h0 + 1:S_DECAY + h0 + 2, :R]], axis=1))

    rng = range(npairs)
    kk = [_bdot_nt(k2[p], k2[p]) for p in rng]
    qk = [_bdot_nt(q2[p], k2[p]) for p in rng]
    dec = [jnp.where(incl2, jnp.exp(jnp.where(incl2, Gc2[p] - Gr2[p], 0.0)), 0.0) for p in rng]
    Bp = [-jnp.where(strict2, beta2[p] * kk[p] * dec[p], 0.0) for p in rng]
    P = [eye2 + Bp[p] for p in rng]
    n = 2
    while n < Lseq:
        Bp = [_dot3(b, b) for b in Bp]
        P = [pp + _dot3(pp, b) for pp, b in zip(P, Bp)]
        n *= 2
    eG = [jnp.exp(Gc2[p]) for p in rng]
    rhs = [jnp.concatenate([beta2[p] * v2[p], (beta2[p] * eG[p]) * k2[p]], axis=1) for p in rng]
    sol = [_dot3(P[p], rhs[p]) for p in rng]
    qk = [qk[p] * dec[p] for p in rng]
    qe = [q2[p] * eG[p] for p in rng]

    items = [(p, j, s) for p in rng for j in range(2) for s in range(NS)]

    def rows(j, s):
        return slice(j * R + s * Lseq, j * R + (s + 1) * Lseq)

    S = {it: so_ref[it[2], 2 * it[0] + it[1]] for it in items}
    both = {}
    for p, j, s in items:
        r = rows(j, s)
        both[(p, j, s)] = _bdot(jnp.concatenate([sol[p][r, DV_A:], qe[p][r]], axis=0), S[(p, j, s)])
    vnew = {}
    for p, j, s in items:
        vnew[(p, j, s)] = sol[p][rows(j, s), :DV_A] - both[(p, j, s)][:Lseq]
    for p, j, s in items:
        r = rows(j, s)
        last = j * R + (s + 1) * Lseq - 1
        GL = Gc2[p][last:last + 1]
        kd = k2[p][r] * jnp.exp(GL - Gc2[p][r])
        so_ref[s, 2 * p + j] = S[(p, j, s)] * jnp.exp(GL) + _bdot_tn(kd, vnew[(p, j, s)])
    for p in rng:
        order = [(p, j, s) for j in range(2) for s in range(NS)]
        vnew2 = jnp.concatenate([vnew[it] for it in order], axis=0)
        o2 = jnp.concatenate([both[it][Lseq:] for it in order], axis=0) + _bdot(qk[p], vnew2)
        for j in range(2):
            h = 2 * p + j
            zh = z_ref[:, h * DV_A:(h + 1) * DV_A]
            o_ref[:, h * DV_A:(h + 1) * DV_A] = _rms(o2[j * R:(j + 1) * R], na_ref[...]) * _silu(zh)

    for s in range(NS):
        cout_ref[s] = ext_ref[s, conv_row0:conv_row0 + SUBLANES, :]
        ext_ref[s, 0:SUBLANES, :] = ext_ref[s, Lseq:Lseq + SUBLANES, :]


def _gdn_parts(big, small, cin, sin, layer_in, cw, hp, na, obuf, sbuf, layer, NS, nseq, rb):
    return dict(
        operands=(big, big, small, cin, sin, cw, hp, na, obuf, sbuf),
        in_specs=[
            pl.BlockSpec((ROWS, CONV_CH), lambda b, c: (rb(b, c), 0)),
            pl.BlockSpec((ROWS, H_A * DV_A), lambda b, c: (rb(b, c), C_Z // (H_A * DV_A))),
            pl.BlockSpec((ROWS, LANES), lambda b, c: (rb(b, c), 0)),
            pl.BlockSpec((None, NS, SUBLANES, CONV_CH), lambda b, c: (layer_in, b, 0, 0)),
            pl.BlockSpec((None, NS, H_A, DK_A, DV_A), lambda b, c: (layer_in, b, 0, 0, 0)),
            pl.BlockSpec((SUBLANES, CONV_CH), lambda b, c: (0, 0)),
            pl.BlockSpec((SUBLANES, LANES), lambda b, c: (0, 0)),
            pl.BlockSpec((1, DV_A), lambda b, c: (0, 0)),
            pl.BlockSpec(memory_space=pl.ANY),
            pl.BlockSpec(memory_space=pl.ANY),
        ],
        out_specs=[
            pl.BlockSpec((ROWS, H_A * DV_A), lambda b, c: (rb(b, c), 0)),
            pl.BlockSpec((NS, SUBLANES, CONV_CH), lambda b, c: (b, 0, 0)),
            pl.BlockSpec((None, NS, H_A, DK_A, DV_A), lambda b, c: (layer, b, 0, 0, 0)),
        ],
        out_shape=[
            jax.ShapeDtypeStruct(obuf.shape, F32),
            jax.ShapeDtypeStruct((nseq, SUBLANES, CONV_CH), F32),
            jax.ShapeDtypeStruct(sbuf.shape, F32),
        ],
        aliases={8: 0, 9: 2},
    )


def _gla_kernel(Lseq, NS, SB, t_lo, t_hi,
                qb_ref, kb_ref, vb_ref, rb_ref, sm_ref, sin_ref, wup_ref, bup_ref, nb_ref, obuf_ref,
                sbuf_ref, o_ref, so_ref):
    del obuf_ref, sbuf_ref
    R = NS * Lseq
    heads = range(H_B)

    @pl.when(pl.program_id(1) == 0)
    def _():
        so_ref[...] = sin_ref[...]

    nb = Lseq // SB
    assert nb == 1 or NS == 1
    n_off = SB * nb * (nb - 1) // 2
    sb_shift = SB.bit_length() - 1

    valid = _valid_rows(R, Lseq, t_lo, t_hi)
    r = lax.broadcasted_iota(jnp.int32, (R, R), 0)
    c = lax.broadcasted_iota(jnp.int32, (R, R), 1)
    same = (r | (Lseq - 1)) == (c | (Lseq - 1))
    m_row = jnp.where(same & (c <= r), 1.0, 0.0)
    m_first = jnp.where(same & (c <= (r & ~(SB - 1))), 1.0, 0.0)
    m_last = jnp.where(same, 1.0, 0.0)
    masks = jnp.concatenate([m_row, m_first, m_last], axis=0).astype(BF16)

    pre = _dot3(sm_ref[...], wup_ref[...]) + bup_ref[...]
    lg = jnp.where(valid, -_softplus(-pre) / GATE_TAU, 0.0)
    cums = _mask_dot(masks, lg)
    Bc_all, b0_all, BL_all = cums[:R], cums[R:2 * R], cums[2 * R:]

    re = lax.broadcasted_iota(jnp.int32, (R, n_off + R), 0)
    ce = lax.broadcasted_iota(jnp.int32, (R, n_off + R), 1)
    rblk = (re & (Lseq - 1)) >> sb_shift
    cd = ce - n_off
    see = ((ce >= (SB // 2) * rblk * (rblk - 1)) & (ce < (SB // 2) * rblk * (rblk + 1))) | (
        (cd >= 0) & ((cd >> sb_shift) == (re >> sb_shift)) & (cd <= re))

    def hs(a, h, w):
        return a[:, h * w:(h + 1) * w]

    q = [hs(qb_ref, h, DK_B) * (DK_B ** -0.5) for h in heads]
    k = [jnp.where(valid, hs(kb_ref, h, DK_B), 0.0) for h in heads]
    v = [hs(vb_ref, h, DV_B) for h in heads]
    Bc = [hs(Bc_all, h, DK_B) for h in heads]
    b0 = [hs(b0_all, h, DK_B) for h in heads]
    BL = [hs(BL_all, h, DK_B) for h in heads]
    qs = [q[h] * jnp.exp(Bc[h] - b0[h]) for h in heads]
    kx, vx = [], []
    for h in heads:
        kparts = [k[h][:i * SB] * jnp.exp(Bc[h][i * SB:i * SB + 1] - Bc[h][:i * SB]) for i in range(1, nb)]
        kparts.append(k[h] * jnp.exp(b0[h] - Bc[h]))
        vparts = [v[h][:i * SB] for i in range(1, nb)] + [v[h]]
        kx.append(kparts[0] if nb == 1 else jnp.concatenate(kparts, axis=0))
        vx.append(vparts[0] if nb == 1 else jnp.concatenate(vparts, axis=0))
    Ax = [jnp.where(see, _dot3(qs[h], kx[h], _dot_nt), 0.0) for h in heads]
    o_intra = [_bdot(Ax[h], vx[h]) for h in heads]
    qe = [q[h] * jnp.exp(Bc[h]) for h in heads]
    kd = [k[h] * jnp.exp(BL[h] - Bc[h]) for h in heads]
    ET = [jnp.transpose(_pad_rows(jnp.exp(BL[h]), LANES)) for h in heads]

    items = [(h, s) for h in heads for s in range(NS)]
    S = {it: so_ref[it[1], it[0]] for it in items}
    o_inter = {}
    for h, s in items:
        o_inter[(h, s)] = _bdot(qe[h][s * Lseq:(s + 1) * Lseq], S[(h, s)])
    for h, s in items:
        rows = slice(s * Lseq, (s + 1) * Lseq)
        so_ref[s, h] = S[(h, s)] * ET[h][:, s * Lseq:s * Lseq + 1] + _bdot_tn(kd[h][rows], v[h][rows])
    for h in heads:
        parts = [o_inter[(h, s)] for s in range(NS)]
        o = (parts[0] if NS == 1 else jnp.concatenate(parts, axis=0)) + o_intra[h]
        rh = hs(rb_ref, h, DV_B)
        o_ref[:, h * DV_B:(h + 1) * DV_B] = _rms(o, nb_ref[...]) * _silu(rh)


def _gla_parts(big, small, sin, layer_in, wup, bup, nb, obuf, sbuf, layer, NS, r):
    return dict(
        operands=(big, big, big, big, small, sin, wup, bup, nb, obuf, sbuf),
        in_specs=[
            pl.BlockSpec((ROWS, H_B * DK_B), lambda b, c: (r(b, c), C_QB // (H_B * DK_B))),
            pl.BlockSpec((ROWS, H_B * DK_B), lambda b, c: (r(b, c), C_KB // (H_B * DK_B))),
            pl.BlockSpec((ROWS, H_B * DV_B), lambda b, c: (r(b, c), C_VB // (H_B * DV_B))),
            pl.BlockSpec((ROWS, H_B * DV_B), lambda b, c: (r(b, c), C_RB // (H_B * DV_B))),
            pl.BlockSpec((ROWS, LANES), lambda b, c: (r(b, c), 0)),
            pl.BlockSpec((None, NS, H_B, DK_B, DV_B), lambda b, c: (layer_in, b, 0, 0, 0)),
            pl.BlockSpec((LANES, H_B * DK_B), lambda b, c: (0, 0)),
            pl.BlockSpec((1, H_B * DK_B), lambda b, c: (0, 0)),
            pl.BlockSpec((1, DV_B), lambda b, c: (0, 0)),
            pl.BlockSpec(memory_space=pl.ANY),
            pl.BlockSpec(memory_space=pl.ANY),
        ],
        out_specs=[
            pl.BlockSpec((ROWS, H_B * DV_B), lambda b, c: (r(b, c), 0)),
            pl.BlockSpec((None, NS, H_B, DK_B, DV_B), lambda b, c: (layer, b, 0, 0, 0)),
        ],
        out_shape=[
            jax.ShapeDtypeStruct(obuf.shape, F32),
            jax.ShapeDtypeStruct(sbuf.shape, F32),
        ],
        aliases={9: 0, 10: 1},
    )


def _mixer_kernel(Lseq, NS, SB, t_lo, t_hi, conv_row0, n_in_a, n_in_b, n_out_a, *refs):
    in_a, in_b = refs[:n_in_a], refs[n_in_a:n_in_a + n_in_b]
    outs = refs[n_in_a + n_in_b:-1]
    _gdn_kernel(Lseq, NS, t_lo, t_hi, conv_row0, *in_a, *outs[:n_out_a], refs[-1])
    _gla_kernel(Lseq, NS, SB, t_lo, t_hi, *in_b, *outs[n_out_a:])


def _mixer(gdn, gla, *, nsteps, Lseq, NS, SB, nchunks, t_lo, t_hi):
    assert NS * Lseq == ROWS
    n_in_a, n_out_a = len(gdn["operands"]), len(gdn["out_specs"])
    aliases = dict(gdn["aliases"])
    aliases.update({n_in_a + i: n_out_a + o for i, o in gla["aliases"].items()})
    conv_row0 = t_hi - (nchunks - 1) * Lseq
    kern = functools.partial(_mixer_kernel, Lseq, NS, SB, t_lo, t_hi, conv_row0,
                             n_in_a, len(gla["operands"]), n_out_a)
    return pl.pallas_call(
        kern,
        grid=(nsteps, nchunks),
        in_specs=gdn["in_specs"] + gla["in_specs"],
        out_specs=gdn["out_specs"] + gla["out_specs"],
        out_shape=gdn["out_shape"] + gla["out_shape"],
        scratch_shapes=[pltpu.VMEM((NS, SUBLANES + Lseq, CONV_CH), F32)],
        input_output_aliases=aliases,
        compiler_params=pltpu.CompilerParams(
            dimension_semantics=("parallel", "arbitrary"), vmem_limit_bytes=VMEM_LIMIT),
        name="mixer_L%d" % Lseq,
    )(*gdn["operands"], *gla["operands"])


def kernel(x_prompt, x_sample, state_delta, state_conv, state_gla, meta_tokens, norm_mix, w_in, conv_w, a_log, dt_bias, norm_a, gla_gate_up, gla_gate_bias, norm_b, w_proj_a, w_proj_b, w_out, norm_ffn, router_group_w, router_group_b, router_expert_w, router_expert_b, expert_w_gate, expert_w_up, expert_w_down, norm_final):
    Bp, Tp, D = x_prompt.shape
    Bs, Ts, _ = x_sample.shape
    depth = w_in.shape[0]
    assert N_META <= CHUNK and Tp % CHUNK == 0 and CHUNK == ROWS
    pad_p = CHUNK - N_META
    Tpp = CHUNK + Tp
    ncp = Tpp // CHUNK
    n_prompt = Bp * Tp
    n_head = Bp * CHUNK
    ns_rows = Bs * SAMPLE_L
    n = n_prompt + n_head + ns_rows
    seq_per_step = ROWS // SAMPLE_L
    dt = x_prompt.dtype
    head = jnp.concatenate([
        jnp.zeros((Bp, pad_p, D), dt),
        jnp.broadcast_to(meta_tokens.astype(dt)[None], (Bp, N_META, D))], axis=1).reshape(n_head, D)
    xs = jnp.pad(x_sample, ((0, 0), (0, SAMPLE_L - Ts), (0, 0))).reshape(ns_rows, D)
    x = jnp.concatenate([x_prompt.reshape(n_prompt, D), head, xs.astype(dt)], axis=0).astype(F32)

    tm_big = 1280 if n % 1280 == 0 else 512
    tm = 512

    zeros_conv = jnp.zeros((1, Bp, SUBLANES, CONV_CH), F32)
    zeros_sa = jnp.zeros((1, Bp, H_A, DK_A, DV_A), F32)
    zeros_sb = jnp.zeros((1, Bp, H_B, DK_B, DV_B), F32)
    head_block0 = n_prompt // ROWS
    sample_block0 = (n_prompt + n_head) // ROWS
    prompt_args = dict(nsteps=Bp, Lseq=CHUNK, NS=1, nchunks=ncp, t_lo=pad_p, t_hi=Tpp)
    sample_args = dict(nsteps=Bs // seq_per_step, Lseq=SAMPLE_L, NS=seq_per_step, nchunks=1, t_lo=0, t_hi=Ts)
    prompt_rb = lambda b, c: jnp.where(c == 0, head_block0 + b, b * (ncp - 1) + c - 1)
    sample_rb = lambda b, c: sample_block0 + b

    wbig_all, wsmall_all = _wprep(w_in)
    wg_all, wu_all = expert_w_gate.astype(BF16), expert_w_up.astype(BF16)
    wd_all = expert_w_down.astype(BF16)
    wa_all, wb_all, wo_all = w_proj_a.astype(BF16), w_proj_b.astype(BF16), w_out.astype(BF16)
    oa = jnp.zeros((n, H_A * DV_A), F32)
    ob = jnp.zeros((n, H_B * DV_B), F32)
    xs_buf = jnp.zeros((_moe_rows(n, MOE_TILE), D), F32)
    sa_p = jnp.zeros((depth, Bp, H_A, DK_A, DV_A), F32)
    sb_p = jnp.zeros((depth, Bp, H_B, DK_B, DV_B), F32)
    sa_s = jnp.zeros((depth, Bs, H_A, DK_A, DV_A), F32)
    sb_s = jnp.zeros((depth, Bs, H_B, DK_B, DV_B), F32)
    sa_in, sb_in = state_delta.astype(F32), state_gla.astype(F32)
    cin_s = jnp.pad(state_conv.astype(F32), ((0, 0), (0, 0), (SUBLANES - (CONV_W - 1), 0), (0, 0)))

    conv_p, conv_s = [], []
    for l in range(depth):
        big, small = _inproj(x, norm_mix[l][None].astype(F32), wbig_all, wsmall_all, l, tm_big, 1024)

        cw = jnp.pad(conv_w[l].astype(F32), ((0, SUBLANES - CONV_W), (0, 0)))
        hp = jnp.zeros((SUBLANES, LANES), F32)
        hp = hp.at[0, S_DECAY:S_DECAY + H_A].set(a_log[l].astype(F32))
        hp = hp.at[1, S_DECAY:S_DECAY + H_A].set(dt_bias[l].astype(F32))
        na = norm_a[l][None].astype(F32)
        wup = jnp.zeros((LANES, H_B * DK_B), F32).at[S_LR:S_LR + GATE_RANK].set(gla_gate_up[l].astype(F32))
        bup = gla_gate_bias[l][None].astype(F32)
        nb = norm_b[l][None].astype(F32)

        oa, c_p, sa_p, ob, sb_p = _mixer(
            _gdn_parts(big, small, zeros_conv, zeros_sa, 0, cw, hp, na, oa, sa_p, l, 1, Bp, prompt_rb),
            _gla_parts(big, small, zeros_sb, 0, wup, bup, nb, ob, sb_p, l, 1, prompt_rb),
            SB=GLA_SUB, **prompt_args)
        oa, c_s, sa_s, ob, sb_s = _mixer(
            _gdn_parts(big, small, cin_s, sa_in, l, cw, hp, na, oa, sa_s, l, seq_per_step, Bs, sample_rb),
            _gla_parts(big, small, sb_in, l, wup, bup, nb, ob, sb_s, l, seq_per_step, sample_rb),
            SB=SAMPLE_L, **sample_args)
        x = _outproj(oa, ob, big, x, wa_all, wb_all, wo_all, l, tm)

        wr = jnp.concatenate([router_expert_w[l], router_group_w[l],
                              jnp.zeros((D, LANES - N_EXPERTS - N_GROUPS), F32)], axis=1).astype(F32)
        br = jnp.concatenate([router_expert_b[l], router_group_b[l],
                              jnp.zeros((LANES - N_EXPERTS - N_GROUPS,), F32)])[None].astype(F32)
        x, xs_buf = _moe(x, norm_ffn[l][None].astype(F32), wr, br, wg_all, wu_all, wd_all, l,
                         norm_final[None].astype(F32), xs_buf,
                         n_prompt if l == depth - 1 else None, tm, MOE_TILE)

        conv_p.append(c_p[:, SUBLANES - (CONV_W - 1):])
        conv_s.append(c_s[:, SUBLANES - (CONV_W - 1):])

    y_first, y_rest = x
    y_prompt = y_first.reshape(Bp, Tp, D).astype(dt)
    y_sample = y_rest[n_head:].reshape(Bs, SAMPLE_L, D)[:, :Ts].astype(x_sample.dtype)
    return (y_prompt, y_sample,
            sa_p.astype(dt), jnp.stack(conv_p).astype(dt), sb_p.astype(dt),
            sa_s.astype(state_delta.dtype), jnp.stack(conv_s).astype(state_conv.dtype),
            sb_s.astype(state_gla.dtype))
```

```python
import functools

import jax
import jax.numpy as jnp
from jax import lax
from jax.experimental import pallas as pl
from jax.experimental.pallas import tpu as pltpu

F32 = jnp.float32
BF16 = jnp.bfloat16

D_MODEL = 1024
N_META = 16
CHUNK = 64
EPS = 1e-6
H_A, DK_A, DV_A = 8, 128, 128
CONV_W = 4
CONV_CH = 2 * H_A * DK_A + H_A * DV_A
H_B, DK_B, DV_B = 4, 128, 256
GATE_RANK = 16
GATE_TAU = 16.0
N_GROUPS = 4
EXPERTS_PER_GROUP = 8
N_EXPERTS = N_GROUPS * EXPERTS_PER_GROUP
D_FF_E = 256

LANES = 128
SUBLANES = 8
VMEM_LIMIT = 48 * 1024 * 1024

C_QKV = 0
C_Z = C_QKV + CONV_CH
C_QB = C_Z + H_A * DV_A
C_KB = C_QB + H_B * DK_B
C_VB = C_KB + H_B * DK_B
C_RB = C_VB + H_B * DV_B
C_GA = C_RB + H_B * DV_B
C_GB = C_GA + D_MODEL
C_BIG = C_GB + D_MODEL
S_BETA = 0
S_DECAY = H_A
S_LR = 2 * H_A

SAMPLE_L = SUBLANES
ROWS = CHUNK
GLA_SUB = 16
MOE_TILE = 512
COMBINE_TILE = 256


def _dot(a, b, prec=None):
    return jnp.dot(a, b, preferred_element_type=F32, precision=prec)


def _dot_nt(a, b, prec=None):
    return lax.dot_general(a, b, (((1,), (1,)), ((), ())), preferred_element_type=F32, precision=prec)


def _dot_tn(a, b, prec=None):
    return lax.dot_general(a, b, (((0,), (0,)), ((), ())), preferred_element_type=F32, precision=prec)


def _bdot(a, b):
    return _dot(a.astype(BF16), b.astype(BF16))


def _bdot_nt(a, b):
    return _dot_nt(a.astype(BF16), b.astype(BF16))


def _bdot_tn(a, b):
    return _dot_tn(a.astype(BF16), b.astype(BF16))


def _split2(a):
    hi = a.astype(BF16)
    return hi, (a - hi.astype(F32)).astype(BF16)


def _dot3(a, b, dot=_dot):
    ah, al = _split2(a)
    bh, bl = _split2(b)
    return dot(ah, bh) + dot(al, bh) + dot(ah, bl)


def _mask_dot(m, b):
    hi = b.astype(BF16)
    r = b - hi.astype(F32)
    mid = r.astype(BF16)
    lo = (r - mid.astype(F32)).astype(BF16)
    return _dot(m, hi) + _dot(m, mid) + _dot(m, lo)


def _sigmoid(x):
    return 1.0 / (1.0 + jnp.exp(-x))


def _silu(x):
    return x * _sigmoid(x)


def _softplus(x):
    return jnp.maximum(x, 0.0) + jnp.log(1.0 + jnp.exp(-jnp.abs(x)))


def _rms(x, w):
    return x * lax.rsqrt(jnp.mean(x * x, axis=-1, keepdims=True) + EPS) * w


O_BA = C_Z + H_A * DV_A
O_QB = O_BA + 2 * H_A
O_LR = O_QB + 2 * H_B * DK_B + 2 * H_B * DV_B
O_GA = O_LR + GATE_RANK
W_ROWS = 128


def _wprep_kernel(w_ref, big_ref, small_ref):
    w = w_ref[...]
    big_ref[...] = jnp.concatenate([w[:, :O_BA], w[:, O_QB:O_LR], w[:, O_GA:]], axis=1).astype(BF16)
    pad = jnp.zeros((w.shape[0], LANES - 2 * H_A - GATE_RANK), F32)
    small_ref[...] = jnp.concatenate([w[:, O_BA:O_QB], w[:, O_LR:O_GA], pad], axis=1)


def _wprep(w_in):
    depth, d, ncols = w_in.shape
    return pl.pallas_call(
        _wprep_kernel,
        grid=(depth, d // W_ROWS),
        in_specs=[pl.BlockSpec((None, W_ROWS, ncols), lambda l, i: (l, i, 0))],
        out_specs=[
            pl.BlockSpec((None, W_ROWS, C_BIG), lambda l, i: (l, i, 0)),
            pl.BlockSpec((None, W_ROWS, LANES), lambda l, i: (l, i, 0)),
        ],
        out_shape=[
            jax.ShapeDtypeStruct((depth, d, C_BIG), BF16),
            jax.ShapeDtypeStruct((depth, d, LANES), F32),
        ],
        compiler_params=pltpu.CompilerParams(
            dimension_semantics=("parallel", "parallel"), vmem_limit_bytes=VMEM_LIMIT),
        name="wprep",
    )(w_in.astype(F32))

def _inproj_kernel(x_ref, nw_ref, wb_ref, ws_ref, big_ref, small_ref, xn_ref):
    @pl.when(pl.program_id(1) == 0)
    def _():
        xn = _rms(x_ref[...], nw_ref[...])
        xn_ref[...] = xn.astype(BF16)
        small_ref[...] = _dot3(xn, ws_ref[...])

    big_ref[...] = _dot(xn_ref[...], wb_ref[...])


def _inproj(x, nw, wbig, wsmall, layer, tm, tn):
    n = x.shape[0]
    return pl.pallas_call(
        _inproj_kernel,
        grid=(n // tm, C_BIG // tn),
        in_specs=[
            pl.BlockSpec((tm, D_MODEL), lambda i, j: (i, 0)),
            pl.BlockSpec((1, D_MODEL), lambda i, j: (0, 0)),
            pl.BlockSpec((None, D_MODEL, tn), lambda i, j: (layer, 0, j)),
            pl.BlockSpec((None, D_MODEL, LANES), lambda i, j: (layer, 0, 0)),
        ],
        out_specs=[
            pl.BlockSpec((tm, tn), lambda i, j: (i, j)),
            pl.BlockSpec((tm, LANES), lambda i, j: (i, 0)),
        ],
        out_shape=[
            jax.ShapeDtypeStruct((n, C_BIG), F32),
            jax.ShapeDtypeStruct((n, LANES), F32),
        ],
        scratch_shapes=[pltpu.VMEM((tm, D_MODEL), BF16)],
        compiler_params=pltpu.CompilerParams(
            dimension_semantics=("parallel", "arbitrary"), vmem_limit_bytes=VMEM_LIMIT),
        name="inproj",
    )(x, nw, wbig, wsmall)


def _outproj_kernel(oa_ref, ob_ref, ga_ref, gb_ref, x_ref, wa_ref, wb_ref, wo_ref, out_ref):
    ya = _dot(oa_ref[...].astype(BF16), wa_ref[...])
    yb = _dot(ob_ref[...].astype(BF16), wb_ref[...])
    merged = _sigmoid(ga_ref[...]) * ya + _sigmoid(gb_ref[...]) * yb
    out_ref[...] = x_ref[...] + _dot(merged.astype(BF16), wo_ref[...])


def _outproj(oa, ob, big, x, wa, wb, wo, layer, tm):
    wspec = pl.BlockSpec((None, D_MODEL, D_MODEL), lambda i: (layer, 0, 0))
    n = x.shape[0]
    row = lambda i: (i, 0)
    return pl.pallas_call(
        _outproj_kernel,
        grid=(n // tm,),
        in_specs=[
            pl.BlockSpec((tm, D_MODEL), row),
            pl.BlockSpec((tm, D_MODEL), row),
            pl.BlockSpec((tm, D_MODEL), lambda i: (i, C_GA // D_MODEL)),
            pl.BlockSpec((tm, D_MODEL), lambda i: (i, C_GB // D_MODEL)),
            pl.BlockSpec((tm, D_MODEL), row),
            wspec,
            wspec,
            wspec,
        ],
        out_specs=pl.BlockSpec((tm, D_MODEL), row),
        out_shape=jax.ShapeDtypeStruct((n, D_MODEL), F32),
        compiler_params=pltpu.CompilerParams(
            dimension_semantics=("parallel",), vmem_limit_bytes=VMEM_LIMIT),
        name="outproj",
    )(oa, ob, big, big, x, wa, wb, wo)


def _route(logits):
    lane = lax.broadcasted_iota(jnp.int32, logits.shape, 1)
    neg = jnp.float32(-jnp.inf)
    big = jnp.int32(1 << 20)
    gmask = (lane >= N_EXPERTS) & (lane < N_EXPERTS + N_GROUPS)
    gl = jnp.where(gmask, logits, neg)
    gmax = jnp.max(gl, axis=-1, keepdims=True)
    gsel = jnp.min(jnp.where(gl == gmax, lane - N_EXPERTS, big), axis=-1, keepdims=True)
    pgsel = 1.0 / jnp.sum(jnp.where(gmask, jnp.exp(gl - gmax), 0.0), axis=-1, keepdims=True)
    lo = gsel * EXPERTS_PER_GROUP
    emask = (lane >= lo) & (lane < lo + EXPERTS_PER_GROUP)
    el = jnp.where(emask, logits, neg)
    m1 = jnp.max(el, axis=-1, keepdims=True)
    i1 = jnp.min(jnp.where(el == m1, lane, big), axis=-1, keepdims=True)
    el2 = jnp.where(lane == i1, neg, el)
    m2 = jnp.max(el2, axis=-1, keepdims=True)
    i2 = jnp.min(jnp.where(el2 == m2, lane, big), axis=-1, keepdims=True)
    r = jnp.exp(m2 - m1)
    w1 = pgsel / (1.0 + r)
    w2 = pgsel * r / (1.0 + r)
    return i1, i2, w1, w2


def _router_kernel(x_ref, nw_ref, wr_ref, br_ref, xn_ref, idx_ref, wt_ref, cnt_ref):
    tm = x_ref.shape[0]
    xn = _rms(x_ref[...], nw_ref[...])
    xn_ref[...] = xn
    i1, i2, w1, w2 = _route(_dot3(xn, wr_ref[...]) + br_ref[...])
    lane = lax.broadcasted_iota(jnp.int32, (tm, LANES), 1)
    onehot = jnp.where((lane == i1) | (lane == i2), 1.0, 0.0)
    r = lax.broadcasted_iota(jnp.int32, (tm, tm), 0)
    c = lax.broadcasted_iota(jnp.int32, (tm, tm), 1)
    earlier = jnp.where(c < r, 1.0, 0.0).astype(BF16)
    cum = _dot(earlier, onehot.astype(BF16))
    rank1 = jnp.sum(jnp.where(lane == i1, cum, 0.0), axis=-1, keepdims=True).astype(jnp.int32)
    rank2 = jnp.sum(jnp.where(lane == i2, cum, 0.0), axis=-1, keepdims=True).astype(jnp.int32)
    idx_ref[...] = jnp.where(lane == 0, i1, jnp.where(lane == 1, i2, jnp.where(
        lane == 2, rank1, jnp.where(lane == 3, rank2, 0))))
    wt_ref[...] = jnp.where(lane == 0, w1, jnp.where(lane == 1, w2, 0.0))
    counts = jnp.sum(onehot, axis=0, keepdims=True).astype(jnp.int32)
    cnt_ref[...] = jnp.broadcast_to(counts, cnt_ref.shape)


def _router(x, nw, wr, br, tm):
    n = x.shape[0]
    row = lambda i: (i, 0)
    full = lambda i: (0, 0)
    return pl.pallas_call(
        _router_kernel,
        grid=(n // tm,),
        in_specs=[
            pl.BlockSpec((tm, D_MODEL), row),
            pl.BlockSpec((1, D_MODEL), full),
            pl.BlockSpec((D_MODEL, LANES), full),
            pl.BlockSpec((1, LANES), full),
        ],
        out_specs=[
            pl.BlockSpec((tm, D_MODEL), row),
            pl.BlockSpec((tm, LANES), row),
            pl.BlockSpec((tm, LANES), row),
            pl.BlockSpec((None, SUBLANES, LANES), lambda i: (i, 0, 0)),
        ],
        out_shape=[
            jax.ShapeDtypeStruct((n, D_MODEL), F32),
            jax.ShapeDtypeStruct((n, LANES), jnp.int32),
            jax.ShapeDtypeStruct((n, LANES), F32),
            jax.ShapeDtypeStruct((n // tm, SUBLANES, LANES), jnp.int32),
        ],
        compiler_params=pltpu.CompilerParams(
            dimension_semantics=("parallel",), vmem_limit_bytes=VMEM_LIMIT),
        name="router",
    )(x, nw, wr, br)


def _moe_plan(idx, cnt, tm_tok, tm_exp):
    n = idx.shape[0]
    nt = 2 * n // tm_exp + N_EXPERTS
    counts = cnt[:, 0, :N_EXPERTS]
    cpad = (jnp.sum(counts, axis=0) + tm_exp - 1) // tm_exp * tm_exp
    ends = jnp.cumsum(cpad)
    base = (ends - cpad)[None] + jnp.cumsum(counts, axis=0) - counts
    base_tok = jnp.repeat(base, tm_tok, axis=0)[:, None, :]
    hit = idx[:, :2, None] == jnp.arange(N_EXPERTS, dtype=jnp.int32)
    dest = (jnp.sum(jnp.where(hit, base_tok, 0), axis=-1) + idx[:, 2:4]).astype(jnp.int32)
    tile_start = jnp.arange(nt, dtype=jnp.int32) * tm_exp
    tile_e = jnp.minimum(jnp.sum(tile_start[:, None] >= ends[None, :], axis=1), N_EXPERTS - 1).astype(jnp.int32)
    tile_valid = (tile_start < ends[-1]).astype(jnp.int32)
    return dest, tile_e, tile_valid


def _slot_major(dest, tm):
    nt = dest.shape[0] // tm
    return jnp.swapaxes(dest.reshape(nt, tm, 2), 1, 2).reshape(nt, 1, 2 * tm)


def _dispatch_kernel(d_ref, xn_ref, xs_in_hbm, xs_hbm, sem):
    del xs_in_hbm
    tm = xn_ref.shape[0]

    def copy(i, j, dst_row):
        return pltpu.make_async_copy(xn_ref.at[pl.ds(i, 1), :], xs_hbm.at[pl.ds(dst_row, 1), :], sem.at[j])

    def start(i, carry):
        for j in range(2):
            copy(i, j, d_ref[0, j * tm + i]).start(priority=j)
        return carry

    def wait(i, carry):
        for j in range(2):
            copy(i, j, 0).wait()
        return carry

    lax.fori_loop(0, tm, start, 0, unroll=8)
    lax.fori_loop(0, tm, wait, 0, unroll=8)


def _dispatch(xn, dest, xs_buf, tm):
    n = xn.shape[0]
    nrows = xs_buf.shape[0]
    return pl.pallas_call(
        _dispatch_kernel,
        grid=(n // tm,),
        in_specs=[
            pl.BlockSpec((None, 1, 2 * tm), lambda t: (t, 0, 0), memory_space=pltpu.SMEM),
            pl.BlockSpec((tm, D_MODEL), lambda t: (t, 0)),
            pl.BlockSpec(memory_space=pl.ANY),
        ],
        out_specs=pl.BlockSpec(memory_space=pl.ANY),
        out_shape=jax.ShapeDtypeStruct((nrows, D_MODEL), F32),
        scratch_shapes=[pltpu.SemaphoreType.DMA((2,))],
        input_output_aliases={2: 0},
        compiler_params=pltpu.CompilerParams(
            dimension_semantics=("arbitrary",), vmem_limit_bytes=VMEM_LIMIT),
        name="dispatch",
    )(_slot_major(dest, tm), xn, xs_buf)


def _row_gather(idx_ref, src_hbm, buf, sem, slot, nrows):
    def body(r, carry):
        for j in range(2):
            rr = 2 * r + j
            pltpu.make_async_copy(src_hbm.at[pl.ds(idx_ref[0, rr], 1), :],
                                  buf.at[slot, pl.ds(rr, 1), :], sem.at[slot]).start(priority=j)
        return carry
    lax.fori_loop(0, nrows // 2, body, 0, unroll=8)


def _row_gather_wait(src_hbm, buf, sem, slot, nrows):
    def body(r, carry):
        pltpu.make_async_copy(src_hbm.at[pl.ds(0, 1), :],
                              buf.at[slot, pl.ds(r, 1), :], sem.at[slot]).wait()
        return carry
    lax.fori_loop(0, nrows, body, 0, unroll=8)


def _expert_kernel(te_ref, tv_ref, xs_ref, wg_ref, wu_ref, wd_ref, y_ref):
    t = pl.program_id(0)

    @pl.when(tv_ref[t] == 1)
    def _():
        xb = xs_ref[...].astype(BF16)
        a = _silu(_bdot(xb, wg_ref[...])) * _bdot(xb, wu_ref[...])
        y_ref[...] = _bdot(a, wd_ref[...])

    @pl.when(tv_ref[t] == 0)
    def _():
        y_ref[...] = jnp.zeros_like(y_ref)


def _experts(xs, tile_e, tile_valid, wg, wu, wd, layer, tm):
    w_in_spec = pl.BlockSpec((None, None, D_MODEL, D_FF_E), lambda t, te, tv: (layer, te[t], 0, 0))
    nt = xs.shape[0] // tm
    return pl.pallas_call(
        _expert_kernel,
        grid_spec=pltpu.PrefetchScalarGridSpec(
            num_scalar_prefetch=2,
            grid=(nt,),
            in_specs=[
                pl.BlockSpec((tm, D_MODEL), lambda t, te, tv: (t, 0)),
                w_in_spec,
                w_in_spec,
                pl.BlockSpec((None, None, D_FF_E, D_MODEL), lambda t, te, tv: (layer, te[t], 0, 0)),
            ],
            out_specs=pl.BlockSpec((tm, D_MODEL), lambda t, te, tv: (t, 0)),
        ),
        out_shape=jax.ShapeDtypeStruct((nt * tm, D_MODEL), F32),
        compiler_params=pltpu.CompilerParams(
            dimension_semantics=("arbitrary",), vmem_limit_bytes=VMEM_LIMIT),
        name="experts",
    )(tile_e, tile_valid, xs, wg, wu, wd)


def _combine_kernel(split_tile, d_ref, d_next_ref, y_hbm, x_ref, wt_ref, fnw_ref, *rest):
    out_refs, (ybuf, sem) = rest[:-2], rest[-2:]
    t = pl.program_id(0)
    nt = pl.num_programs(0)
    tm = x_ref.shape[0]
    slot = t & 1

    @pl.when(t == 0)
    def _():
        _row_gather(d_ref, y_hbm, ybuf, sem, 0, 2 * tm)

    @pl.when(t + 1 < nt)
    def _():
        _row_gather(d_next_ref, y_hbm, ybuf, sem, 1 - slot, 2 * tm)

    _row_gather_wait(y_hbm, ybuf, sem, slot, 2 * tm)
    wt = wt_ref[...]
    out = x_ref[...] + wt[:, 0:1] * ybuf[slot, 0:tm, :] + wt[:, 1:2] * ybuf[slot, tm:2 * tm, :]
    if split_tile is None:
        out_refs[0][...] = out
    else:
        res = _rms(out, fnw_ref[...])

        @pl.when(t < split_tile)
        def _():
            out_refs[0][...] = res

        @pl.when(t >= split_tile)
        def _():
            out_refs[1][...] = res


def _combine(x, y, dest, wt, fnw, split_rows, tm):
    n = x.shape[0]
    nt = n // tm
    d = _slot_major(dest, tm)
    if split_rows is None:
        split_tile = None
        out_specs = pl.BlockSpec((tm, D_MODEL), lambda t: (t, 0))
        out_shape = jax.ShapeDtypeStruct((n, D_MODEL), F32)
    else:
        split_tile = split_rows // tm
        assert split_tile * tm == split_rows and 0 < split_tile < nt
        out_specs = [pl.BlockSpec((tm, D_MODEL), lambda t: (jnp.minimum(t, split_tile - 1), 0)),
                     pl.BlockSpec((tm, D_MODEL), lambda t: (jnp.maximum(t - split_tile, 0), 0))]
        out_shape = [jax.ShapeDtypeStruct((split_rows, D_MODEL), F32),
                     jax.ShapeDtypeStruct((n - split_rows, D_MODEL), F32)]
    return pl.pallas_call(
        functools.partial(_combine_kernel, split_tile),
        grid=(nt,),
        in_specs=[
            pl.BlockSpec((None, 1, 2 * tm), lambda t: (t, 0, 0), memory_space=pltpu.SMEM),
            pl.BlockSpec((None, 1, 2 * tm), lambda t: (jnp.minimum(t + 1, nt - 1), 0, 0),
                         memory_space=pltpu.SMEM),
            pl.BlockSpec(memory_space=pl.ANY),
            pl.BlockSpec((tm, D_MODEL), lambda t: (t, 0)),
            pl.BlockSpec((tm, LANES), lambda t: (t, 0)),
            pl.BlockSpec((1, D_MODEL), lambda t: (0, 0)),
        ],
        out_specs=out_specs,
        out_shape=out_shape,
        scratch_shapes=[pltpu.VMEM((2, 2 * tm, D_MODEL), F32), pltpu.SemaphoreType.DMA((2,))],
        compiler_params=pltpu.CompilerParams(
            dimension_semantics=("arbitrary",), vmem_limit_bytes=VMEM_LIMIT),
        name="combine",
    )(d, d, y, x, wt, fnw)


def _moe_rows(n, tm_exp):
    return (2 * n // tm_exp + N_EXPERTS) * tm_exp


def _moe(x, nw, wr, br, wg, wu, wd, layer, fnw, xs_buf, split_rows, tm_tok, tm_exp, tm_cmb):
    xn, idx, wt, cnt = _router(x, nw, wr, br, tm_tok)
    dest, tile_e, tile_valid = _moe_plan(idx, cnt, tm_tok, tm_exp)
    xs = _dispatch(xn, dest, xs_buf, tm_tok)
    y = _experts(xs, tile_e, tile_valid, wg, wu, wd, layer, tm_exp)
    return _combine(x, y, dest, wt, fnw, split_rows, tm_cmb), xs


def _seq_masks(n, lseq):
    r = lax.broadcasted_iota(jnp.int32, (n, n), 0)
    c = lax.broadcasted_iota(jnp.int32, (n, n), 1)
    sh = lseq.bit_length() - 1
    same = (r >> sh) == (c >> sh)
    return same & (r >= c), same & (r > c), r == c


def _valid_rows(n, lseq, t_lo, t_hi):
    row = lax.broadcasted_iota(jnp.int32, (n, 1), 0)
    t = (row & (lseq - 1)) + pl.program_id(1) * lseq
    return (t >= t_lo) & (t < t_hi)


def _pad_rows(a, n):
    return jnp.concatenate([a, jnp.zeros((n - a.shape[0], a.shape[1]), a.dtype)], axis=0)


def _gdn_kernel(Lseq, NS, t_lo, t_hi, conv_row0,
                qkv_ref, z_ref, sm_ref, cin_ref, sin_ref, cw_ref, hp_ref, na_ref, obuf_ref, sbuf_ref,
                o_ref, cout_ref, so_ref, ext_ref):
    del obuf_ref, sbuf_ref
    R = NS * Lseq
    R2 = 2 * R
    npairs = H_A // 2

    @pl.when(pl.program_id(1) == 0)
    def _():
        ext_ref[:, 0:SUBLANES, :] = cin_ref[...]
        so_ref[...] = sin_ref[...]

    valid = _valid_rows(R, Lseq, t_lo, t_hi)
    valid_l = _valid_rows(Lseq, Lseq, t_lo, t_hi)
    for s in range(NS):
        ext_ref[s, SUBLANES:SUBLANES + Lseq, :] = jnp.where(
            valid_l, qkv_ref[s * Lseq:(s + 1) * Lseq, :], 0.0)

    def conv_act(col0):
        pieces = []
        for s in range(NS):
            acc = None
            for i in range(CONV_W):
                r0 = SUBLANES - (CONV_W - 1) + i
                term = cw_ref[i:i + 1, col0:col0 + LANES] * ext_ref[s, r0:r0 + Lseq, col0:col0 + LANES]
                acc = term if acc is None else acc + term
            pieces.append(acc)
        return _silu(pieces[0] if NS == 1 else jnp.concatenate(pieces, axis=0))

    incl, _, _ = _seq_masks(R, Lseq)
    incl2, strict2, diag2 = _seq_masks(R2, Lseq)
    tri = jnp.where(incl, 1.0, 0.0).astype(BF16)
    eye2 = jnp.where(diag2, 1.0, 0.0)

    sm = sm_ref[...]
    beta_all = jnp.where(valid, _sigmoid(sm), 0.0)
    g_all = jnp.where(valid, -jnp.exp(hp_ref[0:1, :]) * _softplus(sm + hp_ref[1:2, :]), 0.0)
    G_all = _mask_dot(tri, g_all)
    GT = jnp.transpose(_pad_rows(G_all, LANES))

    q2, k2, v2, beta2, Gc2, Gr2 = [], [], [], [], [], []
    for p in range(npairs):
        qs, ks, vs = [], [], []
        for h in (2 * p, 2 * p + 1):
            q = conv_act(h * DK_A)
            k = conv_act(H_A * DK_A + h * DK_A)
            qs.append(q * lax.rsqrt(jnp.sum(q * q, axis=-1, keepdims=True) + EPS) * (DK_A ** -0.5))
            ks.append(jnp.where(valid, k * lax.rsqrt(jnp.sum(k * k, axis=-1, keepdims=True) + EPS), 0.0))
            vs.append(conv_act(2 * H_A * DK_A + h * DV_A))
        q2.append(jnp.concatenate(qs, axis=0))
        k2.append(jnp.concatenate(ks, axis=0))
        v2.append(jnp.concatenate(vs, axis=0))
        h0 = 2 * p
        beta2.append(jnp.concatenate([beta_all[:, S_BETA + h0:S_BETA + h0 + 1],
                                      beta_all[:, S_BETA + h0 + 1:S_BETA + h0 + 2]], axis=0))
        Gc2.append(jnp.concatenate([G_all[:, S_DECAY + h0:S_DECAY + h0 + 1],
                                    G_all[:, S_DECAY + h0 + 1:S_DECAY + h0 + 2]], axis=0))
        Gr2.append(jnp.concatenate([GT[S_DECAY + h0:S_DECAY + h0 + 1, :R],
                                    GT[S_DECAY + h0 + 1:S_DECAY + h0 + 2, :R]], axis=1))

    rng = range(npairs)
    kk = [_bdot_nt(k2[p], k2[p]) for p in rng]
    qk = [_bdot_nt(q2[p], k2[p]) for p in rng]
    dec = [jnp.where(incl2, jnp.exp(jnp.where(incl2, Gc2[p] - Gr2[p], 0.0)), 0.0) for p in rng]
    Bp = [-jnp.where(strict2, beta2[p] * kk[p] * dec[p], 0.0) for p in rng]
    P = [eye2 + Bp[p] for p in rng]
    n = 2
    while n < Lseq:
        Bp = [_dot3(b, b) for b in Bp]
        P = [pp + _dot3(pp, b) for pp, b in zip(P, Bp)]
        n *= 2
    eG = [jnp.exp(Gc2[p]) for p in rng]
    rhs = [jnp.concatenate([beta2[p] * v2[p], (beta2[p] * eG[p]) * k2[p]], axis=1) for p in rng]
    sol = [_dot3(P[p], rhs[p]) for p in rng]
    qk = [qk[p] * dec[p] for p in rng]
    qe = [q2[p] * eG[p] for p in rng]

    items = [(p, j, s) for p in rng for j in range(2) for s in range(NS)]

    def rows(j, s):
        return slice(j * R + s * Lseq, j * R + (s + 1) * Lseq)

    S = {it: so_ref[it[2], 2 * it[0] + it[1]] for it in items}
    both = {}
    for p, j, s in items:
        r = rows(j, s)
        both[(p, j, s)] = _bdot(jnp.concatenate([sol[p][r, DV_A:], qe[p][r]], axis=0), S[(p, j, s)])
    vnew = {}
    for p, j, s in items:
        vnew[(p, j, s)] = sol[p][rows(j, s), :DV_A] - both[(p, j, s)][:Lseq]
    for p, j, s in items:
        r = rows(j, s)
        last = j * R + (s + 1) * Lseq - 1
        GL = Gc2[p][last:last + 1]
        kd = k2[p][r] * jnp.exp(GL - Gc2[p][r])
        so_ref[s, 2 * p + j] = S[(p, j, s)] * jnp.exp(GL) + _bdot_tn(kd, vnew[(p, j, s)])
    for p in rng:
        order = [(p, j, s) for j in range(2) for s in range(NS)]
        vnew2 = jnp.concatenate([vnew[it] for it in order], axis=0)
        o2 = jnp.concatenate([both[it][Lseq:] for it in order], axis=0) + _bdot(qk[p], vnew2)
        for j in range(2):
            h = 2 * p + j
            zh = z_ref[:, h * DV_A:(h + 1) * DV_A]
            o_ref[:, h * DV_A:(h + 1) * DV_A] = _rms(o2[j * R:(j + 1) * R], na_ref[...]) * _silu(zh)

    for s in range(NS):
        cout_ref[s] = ext_ref[s, conv_row0:conv_row0 + SUBLANES, :]
        ext_ref[s, 0:SUBLANES, :] = ext_ref[s, Lseq:Lseq + SUBLANES, :]


def _gdn_parts(big, small, cin, sin, layer_in, cw, hp, na, obuf, sbuf, layer, NS, nseq, rb):
    return dict(
        operands=(big, big, small, cin, sin, cw, hp, na, obuf, sbuf),
        in_specs=[
            pl.BlockSpec((ROWS, CONV_CH), lambda b, c: (rb(b, c), 0)),
            pl.BlockSpec((ROWS, H_A * DV_A), lambda b, c: (rb(b, c), C_Z // (H_A * DV_A))),
            pl.BlockSpec((ROWS, LANES), lambda b, c: (rb(b, c), 0)),
            pl.BlockSpec((None, NS, SUBLANES, CONV_CH), lambda b, c: (layer_in, b, 0, 0)),
            pl.BlockSpec((None, NS, H_A, DK_A, DV_A), lambda b, c: (layer_in, b, 0, 0, 0)),
            pl.BlockSpec((SUBLANES, CONV_CH), lambda b, c: (0, 0)),
            pl.BlockSpec((SUBLANES, LANES), lambda b, c: (0, 0)),
            pl.BlockSpec((1, DV_A), lambda b, c: (0, 0)),
            pl.BlockSpec(memory_space=pl.ANY),
            pl.BlockSpec(memory_space=pl.ANY),
        ],
        out_specs=[
            pl.BlockSpec((ROWS, H_A * DV_A), lambda b, c: (rb(b, c), 0)),
            pl.BlockSpec((NS, SUBLANES, CONV_CH), lambda b, c: (b, 0, 0)),
            pl.BlockSpec((None, NS, H_A, DK_A, DV_A), lambda b, c: (layer, b, 0, 0, 0)),
        ],
        out_shape=[
            jax.ShapeDtypeStruct(obuf.shape, F32),
            jax.ShapeDtypeStruct((nseq, SUBLANES, CONV_CH), F32),
            jax.ShapeDtypeStruct(sbuf.shape, F32),
        ],
        aliases={8: 0, 9: 2},
    )


def _gla_kernel(Lseq, NS, SB, t_lo, t_hi,
                qb_ref, kb_ref, vb_ref, rb_ref, sm_ref, sin_ref, wup_ref, bup_ref, nb_ref, obuf_ref,
                sbuf_ref, o_ref, so_ref):
    del obuf_ref, sbuf_ref
    R = NS * Lseq
    heads = range(H_B)

    @pl.when(pl.program_id(1) == 0)
    def _():
        so_ref[...] = sin_ref[...]

    nb = Lseq // SB
    assert nb == 1 or NS == 1
    n_off = SB * nb * (nb - 1) // 2
    sb_shift = SB.bit_length() - 1

    valid = _valid_rows(R, Lseq, t_lo, t_hi)
    r = lax.broadcasted_iota(jnp.int32, (R, R), 0)
    c = lax.broadcasted_iota(jnp.int32, (R, R), 1)
    same = (r | (Lseq - 1)) == (c | (Lseq - 1))
    m_row = jnp.where(same & (c <= r), 1.0, 0.0)
    m_first = jnp.where(same & (c <= (r & ~(SB - 1))), 1.0, 0.0)
    m_last = jnp.where(same, 1.0, 0.0)
    masks = jnp.concatenate([m_row, m_first, m_last], axis=0).astype(BF16)

    pre = _dot3(sm_ref[...], wup_ref[...]) + bup_ref[...]
    lg = jnp.where(valid, -_softplus(-pre) / GATE_TAU, 0.0)
    cums = _mask_dot(masks, lg)
    Bc_all, b0_all, BL_all = cums[:R], cums[R:2 * R], cums[2 * R:]

    re = lax.broadcasted_iota(jnp.int32, (R, n_off + R), 0)
    ce = lax.broadcasted_iota(jnp.int32, (R, n_off + R), 1)
    rblk = (re & (Lseq - 1)) >> sb_shift
    cd = ce - n_off
    see = ((ce >= (SB // 2) * rblk * (rblk - 1)) & (ce < (SB // 2) * rblk * (rblk + 1))) | (
        (cd >= 0) & ((cd >> sb_shift) == (re >> sb_shift)) & (cd <= re))

    def hs(a, h, w):
        return a[:, h * w:(h + 1) * w]

    q = [hs(qb_ref, h, DK_B) * (DK_B ** -0.5) for h in heads]
    k = [jnp.where(valid, hs(kb_ref, h, DK_B), 0.0) for h in heads]
    v = [hs(vb_ref, h, DV_B) for h in heads]
    Bc = [hs(Bc_all, h, DK_B) for h in heads]
    b0 = [hs(b0_all, h, DK_B) for h in heads]
    BL = [hs(BL_all, h, DK_B) for h in heads]
    qs = [q[h] * jnp.exp(Bc[h] - b0[h]) for h in heads]
    kx, vx = [], []
    for h in heads:
        kparts = [k[h][:i * SB] * jnp.exp(Bc[h][i * SB:i * SB + 1] - Bc[h][:i * SB]) for i in range(1, nb)]
        kparts.append(k[h] * jnp.exp(b0[h] - Bc[h]))
        vparts = [v[h][:i * SB] for i in range(1, nb)] + [v[h]]
        kx.append(kparts[0] if nb == 1 else jnp.concatenate(kparts, axis=0))
        vx.append(vparts[0] if nb == 1 else jnp.concatenate(vparts, axis=0))
    Ax = [jnp.where(see, _dot3(qs[h], kx[h], _dot_nt), 0.0) for h in heads]
    o_intra = [_bdot(Ax[h], vx[h]) for h in heads]
    qe = [q[h] * jnp.exp(Bc[h]) for h in heads]
    kd = [k[h] * jnp.exp(BL[h] - Bc[h]) for h in heads]
    ET = [jnp.transpose(_pad_rows(jnp.exp(BL[h]), LANES)) for h in heads]

    items = [(h, s) for h in heads for s in range(NS)]
    S = {it: so_ref[it[1], it[0]] for it in items}
    o_inter = {}
    for h, s in items:
        o_inter[(h, s)] = _bdot(qe[h][s * Lseq:(s + 1) * Lseq], S[(h, s)])
    for h, s in items:
        rows = slice(s * Lseq, (s + 1) * Lseq)
        so_ref[s, h] = S[(h, s)] * ET[h][:, s * Lseq:s * Lseq + 1] + _bdot_tn(kd[h][rows], v[h][rows])
    for h in heads:
        parts = [o_inter[(h, s)] for s in range(NS)]
        o = (parts[0] if NS == 1 else jnp.concatenate(parts, axis=0)) + o_intra[h]
        rh = hs(rb_ref, h, DV_B)
        o_ref[:, h * DV_B:(h + 1) * DV_B] = _rms(o, nb_ref[...]) * _silu(rh)


def _gla_parts(big, small, sin, layer_in, wup, bup, nb, obuf, sbuf, layer, NS, r):
    return dict(
        operands=(big, big, big, big, small, sin, wup, bup, nb, obuf, sbuf),
        in_specs=[
            pl.BlockSpec((ROWS, H_B * DK_B), lambda b, c: (r(b, c), C_QB // (H_B * DK_B))),
            pl.BlockSpec((ROWS, H_B * DK_B), lambda b, c: (r(b, c), C_KB // (H_B * DK_B))),
            pl.BlockSpec((ROWS, H_B * DV_B), lambda b, c: (r(b, c), C_VB // (H_B * DV_B))),
            pl.BlockSpec((ROWS, H_B * DV_B), lambda b, c: (r(b, c), C_RB // (H_B * DV_B))),
            pl.BlockSpec((ROWS, LANES), lambda b, c: (r(b, c), 0)),
            pl.BlockSpec((None, NS, H_B, DK_B, DV_B), lambda b, c: (layer_in, b, 0, 0, 0)),
            pl.BlockSpec((LANES, H_B * DK_B), lambda b, c: (0, 0)),
            pl.BlockSpec((1, H_B * DK_B), lambda b, c: (0, 0)),
            pl.BlockSpec((1, DV_B), lambda b, c: (0, 0)),
            pl.BlockSpec(memory_space=pl.ANY),
            pl.BlockSpec(memory_space=pl.ANY),
        ],
        out_specs=[
            pl.BlockSpec((ROWS, H_B * DV_B), lambda b, c: (r(b, c), 0)),
            pl.BlockSpec((None, NS, H_B, DK_B, DV_B), lambda b, c: (layer, b, 0, 0, 0)),
        ],
        out_shape=[
            jax.ShapeDtypeStruct(obuf.shape, F32),
            jax.ShapeDtypeStruct(sbuf.shape, F32),
        ],
        aliases={9: 0, 10: 1},
    )


def _mixer_kernel(Lseq, NS, SB, t_lo, t_hi, conv_row0, n_in_a, n_in_b, n_out_a, *refs):
    in_a, in_b = refs[:n_in_a], refs[n_in_a:n_in_a + n_in_b]
    outs = refs[n_in_a + n_in_b:-1]
    _gdn_kernel(Lseq, NS, t_lo, t_hi, conv_row0, *in_a, *outs[:n_out_a], refs[-1])
    _gla_kernel(Lseq, NS, SB, t_lo, t_hi, *in_b, *outs[n_out_a:])


def _mixer(gdn, gla, *, nsteps, Lseq, NS, SB, nchunks, t_lo, t_hi):
    assert NS * Lseq == ROWS
    n_in_a, n_out_a = len(gdn["operands"]), len(gdn["out_specs"])
    aliases = dict(gdn["aliases"])
    aliases.update({n_in_a + i: n_out_a + o for i, o in gla["aliases"].items()})
    conv_row0 = t_hi - (nchunks - 1) * Lseq
    kern = functools.partial(_mixer_kernel, Lseq, NS, SB, t_lo, t_hi, conv_row0,
                             n_in_a, len(gla["operands"]), n_out_a)
    return pl.pallas_call(
        kern,
        grid=(nsteps, nchunks),
        in_specs=gdn["in_specs"] + gla["in_specs"],
        out_specs=gdn["out_specs"] + gla["out_specs"],
        out_shape=gdn["out_shape"] + gla["out_shape"],
        scratch_shapes=[pltpu.VMEM((NS, SUBLANES + Lseq, CONV_CH), F32)],
        input_output_aliases=aliases,
        compiler_params=pltpu.CompilerParams(
            dimension_semantics=("parallel", "arbitrary"), vmem_limit_bytes=VMEM_LIMIT),
        name="mixer_L%d" % Lseq,
    )(*gdn["operands"], *gla["operands"])


def kernel(x_prompt, x_sample, state_delta, state_conv, state_gla, meta_tokens, norm_mix, w_in, conv_w, a_log, dt_bias, norm_a, gla_gate_up, gla_gate_bias, norm_b, w_proj_a, w_proj_b, w_out, norm_ffn, router_group_w, router_group_b, router_expert_w, router_expert_b, expert_w_gate, expert_w_up, expert_w_down, norm_final):
    Bp, Tp, D = x_prompt.shape
    Bs, Ts, _ = x_sample.shape
    depth = w_in.shape[0]
    assert N_META <= CHUNK and Tp % CHUNK == 0 and CHUNK == ROWS
    pad_p = CHUNK - N_META
    Tpp = CHUNK + Tp
    ncp = Tpp // CHUNK
    n_prompt = Bp * Tp
    n_head = Bp * CHUNK
    ns_rows = Bs * SAMPLE_L
    n = n_prompt + n_head + ns_rows
    seq_per_step = ROWS // SAMPLE_L
    dt = x_prompt.dtype
    head = jnp.concatenate([
        jnp.zeros((Bp, pad_p, D), dt),
        jnp.broadcast_to(meta_tokens.astype(dt)[None], (Bp, N_META, D))], axis=1).reshape(n_head, D)
    xs = jnp.pad(x_sample, ((0, 0), (0, SAMPLE_L - Ts), (0, 0))).reshape(ns_rows, D)
    x = jnp.concatenate([x_prompt.reshape(n_prompt, D), head, xs.astype(dt)], axis=0).astype(F32)

    tm_big = 1280 if n % 1280 == 0 else 512
    tm = 512

    zeros_conv = jnp.zeros((1, Bp, SUBLANES, CONV_CH), F32)
    zeros_sa = jnp.zeros((1, Bp, H_A, DK_A, DV_A), F32)
    zeros_sb = jnp.zeros((1, Bp, H_B, DK_B, DV_B), F32)
    head_block0 = n_prompt // ROWS
    sample_block0 = (n_prompt + n_head) // ROWS
    prompt_args = dict(nsteps=Bp, Lseq=CHUNK, NS=1, nchunks=ncp, t_lo=pad_p, t_hi=Tpp)
    sample_args = dict(nsteps=Bs // seq_per_step, Lseq=SAMPLE_L, NS=seq_per_step, nchunks=1, t_lo=0, t_hi=Ts)
    prompt_rb = lambda b, c: jnp.where(c == 0, head_block0 + b, b * (ncp - 1) + c - 1)
    sample_rb = lambda b, c: sample_block0 + b

    wbig_all, wsmall_all = _wprep(w_in)
    wg_all, wu_all, wd_all = expert_w_gate.astype(F32), expert_w_up.astype(F32), expert_w_down.astype(F32)
    wa_all, wb_all, wo_all = w_proj_a.astype(BF16), w_proj_b.astype(BF16), w_out.astype(BF16)
    oa = jnp.zeros((n, H_A * DV_A), F32)
    ob = jnp.zeros((n, H_B * DV_B), F32)
    xs_buf = jnp.zeros((_moe_rows(n, MOE_TILE), D), F32)
    sa_p = jnp.zeros((depth, Bp, H_A, DK_A, DV_A), F32)
    sb_p = jnp.zeros((depth, Bp, H_B, DK_B, DV_B), F32)
    sa_s = jnp.zeros((depth, Bs, H_A, DK_A, DV_A), F32)
    sb_s = jnp.zeros((depth, Bs, H_B, DK_B, DV_B), F32)
    sa_in, sb_in = state_delta.astype(F32), state_gla.astype(F32)
    cin_s = jnp.pad(state_conv.astype(F32), ((0, 0), (0, 0), (SUBLANES - (CONV_W - 1), 0), (0, 0)))

    conv_p, conv_s = [], []
    for l in range(depth):
        big, small = _inproj(x, norm_mix[l][None].astype(F32), wbig_all, wsmall_all, l, tm_big, 1024)

        cw = jnp.pad(conv_w[l].astype(F32), ((0, SUBLANES - CONV_W), (0, 0)))
        hp = jnp.zeros((SUBLANES, LANES), F32)
        hp = hp.at[0, S_DECAY:S_DECAY + H_A].set(a_log[l].astype(F32))
        hp = hp.at[1, S_DECAY:S_DECAY + H_A].set(dt_bias[l].astype(F32))
        na = norm_a[l][None].astype(F32)
        wup = jnp.zeros((LANES, H_B * DK_B), F32).at[S_LR:S_LR + GATE_RANK].set(gla_gate_up[l].astype(F32))
        bup = gla_gate_bias[l][None].astype(F32)
        nb = norm_b[l][None].astype(F32)

        oa, c_p, sa_p, ob, sb_p = _mixer(
            _gdn_parts(big, small, zeros_conv, zeros_sa, 0, cw, hp, na, oa, sa_p, l, 1, Bp, prompt_rb),
            _gla_parts(big, small, zeros_sb, 0, wup, bup, nb, ob, sb_p, l, 1, prompt_rb),
            SB=GLA_SUB, **prompt_args)
        oa, c_s, sa_s, ob, sb_s = _mixer(
            _gdn_parts(big, small, cin_s, sa_in, l, cw, hp, na, oa, sa_s, l, seq_per_step, Bs, sample_rb),
            _gla_parts(big, small, sb_in, l, wup, bup, nb, ob, sb_s, l, seq_per_step, sample_rb),
            SB=SAMPLE_L, **sample_args)
        x = _outproj(oa, ob, big, x, wa_all, wb_all, wo_all, l, tm)

        wr = jnp.concatenate([router_expert_w[l], router_group_w[l],
                              jnp.zeros((D, LANES - N_EXPERTS - N_GROUPS), F32)], axis=1).astype(F32)
        br = jnp.concatenate([router_expert_b[l], router_group_b[l],
                              jnp.zeros((LANES - N_EXPERTS - N_GROUPS,), F32)])[None].astype(F32)
        x, xs_buf = _moe(x, norm_ffn[l][None].astype(F32), wr, br, wg_all, wu_all, wd_all, l,
                         norm_final[None].astype(F32), xs_buf,
                         n_prompt if l == depth - 1 else None, tm, MOE_TILE, COMBINE_TILE)

        conv_p.append(c_p[:, SUBLANES - (CONV_W - 1):])
        conv_s.append(c_s[:, SUBLANES - (CONV_W - 1):])

    y_first, y_rest = x
    y_prompt = y_first.reshape(Bp, Tp, D).astype(dt)
    y_sample = y_rest[n_head:].reshape(Bs, SAMPLE_L, D)[:, :Ts].astype(x_sample.dtype)
    return (y_prompt, y_sample,
            sa_p.astype(dt), jnp.stack(conv_p).astype(dt), sb_p.astype(dt),
            sa_s.astype(state_delta.dtype), jnp.stack(conv_s).astype(state_conv.dtype),
            sb_s.astype(state_gla.dtype))
```

```python
import functools

import jax
import jax.numpy as jnp
from jax import lax
from jax.experimental import pallas as pl
from jax.experimental.pallas import tpu as pltpu

F32 = jnp.float32
BF16 = jnp.bfloat16

D_MODEL = 1024
N_META = 16
CHUNK = 64
EPS = 1e-6
H_A, DK_A, DV_A = 8, 128, 128
CONV_W = 4
CONV_CH = 2 * H_A * DK_A + H_A * DV_A
H_B, DK_B, DV_B = 4, 128, 256
GATE_RANK = 16
GATE_TAU = 16.0
N_GROUPS = 4
EXPERTS_PER_GROUP = 8
N_EXPERTS = N_GROUPS * EXPERTS_PER_GROUP
D_FF_E = 256

LANES = 128
SUBLANES = 8
VMEM_LIMIT = 48 * 1024 * 1024

C_QKV = 0
C_Z = C_QKV + CONV_CH
C_QB = C_Z + H_A * DV_A
C_KB = C_QB + H_B * DK_B
C_VB = C_KB + H_B * DK_B
C_RB = C_VB + H_B * DV_B
C_GA = C_RB + H_B * DV_B
C_GB = C_GA + D_MODEL
C_BIG = C_GB + D_MODEL
S_BETA = 0
S_DECAY = H_A
S_LR = 2 * H_A

SAMPLE_L = SUBLANES
ROWS = CHUNK
GLA_SUB = 16
MOE_TILE = 512
COMBINE_TILE = 256


def _dot(a, b, prec=None):
    return jnp.dot(a, b, preferred_element_type=F32, precision=prec)


def _dot_nt(a, b, prec=None):
    return lax.dot_general(a, b, (((1,), (1,)), ((), ())), preferred_element_type=F32, precision=prec)


def _dot_tn(a, b, prec=None):
    return lax.dot_general(a, b, (((0,), (0,)), ((), ())), preferred_element_type=F32, precision=prec)


def _bdot(a, b):
    return _dot(a.astype(BF16), b.astype(BF16))


def _bdot_nt(a, b):
    return _dot_nt(a.astype(BF16), b.astype(BF16))


def _bdot_tn(a, b):
    return _dot_tn(a.astype(BF16), b.astype(BF16))


def _split2(a):
    hi = a.astype(BF16)
    return hi, (a - hi.astype(F32)).astype(BF16)


def _dot3(a, b, dot=_dot):
    ah, al = _split2(a)
    bh, bl = _split2(b)
    return dot(ah, bh) + dot(al, bh) + dot(ah, bl)


def _mask_dot(m, b):
    hi = b.astype(BF16)
    r = b - hi.astype(F32)
    mid = r.astype(BF16)
    lo = (r - mid.astype(F32)).astype(BF16)
    return _dot(m, hi) + _dot(m, mid) + _dot(m, lo)


def _sigmoid(x):
    return 1.0 / (1.0 + jnp.exp(-x))


def _silu(x):
    return x * _sigmoid(x)


def _softplus(x):
    return jnp.maximum(x, 0.0) + jnp.log(1.0 + jnp.exp(-jnp.abs(x)))


def _rms(x, w):
    return x * lax.rsqrt(jnp.mean(x * x, axis=-1, keepdims=True) + EPS) * w


O_BA = C_Z + H_A * DV_A
O_QB = O_BA + 2 * H_A
O_LR = O_QB + 2 * H_B * DK_B + 2 * H_B * DV_B
O_GA = O_LR + GATE_RANK
W_ROWS = 128


def _wprep_kernel(w_ref, big_ref, small_ref):
    w = w_ref[...]
    big_ref[...] = jnp.concatenate([w[:, :O_BA], w[:, O_QB:O_LR], w[:, O_GA:]], axis=1).astype(BF16)
    pad = jnp.zeros((w.shape[0], LANES - 2 * H_A - GATE_RANK), F32)
    small_ref[...] = jnp.concatenate([w[:, O_BA:O_QB], w[:, O_LR:O_GA], pad], axis=1)


def _wprep(w_in):
    depth, d, ncols = w_in.shape
    return pl.pallas_call(
        _wprep_kernel,
        grid=(depth, d // W_ROWS),
        in_specs=[pl.BlockSpec((None, W_ROWS, ncols), lambda l, i: (l, i, 0))],
        out_specs=[
            pl.BlockSpec((None, W_ROWS, C_BIG), lambda l, i: (l, i, 0)),
            pl.BlockSpec((None, W_ROWS, LANES), lambda l, i: (l, i, 0)),
        ],
        out_shape=[
            jax.ShapeDtypeStruct((depth, d, C_BIG), BF16),
            jax.ShapeDtypeStruct((depth, d, LANES), F32),
        ],
        compiler_params=pltpu.CompilerParams(
            dimension_semantics=("parallel", "parallel"), vmem_limit_bytes=VMEM_LIMIT),
        name="wprep",
    )(w_in.astype(F32))

def _inproj_kernel(x_ref, nw_ref, wb_ref, ws_ref, big_ref, small_ref, xn_ref):
    @pl.when(pl.program_id(1) == 0)
    def _():
        xn = _rms(x_ref[...], nw_ref[...])
        xn_ref[...] = xn.astype(BF16)
        small_ref[...] = _dot3(xn, ws_ref[...])

    big_ref[...] = _dot(xn_ref[...], wb_ref[...])


def _inproj(x, nw, wbig, wsmall, layer, tm, tn):
    n = x.shape[0]
    return pl.pallas_call(
        _inproj_kernel,
        grid=(n // tm, C_BIG // tn),
        in_specs=[
            pl.BlockSpec((tm, D_MODEL), lambda i, j: (i, 0)),
            pl.BlockSpec((1, D_MODEL), lambda i, j: (0, 0)),
            pl.BlockSpec((None, D_MODEL, tn), lambda i, j: (layer, 0, j)),
            pl.BlockSpec((None, D_MODEL, LANES), lambda i, j: (layer, 0, 0)),
        ],
        out_specs=[
            pl.BlockSpec((tm, tn), lambda i, j: (i, j)),
            pl.BlockSpec((tm, LANES), lambda i, j: (i, 0)),
        ],
        out_shape=[
            jax.ShapeDtypeStruct((n, C_BIG), F32),
            jax.ShapeDtypeStruct((n, LANES), F32),
        ],
        scratch_shapes=[pltpu.VMEM((tm, D_MODEL), BF16)],
        compiler_params=pltpu.CompilerParams(
            dimension_semantics=("parallel", "arbitrary"), vmem_limit_bytes=VMEM_LIMIT),
        name="inproj",
    )(x, nw, wbig, wsmall)


def _outproj_kernel(oa_ref, ob_ref, ga_ref, gb_ref, x_ref, wa_ref, wb_ref, wo_ref, out_ref):
    ya = _dot(oa_ref[...].astype(BF16), wa_ref[...])
    yb = _dot(ob_ref[...].astype(BF16), wb_ref[...])
    merged = _sigmoid(ga_ref[...]) * ya + _sigmoid(gb_ref[...]) * yb
    out_ref[...] = x_ref[...] + _dot(merged.astype(BF16), wo_ref[...])


def _outproj(oa, ob, big, x, wa, wb, wo, layer, tm):
    wspec = pl.BlockSpec((None, D_MODEL, D_MODEL), lambda i: (layer, 0, 0))
    n = x.shape[0]
    row = lambda i: (i, 0)
    return pl.pallas_call(
        _outproj_kernel,
        grid=(n // tm,),
        in_specs=[
            pl.BlockSpec((tm, D_MODEL), row),
            pl.BlockSpec((tm, D_MODEL), row),
            pl.BlockSpec((tm, D_MODEL), lambda i: (i, C_GA // D_MODEL)),
            pl.BlockSpec((tm, D_MODEL), lambda i: (i, C_GB // D_MODEL)),
            pl.BlockSpec((tm, D_MODEL), row),
            wspec,
            wspec,
            wspec,
        ],
        out_specs=pl.BlockSpec((tm, D_MODEL), row),
        out_shape=jax.ShapeDtypeStruct((n, D_MODEL), F32),
        compiler_params=pltpu.CompilerParams(
            dimension_semantics=("parallel",), vmem_limit_bytes=VMEM_LIMIT),
        name="outproj",
    )(oa, ob, big, big, x, wa, wb, wo)


def _route(logits):
    lane = lax.broadcasted_iota(jnp.int32, logits.shape, 1)
    neg = jnp.float32(-jnp.inf)
    big = jnp.int32(1 << 20)
    gmask = (lane >= N_EXPERTS) & (lane < N_EXPERTS + N_GROUPS)
    gl = jnp.where(gmask, logits, neg)
    gmax = jnp.max(gl, axis=-1, keepdims=True)
    gsel = jnp.min(jnp.where(gl == gmax, lane - N_EXPERTS, big), axis=-1, keepdims=True)
    pgsel = 1.0 / jnp.sum(jnp.where(gmask, jnp.exp(gl - gmax), 0.0), axis=-1, keepdims=True)
    lo = gsel * EXPERTS_PER_GROUP
    emask = (lane >= lo) & (lane < lo + EXPERTS_PER_GROUP)
    el = jnp.where(emask, logits, neg)
    m1 = jnp.max(el, axis=-1, keepdims=True)
    i1 = jnp.min(jnp.where(el == m1, lane, big), axis=-1, keepdims=True)
    el2 = jnp.where(lane == i1, neg, el)
    m2 = jnp.max(el2, axis=-1, keepdims=True)
    i2 = jnp.min(jnp.where(el2 == m2, lane, big), axis=-1, keepdims=True)
    r = jnp.exp(m2 - m1)
    w1 = pgsel / (1.0 + r)
    w2 = pgsel * r / (1.0 + r)
    return i1, i2, w1, w2


def _router_kernel(x_ref, nw_ref, wr_ref, br_ref, xn_ref, idx_ref, wt_ref, cnt_ref):
    tm = x_ref.shape[0]
    xn = _rms(x_ref[...], nw_ref[...])
    xn_ref[...] = xn
    i1, i2, w1, w2 = _route(_dot3(xn, wr_ref[...]) + br_ref[...])
    lane = lax.broadcasted_iota(jnp.int32, (tm, LANES), 1)
    onehot = jnp.where((lane == i1) | (lane == i2), 1.0, 0.0)
    r = lax.broadcasted_iota(jnp.int32, (tm, tm), 0)
    c = lax.broadcasted_iota(jnp.int32, (tm, tm), 1)
    earlier = jnp.where(c < r, 1.0, 0.0).astype(BF16)
    cum = _dot(earlier, onehot.astype(BF16))
    rank1 = jnp.sum(jnp.where(lane == i1, cum, 0.0), axis=-1, keepdims=True).astype(jnp.int32)
    rank2 = jnp.sum(jnp.where(lane == i2, cum, 0.0), axis=-1, keepdims=True).astype(jnp.int32)
    idx_ref[...] = jnp.where(lane == 0, i1, jnp.where(lane == 1, i2, jnp.where(
        lane == 2, rank1, jnp.where(lane == 3, rank2, 0))))
    wt_ref[...] = jnp.where(lane == 0, w1, jnp.where(lane == 1, w2, 0.0))
    counts = jnp.sum(onehot, axis=0, keepdims=True).astype(jnp.int32)
    cnt_ref[...] = jnp.broadcast_to(counts, cnt_ref.shape)


def _router(x, nw, wr, br, tm):
    n = x.shape[0]
    row = lambda i: (i, 0)
    full = lambda i: (0, 0)
    return pl.pallas_call(
        _router_kernel,
        grid=(n // tm,),
        in_specs=[
            pl.BlockSpec((tm, D_MODEL), row),
            pl.BlockSpec((1, D_MODEL), full),
            pl.BlockSpec((D_MODEL, LANES), full),
            pl.BlockSpec((1, LANES), full),
        ],
        out_specs=[
            pl.BlockSpec((tm, D_MODEL), row),
            pl.BlockSpec((tm, LANES), row),
            pl.BlockSpec((tm, LANES), row),
            pl.BlockSpec((None, SUBLANES, LANES), lambda i: (i, 0, 0)),
        ],
        out_shape=[
            jax.ShapeDtypeStruct((n, D_MODEL), F32),
            jax.ShapeDtypeStruct((n, LANES), jnp.int32),
            jax.ShapeDtypeStruct((n, LANES), F32),
            jax.ShapeDtypeStruct((n // tm, SUBLANES, LANES), jnp.int32),
        ],
        compiler_params=pltpu.CompilerParams(
            dimension_semantics=("parallel",), vmem_limit_bytes=VMEM_LIMIT),
        name="router",
    )(x, nw, wr, br)


def _moe_plan(idx, cnt, tm_tok, tm_exp):
    n = idx.shape[0]
    nt = 2 * n // tm_exp + N_EXPERTS
    counts = cnt[:, 0, :N_EXPERTS]
    cpad = (jnp.sum(counts, axis=0) + tm_exp - 1) // tm_exp * tm_exp
    ends = jnp.cumsum(cpad)
    base = (ends - cpad)[None] + jnp.cumsum(counts, axis=0) - counts
    base_tok = jnp.repeat(base, tm_tok, axis=0)[:, None, :]
    hit = idx[:, :2, None] == jnp.arange(N_EXPERTS, dtype=jnp.int32)
    dest = (jnp.sum(jnp.where(hit, base_tok, 0), axis=-1) + idx[:, 2:4]).astype(jnp.int32)
    tile_start = jnp.arange(nt, dtype=jnp.int32) * tm_exp
    tile_e = jnp.minimum(jnp.sum(tile_start[:, None] >= ends[None, :], axis=1), N_EXPERTS - 1).astype(jnp.int32)
    tile_valid = (tile_start < ends[-1]).astype(jnp.int32)
    return dest, tile_e, tile_valid


def _slot_major(dest, tm):
    nt = dest.shape[0] // tm
    return jnp.swapaxes(dest.reshape(nt, tm, 2), 1, 2).reshape(nt, 1, 2 * tm)


def _dispatch_kernel(d_ref, xn_ref, xs_in_hbm, xs_hbm, sem):
    del xs_in_hbm
    tm = xn_ref.shape[0]

    def copy(i, j, dst_row):
        return pltpu.make_async_copy(xn_ref.at[pl.ds(i, 1), :], xs_hbm.at[pl.ds(dst_row, 1), :], sem.at[j])

    def start(i, carry):
        for j in range(2):
            copy(i, j, d_ref[0, j * tm + i]).start(priority=j)
        return carry

    def wait(i, carry):
        for j in range(2):
            copy(i, j, 0).wait()
        return carry

    lax.fori_loop(0, tm, start, 0, unroll=8)
    lax.fori_loop(0, tm, wait, 0, unroll=8)


def _dispatch(xn, dest, xs_buf, tm):
    n = xn.shape[0]
    nrows = xs_buf.shape[0]
    return pl.pallas_call(
        _dispatch_kernel,
        grid=(n // tm,),
        in_specs=[
            pl.BlockSpec((None, 1, 2 * tm), lambda t: (t, 0, 0), memory_space=pltpu.SMEM),
            pl.BlockSpec((tm, D_MODEL), lambda t: (t, 0)),
            pl.BlockSpec(memory_space=pl.ANY),
        ],
        out_specs=pl.BlockSpec(memory_space=pl.ANY),
        out_shape=jax.ShapeDtypeStruct((nrows, D_MODEL), F32),
        scratch_shapes=[pltpu.SemaphoreType.DMA((2,))],
        input_output_aliases={2: 0},
        compiler_params=pltpu.CompilerParams(
            dimension_semantics=("arbitrary",), vmem_limit_bytes=VMEM_LIMIT),
        name="dispatch",
    )(_slot_major(dest, tm), xn, xs_buf)


def _row_gather(idx_ref, src_hbm, buf, sem, slot, nrows):
    def body(r, carry):
        for j in range(2):
            rr = 2 * r + j
            pltpu.make_async_copy(src_hbm.at[pl.ds(idx_ref[0, rr], 1), :],
                                  buf.at[slot, pl.ds(rr, 1), :], sem.at[slot]).start(priority=j)
        return carry
    lax.fori_loop(0, nrows // 2, body, 0, unroll=8)


def _row_gather_wait(src_hbm, buf, sem, slot, nrows):
    def body(r, carry):
        pltpu.make_async_copy(src_hbm.at[pl.ds(0, 1), :],
                              buf.at[slot, pl.ds(r, 1), :], sem.at[slot]).wait()
        return carry
    lax.fori_loop(0, nrows, body, 0, unroll=8)


def _expert_kernel(te_ref, tv_ref, xs_ref, wg_ref, wu_ref, wd_ref, y_ref):
    t = pl.program_id(0)

    @pl.when(tv_ref[t] == 1)
    def _():
        xb = xs_ref[...].astype(BF16)
        a = _silu(_bdot(xb, wg_ref[...])) * _bdot(xb, wu_ref[...])
        y_ref[...] = _bdot(a, wd_ref[...])

    @pl.when(tv_ref[t] == 0)
    def _():
        y_ref[...] = jnp.zeros_like(y_ref)


def _experts(xs, tile_e, tile_valid, wg, wu, wd, layer, tm):
    w_in_spec = pl.BlockSpec((None, None, D_MODEL, D_FF_E), lambda t, te, tv: (layer, te[t], 0, 0))
    nt = xs.shape[0] // tm
    return pl.pallas_call(
        _expert_kernel,
        grid_spec=pltpu.PrefetchScalarGridSpec(
            num_scalar_prefetch=2,
            grid=(nt,),
            in_specs=[
                pl.BlockSpec((tm, D_MODEL), lambda t, te, tv: (t, 0)),
                w_in_spec,
                w_in_spec,
                pl.BlockSpec((None, None, D_FF_E, D_MODEL), lambda t, te, tv: (layer, te[t], 0, 0)),
            ],
            out_specs=pl.BlockSpec((tm, D_MODEL), lambda t, te, tv: (t, 0)),
        ),
        out_shape=jax.ShapeDtypeStruct((nt * tm, D_MODEL), F32),
        compiler_params=pltpu.CompilerParams(
            dimension_semantics=("arbitrary",), vmem_limit_bytes=VMEM_LIMIT),
        name="experts",
    )(tile_e, tile_valid, xs, wg, wu, wd)


def _combine_kernel(split_tile, d_ref, d_next_ref, y_hbm, x_ref, wt_ref, fnw_ref, *rest):
    out_refs, (ybuf, sem) = rest[:-2], rest[-2:]
    t = pl.program_id(0)
    nt = pl.num_programs(0)
    tm = x_ref.shape[0]
    slot = t & 1

    @pl.when(t == 0)
    def _():
        _row_gather(d_ref, y_hbm, ybuf, sem, 0, 2 * tm)

    @pl.when(t + 1 < nt)
    def _():
        _row_gather(d_next_ref, y_hbm, ybuf, sem, 1 - slot, 2 * tm)

    _row_gather_wait(y_hbm, ybuf, sem, slot, 2 * tm)
    wt = wt_ref[...]
    out = x_ref[...] + wt[:, 0:1] * ybuf[slot, 0:tm, :] + wt[:, 1:2] * ybuf[slot, tm:2 * tm, :]
    if split_tile is None:
        out_refs[0][...] = out
    else:
        res = _rms(out, fnw_ref[...])

        @pl.when(t < split_tile)
        def _():
            out_refs[0][...] = res

        @pl.when(t >= split_tile)
        def _():
            out_refs[1][...] = res


def _combine(x, y, dest, wt, fnw, split_rows, tm):
    n = x.shape[0]
    nt = n // tm
    d = _slot_major(dest, tm)
    if split_rows is None:
        split_tile = None
        out_specs = pl.BlockSpec((tm, D_MODEL), lambda t: (t, 0))
        out_shape = jax.ShapeDtypeStruct((n, D_MODEL), F32)
    else:
        split_tile = split_rows // tm
        assert split_tile * tm == split_rows and 0 < split_tile < nt
        out_specs = [pl.BlockSpec((tm, D_MODEL), lambda t: (jnp.minimum(t, split_tile - 1), 0)),
                     pl.BlockSpec((tm, D_MODEL), lambda t: (jnp.maximum(t - split_tile, 0), 0))]
        out_shape = [jax.ShapeDtypeStruct((split_rows, D_MODEL), F32),
                     jax.ShapeDtypeStruct((n - split_rows, D_MODEL), F32)]
    return pl.pallas_call(
        functools.partial(_combine_kernel, split_tile),
        grid=(nt,),
        in_specs=[
            pl.BlockSpec((None, 1, 2 * tm), lambda t: (t, 0, 0), memory_space=pltpu.SMEM),
            pl.BlockSpec((None, 1, 2 * tm), lambda t: (jnp.minimum(t + 1, nt - 1), 0, 0),
                         memory_space=pltpu.SMEM),
            pl.BlockSpec(memory_space=pl.ANY),
            pl.BlockSpec((tm, D_MODEL), lambda t: (t, 0)),
            pl.BlockSpec((tm, LANES), lambda t: (t, 0)),
            pl.BlockSpec((1, D_MODEL), lambda t: (0, 0)),
        ],
        out_specs=out_specs,
        out_shape=out_shape,
        scratch_shapes=[pltpu.VMEM((2, 2 * tm, D_MODEL), F32), pltpu.SemaphoreType.DMA((2,))],
        compiler_params=pltpu.CompilerParams(
            dimension_semantics=("arbitrary",), vmem_limit_bytes=VMEM_LIMIT),
        name="combine",
    )(d, d, y, x, wt, fnw)


def _moe_rows(n, tm_exp):
    return (2 * n // tm_exp + N_EXPERTS) * tm_exp


def _moe(x, nw, wr, br, wg, wu, wd, layer, fnw, xs_buf, split_rows, tm_tok, tm_exp, tm_cmb):
    xn, idx, wt, cnt = _router(x, nw, wr, br, tm_tok)
    dest, tile_e, tile_valid = _moe_plan(idx, cnt, tm_tok, tm_exp)
    xs = _dispatch(xn, dest, xs_buf, tm_tok)
    y = _experts(xs, tile_e, tile_valid, wg, wu, wd, layer, tm_exp)
    return _combine(x, y, dest, wt, fnw, split_rows, tm_cmb), xs


def _seq_masks(n, lseq):
    r = lax.broadcasted_iota(jnp.int32, (n, n), 0)
    c = lax.broadcasted_iota(jnp.int32, (n, n), 1)
    sh = lseq.bit_length() - 1
    same = (r >> sh) == (c >> sh)
    return same & (r >= c), same & (r > c), r == c


def _valid_rows(n, lseq, t_lo, t_hi):
    row = lax.broadcasted_iota(jnp.int32, (n, 1), 0)
    t = (row & (lseq - 1)) + pl.program_id(1) * lseq
    return (t >= t_lo) & (t < t_hi)


def _pad_rows(a, n):
    return jnp.concatenate([a, jnp.zeros((n - a.shape[0], a.shape[1]), a.dtype)], axis=0)


def _gdn_kernel(Lseq, NS, t_lo, t_hi, conv_row0,
                qkv_ref, z_ref, sm_ref, cin_ref, sin_ref, cw_ref, hp_ref, na_ref, obuf_ref, sbuf_ref,
                o_ref, cout_ref, so_ref, ext_ref):
    del obuf_ref, sbuf_ref
    R = NS * Lseq
    R2 = 2 * R
    npairs = H_A // 2

    @pl.when(pl.program_id(1) == 0)
    def _():
        ext_ref[:, 0:SUBLANES, :] = cin_ref[...]
        so_ref[...] = sin_ref[...]

    valid = _valid_rows(R, Lseq, t_lo, t_hi)
    valid_l = _valid_rows(Lseq, Lseq, t_lo, t_hi)
    for s in range(NS):
        ext_ref[s, SUBLANES:SUBLANES + Lseq, :] = jnp.where(
            valid_l, qkv_ref[s * Lseq:(s + 1) * Lseq, :], 0.0)

    def conv_act(col0):
        pieces = []
        for s in range(NS):
            acc = None
            for i in range(CONV_W):
                r0 = SUBLANES - (CONV_W - 1) + i
                term = cw_ref[i:i + 1, col0:col0 + LANES] * ext_ref[s, r0:r0 + Lseq, col0:col0 + LANES]
                acc = term if acc is None else acc + term
            pieces.append(acc)
        return _silu(pieces[0] if NS == 1 else jnp.concatenate(pieces, axis=0))

    incl, _, _ = _seq_masks(R, Lseq)
    incl2, strict2, diag2 = _seq_masks(R2, Lseq)
    tri = jnp.where(incl, 1.0, 0.0).astype(BF16)
    eye2 = jnp.where(diag2, 1.0, 0.0)

    sm = sm_ref[...]
    beta_all = jnp.where(valid, _sigmoid(sm), 0.0)
    g_all = jnp.where(valid, -jnp.exp(hp_ref[0:1, :]) * _softplus(sm + hp_ref[1:2, :]), 0.0)
    G_all = _mask_dot(tri, g_all)
    GT = jnp.transpose(_pad_rows(G_all, LANES))

    q2, k2, v2, beta2, Gc2, Gr2 = [], [], [], [], [], []
    for p in range(npairs):
        qs, ks, vs = [], [], []
        for h in (2 * p, 2 * p + 1):
            q = conv_act(h * DK_A)
            k = conv_act(H_A * DK_A + h * DK_A)
            qs.append(q * lax.rsqrt(jnp.sum(q * q, axis=-1, keepdims=True) + EPS) * (DK_A ** -0.5))
            ks.append(jnp.where(valid, k * lax.rsqrt(jnp.sum(k * k, axis=-1, keepdims=True) + EPS), 0.0))
            vs.append(conv_act(2 * H_A * DK_A + h * DV_A))
        q2.append(jnp.concatenate(qs, axis=0))
        k2.append(jnp.concatenate(ks, axis=0))
        v2.append(jnp.concatenate(vs, axis=0))
        h0 = 2 * p
        beta2.append(jnp.concatenate([beta_all[:, S_BETA + h0:S_BETA + h0 + 1],
                                      beta_all[:, S_BETA + h0 + 1:S_BETA + h0 + 2]], axis=0))
        Gc2.append(jnp.concatenate([G_all[:, S_DECAY + h0:S_DECAY + h0 + 1],
                                    G_all[:, S_DECAY + h0 + 1:S_DECAY + h0 + 2]], axis=0))
        Gr2.append(jnp.concatenate([GT[S_DECAY + h0:S_DECAY + h0 + 1, :R],
                                    GT[S_DECAY + h0 + 1:S_DECAY + h0 + 2, :R]], axis=1))

    rng = range(npairs)
    kk = [_bdot_nt(k2[p], k2[p]) for p in rng]
    qk = [_bdot_nt(q2[p], k2[p]) for p in rng]
    dec = [jnp.where(incl2, jnp.exp(jnp.where(incl2, Gc2[p] - Gr2[p], 0.0)), 0.0) for p in rng]
    Bp = [-jnp.where(strict2, beta2[p] * kk[p] * dec[p], 0.0) for p in rng]
    P = [eye2 + Bp[p] for p in rng]
    n = 2
    while n < Lseq:
        Bp = [_dot3(b, b) for b in Bp]
        P = [pp + _dot3(pp, b) for pp, b in zip(P, Bp)]
        n *= 2
    eG = [jnp.exp(Gc2[p]) for p in rng]
    rhs = [jnp.concatenate([beta2[p] * v2[p], (beta2[p] * eG[p]) * k2[p]], axis=1) for p in rng]
    sol = [_dot3(P[p], rhs[p]) for p in rng]
    qk = [qk[p] * dec[p] for p in rng]
    qe = [q2[p] * eG[p] for p in rng]

    items = [(p, j, s) for p in rng for j in range(2) for s in range(NS)]

    def rows(j, s):
        return slice(j * R + s * Lseq, j * R + (s + 1) * Lseq)

    S = {it: so_ref[it[2], 2 * it[0] + it[1]] for it in items}
    both = {}
    for p, j, s in items:
        r = rows(j, s)
        both[(p, j, s)] = _bdot(jnp.concatenate([sol[p][r, DV_A:], qe[p][r]], axis=0), S[(p, j, s)])
    vnew = {}
    for p, j, s in items:
        vnew[(p, j, s)] = sol[p][rows(j, s), :DV_A] - both[(p, j, s)][:Lseq]
    for p, j, s in items:
        r = rows(j, s)
        last = j * R + (s + 1) * Lseq - 1
        GL = Gc2[p][last:last + 1]
        kd = k2[p][r] * jnp.exp(GL - Gc2[p][r])
        so_ref[s, 2 * p + j] = S[(p, j, s)] * jnp.exp(GL) + _bdot_tn(kd, vnew[(p, j, s)])
    for p in rng:
        order = [(p, j, s) for j in range(2) for s in range(NS)]
        vnew2 = jnp.concatenate([vnew[it] for it in order], axis=0)
        o2 = jnp.concatenate([both[it][Lseq:] for it in order], axis=0) + _bdot(qk[p], vnew2)
        for j in range(2):
            h = 2 * p + j
            zh = z_ref[:, h * DV_A:(h + 1) * DV_A]
            o_ref[:, h * DV_A:(h + 1) * DV_A] = _rms(o2[j * R:(j + 1) * R], na_ref[...]) * _silu(zh)

    for s in range(NS):
        cout_ref[s] = ext_ref[s, conv_row0:conv_row0 + SUBLANES, :]
        ext_ref[s, 0:SUBLANES, :] = ext_ref[s, Lseq:Lseq + SUBLANES, :]


def _gdn_parts(big, small, cin, sin, layer_in, cw, hp, na, obuf, sbuf, layer, NS, nseq, rb):
    return dict(
        operands=(big, big, small, cin, sin, cw, hp, na, obuf, sbuf),
        in_specs=[
            pl.BlockSpec((ROWS, CONV_CH), lambda b, c: (rb(b, c), 0)),
            pl.BlockSpec((ROWS, H_A * DV_A), lambda b, c: (rb(b, c), C_Z // (H_A * DV_A))),
            pl.BlockSpec((ROWS, LANES), lambda b, c: (rb(b, c), 0)),
            pl.BlockSpec((None, NS, SUBLANES, CONV_CH), lambda b, c: (layer_in, b, 0, 0)),
            pl.BlockSpec((None, NS, H_A, DK_A, DV_A), lambda b, c: (layer_in, b, 0, 0, 0)),
            pl.BlockSpec((SUBLANES, CONV_CH), lambda b, c: (0, 0)),
            pl.BlockSpec((SUBLANES, LANES), lambda b, c: (0, 0)),
            pl.BlockSpec((1, DV_A), lambda b, c: (0, 0)),
            pl.BlockSpec(memory_space=pl.ANY),
            pl.BlockSpec(memory_space=pl.ANY),
        ],
        out_specs=[
            pl.BlockSpec((ROWS, H_A * DV_A), lambda b, c: (rb(b, c), 0)),
            pl.BlockSpec((NS, SUBLANES, CONV_CH), lambda b, c: (b, 0, 0)),
            pl.BlockSpec((None, NS, H_A, DK_A, DV_A), lambda b, c: (layer, b, 0, 0, 0)),
        ],
        out_shape=[
            jax.ShapeDtypeStruct(obuf.shape, F32),
            jax.ShapeDtypeStruct((nseq, SUBLANES, CONV_CH), F32),
            jax.ShapeDtypeStruct(sbuf.shape, F32),
        ],
        aliases={8: 0, 9: 2},
    )


def _gla_kernel(Lseq, NS, SB, t_lo, t_hi,
                qb_ref, kb_ref, vb_ref, rb_ref, sm_ref, sin_ref, wup_ref, bup_ref, nb_ref, obuf_ref,
                sbuf_ref, o_ref, so_ref):
    del obuf_ref, sbuf_ref
    R = NS * Lseq
    heads = range(H_B)

    @pl.when(pl.program_id(1) == 0)
    def _():
        so_ref[...] = sin_ref[...]

    nb = Lseq // SB
    assert nb == 1 or NS == 1
    n_off = SB * nb * (nb - 1) // 2
    sb_shift = SB.bit_length() - 1

    valid = _valid_rows(R, Lseq, t_lo, t_hi)
    r = lax.broadcasted_iota(jnp.int32, (R, R), 0)
    c = lax.broadcasted_iota(jnp.int32, (R, R), 1)
    same = (r | (Lseq - 1)) == (c | (Lseq - 1))
    m_row = jnp.where(same & (c <= r), 1.0, 0.0)
    m_first = jnp.where(same & (c <= (r & ~(SB - 1))), 1.0, 0.0)
    m_last = jnp.where(same, 1.0, 0.0)
    masks = jnp.concatenate([m_row, m_first, m_last], axis=0).astype(BF16)

    pre = _dot3(sm_ref[...], wup_ref[...]) + bup_ref[...]
    lg = jnp.where(valid, -_softplus(-pre) / GATE_TAU, 0.0)
    cums = _mask_dot(masks, lg)
    Bc_all, b0_all, BL_all = cums[:R], cums[R:2 * R], cums[2 * R:]

    re = lax.broadcasted_iota(jnp.int32, (R, n_off + R), 0)
    ce = lax.broadcasted_iota(jnp.int32, (R, n_off + R), 1)
    rblk = (re & (Lseq - 1)) >> sb_shift
    cd = ce - n_off
    see = ((ce >= (SB // 2) * rblk * (rblk - 1)) & (ce < (SB // 2) * rblk * (rblk + 1))) | (
        (cd >= 0) & ((cd >> sb_shift) == (re >> sb_shift)) & (cd <= re))

    def hs(a, h, w):
        return a[:, h * w:(h + 1) * w]

    q = [hs(qb_ref, h, DK_B) * (DK_B ** -0.5) for h in heads]
    k = [jnp.where(valid, hs(kb_ref, h, DK_B), 0.0) for h in heads]
    v = [hs(vb_ref, h, DV_B) for h in heads]
    Bc = [hs(Bc_all, h, DK_B) for h in heads]
    b0 = [hs(b0_all, h, DK_B) for h in heads]
    BL = [hs(BL_all, h, DK_B) for h in heads]
    qs = [q[h] * jnp.exp(Bc[h] - b0[h]) for h in heads]
    kx, vx = [], []
    for h in heads:
        kparts = [k[h][:i * SB] * jnp.exp(Bc[h][i * SB:i * SB + 1] - Bc[h][:i * SB]) for i in range(1, nb)]
        kparts.append(k[h] * jnp.exp(b0[h] - Bc[h]))
        vparts = [v[h][:i * SB] for i in range(1, nb)] + [v[h]]
        kx.append(kparts[0] if nb == 1 else jnp.concatenate(kparts, axis=0))
        vx.append(vparts[0] if nb == 1 else jnp.concatenate(vparts, axis=0))
    Ax = [jnp.where(see, _dot3(qs[h], kx[h], _dot_nt), 0.0) for h in heads]
    o_intra = [_bdot(Ax[h], vx[h]) for h in heads]
    qe = [q[h] * jnp.exp(Bc[h]) for h in heads]
    kd = [k[h] * jnp.exp(BL[h] - Bc[h]) for h in heads]
    ET = [jnp.transpose(_pad_rows(jnp.exp(BL[h]), LANES)) for h in heads]

    items = [(h, s) for h in heads for s in range(NS)]
    S = {it: so_ref[it[1], it[0]] for it in items}
    o_inter = {}
    for h, s in items:
        o_inter[(h, s)] = _bdot(qe[h][s * Lseq:(s + 1) * Lseq], S[(h, s)])
    for h, s in items:
        rows = slice(s * Lseq, (s + 1) * Lseq)
        so_ref[s, h] = S[(h, s)] * ET[h][:, s * Lseq:s * Lseq + 1] + _bdot_tn(kd[h][rows], v[h][rows])
    for h in heads:
        parts = [o_inter[(h, s)] for s in range(NS)]
        o = (parts[0] if NS == 1 else jnp.concatenate(parts, axis=0)) + o_intra[h]
        rh = hs(rb_ref, h, DV_B)
        o_ref[:, h * DV_B:(h + 1) * DV_B] = _rms(o, nb_ref[...]) * _silu(rh)


def _gla_parts(big, small, sin, layer_in, wup, bup, nb, obuf, sbuf, layer, NS, r):
    return dict(
        operands=(big, big, big, big, small, sin, wup, bup, nb, obuf, sbuf),
        in_specs=[
            pl.BlockSpec((ROWS, H_B * DK_B), lambda b, c: (r(b, c), C_QB // (H_B * DK_B))),
            pl.BlockSpec((ROWS, H_B * DK_B), lambda b, c: (r(b, c), C_KB // (H_B * DK_B))),
            pl.BlockSpec((ROWS, H_B * DV_B), lambda b, c: (r(b, c), C_VB // (H_B * DV_B))),
            pl.BlockSpec((ROWS, H_B * DV_B), lambda b, c: (r(b, c), C_RB // (H_B * DV_B))),
            pl.BlockSpec((ROWS, LANES), lambda b, c: (r(b, c), 0)),
            pl.BlockSpec((None, NS, H_B, DK_B, DV_B), lambda b, c: (layer_in, b, 0, 0, 0)),
            pl.BlockSpec((LANES, H_B * DK_B), lambda b, c: (0, 0)),
            pl.BlockSpec((1, H_B * DK_B), lambda b, c: (0, 0)),
            pl.BlockSpec((1, DV_B), lambda b, c: (0, 0)),
            pl.BlockSpec(memory_space=pl.ANY),
            pl.BlockSpec(memory_space=pl.ANY),
        ],
        out_specs=[
            pl.BlockSpec((ROWS, H_B * DV_B), lambda b, c: (r(b, c), 0)),
            pl.BlockSpec((None, NS, H_B, DK_B, DV_B), lambda b, c: (layer, b, 0, 0, 0)),
        ],
        out_shape=[
            jax.ShapeDtypeStruct(obuf.shape, F32),
            jax.ShapeDtypeStruct(sbuf.shape, F32),
        ],
        aliases={9: 0, 10: 1},
    )


def _mixer_kernel(Lseq, NS, SB, t_lo, t_hi, conv_row0, n_in_a, n_in_b, n_out_a, *refs):
    in_a, in_b = refs[:n_in_a], refs[n_in_a:n_in_a + n_in_b]
    outs = refs[n_in_a + n_in_b:-1]
    _gdn_kernel(Lseq, NS, t_lo, t_hi, conv_row0, *in_a, *outs[:n_out_a], refs[-1])
    _gla_kernel(Lseq, NS, SB, t_lo, t_hi, *in_b, *outs[n_out_a:])


def _mixer(gdn, gla, *, nsteps, Lseq, NS, SB, nchunks, t_lo, t_hi):
    assert NS * Lseq == ROWS
    n_in_a, n_out_a = len(gdn["operands"]), len(gdn["out_specs"])
    aliases = dict(gdn["aliases"])
    aliases.update({n_in_a + i: n_out_a + o for i, o in gla["aliases"].items()})
    conv_row0 = t_hi - (nchunks - 1) * Lseq
    kern = functools.partial(_mixer_kernel, Lseq, NS, SB, t_lo, t_hi, conv_row0,
                             n_in_a, len(gla["operands"]), n_out_a)
    return pl.pallas_call(
        kern,
        grid=(nsteps, nchunks),
        in_specs=gdn["in_specs"] + gla["in_specs"],
        out_specs=gdn["out_specs"] + gla["out_specs"],
        out_shape=gdn["out_shape"] + gla["out_shape"],
        scratch_shapes=[pltpu.VMEM((NS, SUBLANES + Lseq, CONV_CH), F32)],
        input_output_aliases=aliases,
        compiler_params=pltpu.CompilerParams(
            dimension_semantics=("parallel", "arbitrary"), vmem_limit_bytes=VMEM_LIMIT),
        name="mixer_L%d" % Lseq,
    )(*gdn["operands"], *gla["operands"])


def kernel(x_prompt, x_sample, state_delta, state_conv, state_gla, meta_tokens, norm_mix, w_in, conv_w, a_log, dt_bias, norm_a, gla_gate_up, gla_gate_bias, norm_b, w_proj_a, w_proj_b, w_out, norm_ffn, router_group_w, router_group_b, router_expert_w, router_expert_b, expert_w_gate, expert_w_up, expert_w_down, norm_final):
    Bp, Tp, D = x_prompt.shape
    Bs, Ts, _ = x_sample.shape
    depth = w_in.shape[0]
    assert N_META <= CHUNK and Tp % CHUNK == 0 and CHUNK == ROWS
    pad_p = CHUNK - N_META
    Tpp = CHUNK + Tp
    ncp = Tpp // CHUNK
    n_prompt = Bp * Tp
    n_head = Bp * CHUNK
    ns_rows = Bs * SAMPLE_L
    n = n_prompt + n_head + ns_rows
    seq_per_step = ROWS // SAMPLE_L
    dt = x_prompt.dtype
    head = jnp.concatenate([
        jnp.zeros((Bp, pad_p, D), dt),
        jnp.broadcast_to(meta_tokens.astype(dt)[None], (Bp, N_META, D))], axis=1).reshape(n_head, D)
    xs = jnp.pad(x_sample, ((0, 0), (0, SAMPLE_L - Ts), (0, 0))).reshape(ns_rows, D)
    x = jnp.concatenate([x_prompt.reshape(n_prompt, D), head, xs.astype(dt)], axis=0).astype(F32)

    tm_big = 1280 if n % 1280 == 0 else 512
    tm = 512

    zeros_conv = jnp.zeros((1, Bp, SUBLANES, CONV_CH), F32)
    zeros_sa = jnp.zeros((1, Bp, H_A, DK_A, DV_A), F32)
    zeros_sb = jnp.zeros((1, Bp, H_B, DK_B, DV_B), F32)
    head_block0 = n_prompt // ROWS
    sample_block0 = (n_prompt + n_head) // ROWS
    prompt_args = dict(nsteps=Bp, Lseq=CHUNK, NS=1, nchunks=ncp, t_lo=pad_p, t_hi=Tpp)
    sample_args = dict(nsteps=Bs // seq_per_step, Lseq=SAMPLE_L, NS=seq_per_step, nchunks=1, t_lo=0, t_hi=Ts)
    prompt_rb = lambda b, c: jnp.where(c == 0, head_block0 + b, b * (ncp - 1) + c - 1)
    sample_rb = lambda b, c: sample_block0 + b

    wbig_all, wsmall_all = _wprep(w_in)
    wg_all, wu_all, wd_all = expert_w_gate.astype(F32), expert_w_up.astype(F32), expert_w_down.astype(F32)
    wa_all, wb_all, wo_all = w_proj_a.astype(BF16), w_proj_b.astype(BF16), w_out.astype(BF16)
    oa = jnp.zeros((n, H_A * DV_A), F32)
    ob = jnp.zeros((n, H_B * DV_B), F32)
    xs_buf = jnp.zeros((_moe_rows(n, MOE_TILE), D), F32)
    sa_p = jnp.zeros((depth, Bp, H_A, DK_A, DV_A), F32)
    sb_p = jnp.zeros((depth, Bp, H_B, DK_B, DV_B), F32)
    sa_s = jnp.zeros((depth, Bs, H_A, DK_A, DV_A), F32)
    sb_s = jnp.zeros((depth, Bs, H_B, DK_B, DV_B), F32)
    sa_in, sb_in = state_delta.astype(F32), state_gla.astype(F32)
    cin_s = jnp.pad(state_conv.astype(F32), ((0, 0), (0, 0), (SUBLANES - (CONV_W - 1), 0), (0, 0)))

    conv_p, conv_s = [], []
    for l in range(depth):
        big, small = _inproj(x, norm_mix[l][None].astype(F32), wbig_all, wsmall_all, l, tm_big, 1536)

        cw = jnp.pad(conv_w[l].astype(F32), ((0, SUBLANES - CONV_W), (0, 0)))
        hp = jnp.zeros((SUBLANES, LANES), F32)
        hp = hp.at[0, S_DECAY:S_DECAY + H_A].set(a_log[l].astype(F32))
        hp = hp.at[1, S_DECAY:S_DECAY + H_A].set(dt_bias[l].astype(F32))
        na = norm_a[l][None].astype(F32)
        wup = jnp.zeros((LANES, H_B * DK_B), F32).at[S_LR:S_LR + GATE_RANK].set(gla_gate_up[l].astype(F32))
        bup = gla_gate_bias[l][None].astype(F32)
        nb = norm_b[l][None].astype(F32)

        oa, c_p, sa_p, ob, sb_p = _mixer(
            _gdn_parts(big, small, zeros_conv, zeros_sa, 0, cw, hp, na, oa, sa_p, l, 1, Bp, prompt_rb),
            _gla_parts(big, small, zeros_sb, 0, wup, bup, nb, ob, sb_p, l, 1, prompt_rb),
            SB=GLA_SUB, **prompt_args)
        oa, c_s, sa_s, ob, sb_s = _mixer(
            _gdn_parts(big, small, cin_s, sa_in, l, cw, hp, na, oa, sa_s, l, seq_per_step, Bs, sample_rb),
            _gla_parts(big, small, sb_in, l, wup, bup, nb, ob, sb_s, l, seq_per_step, sample_rb),
            SB=SAMPLE_L, **sample_args)
        x = _outproj(oa, ob, big, x, wa_all, wb_all, wo_all, l, tm)

        wr = jnp.concatenate([router_expert_w[l], router_group_w[l],
                              jnp.zeros((D, LANES - N_EXPERTS - N_GROUPS), F32)], axis=1).astype(F32)
        br = jnp.concatenate([router_expert_b[l], router_group_b[l],
                              jnp.zeros((LANES - N_EXPERTS - N_GROUPS,), F32)])[None].astype(F32)
        x, xs_buf = _moe(x, norm_ffn[l][None].astype(F32), wr, br, wg_all, wu_all, wd_all, l,
                         norm_final[None].astype(F32), xs_buf,
                         n_prompt if l == depth - 1 else None, tm, MOE_TILE, COMBINE_TILE)

        conv_p.append(c_p[:, SUBLANES - (CONV_W - 1):])
        conv_s.append(c_s[:, SUBLANES - (CONV_W - 1):])

    y_first, y_rest = x
    y_prompt = y_first.reshape(Bp, Tp, D).astype(dt)
    y_sample = y_rest[n_head:].reshape(Bs, SAMPLE_L, D)[:, :Ts].astype(x_sample.dtype)
    return (y_prompt, y_sample,
            sa_p.astype(dt), jnp.stack(conv_p).astype(dt), sb_p.astype(dt),
            sa_s.astype(state_delta.dtype), jnp.stack(conv_s).astype(state_conv.dtype),
            sb_s.astype(state_gla.dtype))
```
